```python
import math
import jax, jax.numpy as jnp
from jax import lax
import numpy as np

D_MODEL = 1024
BATCH = 8
SEQ = 4096
DEPTH = 2

D_INNER = 2 * D_MODEL
SSM_WIDTH = D_INNER // 2
RET_WIDTH = D_INNER - SSM_WIDTH
SSM_HEAD_DIM = 64
SSM_HEADS = SSM_WIDTH // SSM_HEAD_DIM
SSM_GROUPS = 2
SSM_STATE = 128
CONV_K = 4
CONV_DIM = SSM_WIDTH + 2 * SSM_GROUPS * SSM_STATE
SSM_CHUNK = 128
RET_HEADS = 4
RET_V_DIM = RET_WIDTH // RET_HEADS
RET_QK_DIM = RET_V_DIM // 2
RET_CHUNK = 128
ROPE_BASE = 10000.0
D_FF = 2816
IN_PROJ_DIM = SSM_WIDTH + CONV_DIM + SSM_HEADS + 2 * RET_HEADS * RET_QK_DIM + 2 * RET_WIDTH
NORM_EPS = 1e-6

kernel_name = "hybrid_ssd_retention_macaron_sandwich"


def rms_norm(x, w):
    xf = x.astype(jnp.float32)
    y = xf * lax.rsqrt(jnp.mean(xf * xf, axis=-1, keepdims=True) + NORM_EPS)
    return (y * w.astype(jnp.float32)).astype(x.dtype)


def swiglu(x, w_in, w_out):
    gate, up = jnp.split(x @ w_in, 2, axis=-1)
    return (jax.nn.silu(gate) * up) @ w_out


def to_chunks(a, ch):
    b, s = a.shape[:2]
    return jnp.moveaxis(a.reshape(b, s // ch, ch, *a.shape[2:]), 1, 0)


def from_chunks(a):
    a = jnp.moveaxis(a, 0, 1)
    return a.reshape(a.shape[0], a.shape[1] * a.shape[2], *a.shape[3:])


def causal_depthwise_conv(x, w, bias):
    y = lax.conv_general_dilated(
        x, w[:, None, :].astype(x.dtype), window_strides=(1,),
        padding=[(CONV_K - 1, 0)], dimension_numbers=("NWC", "WIO", "NWC"),
        feature_group_count=x.shape[-1])
    return y + bias.astype(x.dtype)


def ssd_chunked(xdt, dta, bm, cm):
    b, s = xdt.shape[:2]
    e = SSM_HEADS // SSM_GROUPS
    xs = to_chunks(xdt.reshape(b, s, SSM_GROUPS, e, SSM_HEAD_DIM), SSM_CHUNK)
    as_ = to_chunks(dta.reshape(b, s, SSM_GROUPS, e), SSM_CHUNK)
    bs = to_chunks(bm, SSM_CHUNK)
    cs = to_chunks(cm, SSM_CHUNK)
    causal = jnp.tril(jnp.ones((SSM_CHUNK, SSM_CHUNK), dtype=bool))[None, :, :, None, None]

    def step(state, inp):
        xc, ac, bc, cc = inp
        acum = jnp.cumsum(ac, axis=1)
        diff = acum[:, :, None] - acum[:, None, :]
        lmat = jnp.exp(jnp.where(causal, diff, -jnp.inf))
        cb = jnp.einsum("btgn,bsgn->btsg", cc, bc)
        y_intra = jnp.einsum("btsg,btsge,bsgep->btgep", cb, lmat, xc)
        y_inter = jnp.einsum("btgn,bgepn->btgep", cc, state) * jnp.exp(acum)[..., None]
        decay_to_end = jnp.exp(acum[:, -1:] - acum)
        new_state = state * jnp.exp(acum[:, -1])[..., None, None] + jnp.einsum(
            "bsgn,bsge,bsgep->bgepn", bc, decay_to_end, xc)
        return new_state, y_intra + y_inter

    init = jnp.zeros((b, SSM_GROUPS, e, SSM_HEAD_DIM, SSM_STATE), jnp.float32)
    _, ys = lax.scan(step, init, (xs, as_, bs, cs))
    return from_chunks(ys).reshape(b, s, SSM_HEADS, SSM_HEAD_DIM)


def retention_chunked(q, k, v):
    b = q.shape[0]
    log_gamma = jnp.log(1.0 - 2.0 ** (-5.0 - jnp.arange(RET_HEADS, dtype=jnp.float32)))
    pos = jnp.arange(RET_CHUNK, dtype=jnp.float32)
    rel = pos[:, None] - pos[None, :]
    dmat = jnp.exp(jnp.where((rel >= 0)[None], rel[None] * log_gamma[:, None, None], -jnp.inf))
    q_decay = jnp.exp((pos[:, None] + 1.0) * log_gamma[None])
    k_decay = jnp.exp((RET_CHUNK - 1.0 - pos[:, None]) * log_gamma[None])
    chunk_decay = jnp.exp(RET_CHUNK * log_gamma)

    def step(state, inp):
        qc, kc, vc = inp
        scores = jnp.einsum("bthd,bshd->bhts", qc, kc) * dmat
        y = jnp.einsum("bhts,bshe->bthe", scores, vc) + jnp.einsum(
            "bthd,bhde->bthe", qc, state) * q_decay[None, :, :, None]
        new_state = state * chunk_decay[None, :, None, None] + jnp.einsum(
            "bshd,bshe,sh->bhde", kc, vc, k_decay)
        return new_state, y

    init = jnp.zeros((b, RET_HEADS, RET_QK_DIM, RET_V_DIM), jnp.float32)
    _, ys = lax.scan(step, init, (to_chunks(q, RET_CHUNK), to_chunks(k, RET_CHUNK), to_chunks(v, RET_CHUNK)))
    return from_chunks(ys)


def rotary(x, positions):
    half = x.shape[-1] // 2
    inv_freq = ROPE_BASE ** (-jnp.arange(half, dtype=jnp.float32) / half)
    ang = positions.astype(jnp.float32)[..., None] * inv_freq
    cos = jnp.cos(ang)[:, :, None, :]
    sin = jnp.sin(ang)[:, :, None, :]
    x1, x2 = x[..., :half], x[..., half:]
    return jnp.concatenate([x1 * cos - x2 * sin, x1 * sin + x2 * cos], axis=-1)


def hybrid_mixer(u, positions, w_in, conv_w, conv_b, dt_bias, a_log, d_skip, ssm_norm, ret_norm, w_out):
    b, s, _ = u.shape
    f32 = jnp.float32
    sizes = [SSM_WIDTH, CONV_DIM, SSM_HEADS, RET_HEADS * RET_QK_DIM, RET_HEADS * RET_QK_DIM, RET_WIDTH, RET_WIDTH]
    split_at = np.cumsum(sizes)[:-1].tolist()
    z, xbc, dt, q, k, v, g = jnp.split(u @ w_in, split_at, axis=-1)

    xbc = jax.nn.silu(causal_depthwise_conv(xbc, conv_w, conv_b))
    xs, bm, cm = jnp.split(xbc.astype(f32), [SSM_WIDTH, SSM_WIDTH + SSM_GROUPS * SSM_STATE], axis=-1)
    xs = xs.reshape(b, s, SSM_HEADS, SSM_HEAD_DIM)
    bm = bm.reshape(b, s, SSM_GROUPS, SSM_STATE)
    cm = cm.reshape(b, s, SSM_GROUPS, SSM_STATE)
    dt = jax.nn.softplus(dt.astype(f32) + dt_bias.astype(f32))
    a = -jnp.exp(a_log.astype(f32))
    y = ssd_chunked(xs * dt[..., None], dt * a, bm, cm) + xs * d_skip.astype(f32)[:, None]
    y_ssm = rms_norm(y.reshape(b, s, SSM_WIDTH) * jax.nn.silu(z.astype(f32)), ssm_norm)

    q = rotary(q.astype(f32).reshape(b, s, RET_HEADS, RET_QK_DIM), positions)
    k = rotary(k.astype(f32).reshape(b, s, RET_HEADS, RET_QK_DIM), positions) * (RET_QK_DIM ** -0.5)
    v = v.astype(f32).reshape(b, s, RET_HEADS, RET_V_DIM)
    r = retention_chunked(q, k, v)
    mu = jnp.mean(r, axis=-1, keepdims=True)
    var = jnp.mean(jnp.square(r - mu), axis=-1, keepdims=True)
    r = (r - mu) * lax.rsqrt(var + 1e-5) * ret_norm.astype(f32).reshape(RET_HEADS, RET_V_DIM)
    y_ret = jax.nn.silu(g.astype(f32)) * r.reshape(b, s, RET_WIDTH)

    return jnp.concatenate([y_ssm.astype(f32), y_ret], axis=-1).astype(u.dtype) @ w_out


def setup_inputs(seed: int = 0) -> dict:
    key = jax.random.key(seed)
    ks = iter(jax.random.split(key, 32))
    f32 = jnp.float32
    L = DEPTH

    def normal(shape, scale):
        return jax.random.normal(next(ks), shape, f32) * scale

    def gain(shape):
        return 1.0 + 0.02 * jax.random.normal(next(ks), shape, f32)

    x = jax.random.normal(next(ks), (BATCH, SEQ, D_MODEL), f32)
    positions = jnp.broadcast_to(jnp.arange(SEQ, dtype=jnp.int32)[None], (BATCH, SEQ))
    dt0 = jnp.exp(jax.random.uniform(next(ks), (L, SSM_HEADS), f32, math.log(1e-3), math.log(1e-1)))
    dt_bias = dt0 + jnp.log(-jnp.expm1(-dt0))
    a_log = jnp.log(jax.random.uniform(next(ks), (L, SSM_HEADS), f32, 1.0, 16.0))
    return {
        "x": x,
        "positions": positions,
        "ffn1_pre_norm": gain((L, D_MODEL)),
        "ffn1_w_in": normal((L, D_MODEL, 2 * D_FF), D_MODEL ** -0.5),
        "ffn1_w_out": normal((L, D_FF, D_MODEL), D_FF ** -0.5),
        "ffn1_post_norm": gain((L, D_MODEL)),
        "mix_pre_norm": gain((L, D_MODEL)),
        "mix_w_in": normal((L, D_MODEL, IN_PROJ_DIM), D_MODEL ** -0.5),
        "conv_w": normal((L, CONV_K, CONV_DIM), CONV_K ** -0.5),
        "conv_b": normal((L, CONV_DIM), 0.02),
        "dt_bias": dt_bias,
        "a_log": a_log,
        "d_skip": gain((L, SSM_HEADS)),
        "ssm_norm": gain((L, SSM_WIDTH)),
        "ret_norm": gain((L, RET_WIDTH)),
        "mix_w_out": normal((L, D_INNER, D_MODEL), D_INNER ** -0.5),
        "mix_post_norm": gain((L, D_MODEL)),
        "ffn2_pre_norm": gain((L, D_MODEL)),
        "ffn2_w_in": normal((L, D_MODEL, 2 * D_FF), D_MODEL ** -0.5),
        "ffn2_w_out": normal((L, D_FF, D_MODEL), D_FF ** -0.5),
        "ffn2_post_norm": gain((L, D_MODEL)),
    }


def reference(x, positions, ffn1_pre_norm, ffn1_w_in, ffn1_w_out, ffn1_post_norm,
              mix_pre_norm, mix_w_in, conv_w, conv_b, dt_bias, a_log, d_skip, ssm_norm, ret_norm,
              mix_w_out, mix_post_norm, ffn2_pre_norm, ffn2_w_in, ffn2_w_out, ffn2_post_norm):
    for l in range(DEPTH):
        x = x + 0.5 * rms_norm(swiglu(rms_norm(x, ffn1_pre_norm[l]), ffn1_w_in[l], ffn1_w_out[l]), ffn1_post_norm[l])
        mixed = hybrid_mixer(rms_norm(x, mix_pre_norm[l]), positions, mix_w_in[l], conv_w[l], conv_b[l],
                             dt_bias[l], a_log[l], d_skip[l], ssm_norm[l], ret_norm[l], mix_w_out[l])
        x = x + rms_norm(mixed, mix_post_norm[l])
        x = x + 0.5 * rms_norm(swiglu(rms_norm(x, ffn2_pre_norm[l]), ffn2_w_in[l], ffn2_w_out[l]), ffn2_post_norm[l])
    return x
```

```python
import functools

import jax
import jax.numpy as jnp
import numpy as np
from jax import lax
from jax.experimental import pallas as pl
from jax.experimental.pallas import tpu as pltpu

F32 = jnp.float32
BF16 = jnp.bfloat16

D_MODEL = 1024
D_FF = 2816
NORM_EPS = 1e-6
RET_NORM_EPS = 1e-5

D_INNER = 2 * D_MODEL
SSM_WIDTH = D_INNER // 2
RET_WIDTH = D_INNER - SSM_WIDTH
SSM_HEAD_DIM = 64
SSM_HEADS = SSM_WIDTH // SSM_HEAD_DIM
SSM_GROUPS = 2
HEADS_PER_GROUP = SSM_HEADS // SSM_GROUPS
GROUP_WIDTH = HEADS_PER_GROUP * SSM_HEAD_DIM
SSM_STATE = 128
BC_WIDTH = SSM_GROUPS * SSM_STATE
CONV_K = 4
CONV_DIM = SSM_WIDTH + 2 * BC_WIDTH
RET_HEADS = 4
RET_V_DIM = RET_WIDTH // RET_HEADS
RET_QK_DIM = RET_V_DIM // 2
RET_QK_WIDTH = RET_HEADS * RET_QK_DIM
ROPE_BASE = 10000.0

LANES = 128
SUBLANES = 8

CHUNK = 128
MIX_BLOCK = 512
FFN_BLOCK = 512
FFN_COL_CHUNK = 1408
ROPE_BLOCK = 2048
VMEM_LIMIT = 56 * 1024 * 1024

_Z0, _Z1 = 0, SSM_WIDTH
_X0, _X1 = _Z1, _Z1 + CONV_DIM
_Q0, _Q1 = _X1, _X1 + RET_QK_WIDTH
_K0, _K1 = _Q1, _Q1 + RET_QK_WIDTH
_V0, _V1 = _K1, _K1 + RET_WIDTH
_G0, _G1 = _V1, _V1 + RET_WIDTH
MAIN_PROJ = _G1


def _resident(shape):
    nd = len(shape)
    return pl.BlockSpec(shape, lambda *_: (0,) * nd, pipeline_mode=pl.Buffered(1))


def _sigmoid(x):
    return 1.0 / (1.0 + jnp.exp(-x))


def _softplus(x):
    return jnp.maximum(x, 0.0) + jnp.log1p(jnp.exp(-jnp.abs(x)))


def _rms(x, w):
    ms = jnp.mean(x * x, axis=-1, keepdims=True)
    return x * lax.rsqrt(ms + NORM_EPS) * w


def _dot(a, b):
    return jnp.dot(a, b, preferred_element_type=F32)


def _dot_nt(a, b):
    return lax.dot_general(a, b, (((1,), (1,)), ((), ())), preferred_element_type=F32)


def _dot_exact(a, b):
    return jnp.dot(a, b, preferred_element_type=F32, precision=lax.Precision.HIGHEST)


def _rope_kernel(pos_ref, invf_ref, sign_ref, cos_ref, sin_ref):
    ang = pos_ref[...].astype(F32) * invf_ref[...]
    cos_ref[...] = jnp.cos(ang)
    sin_ref[...] = jnp.sin(ang) * sign_ref[...]


def _rope_tables(positions):
    t = positions.size
    tb = min(ROPE_BLOCK, t)
    half = RET_QK_DIM // 2
    inv_freq = ROPE_BASE ** (-jnp.arange(half, dtype=F32) / half)
    invf = jnp.concatenate([inv_freq, inv_freq])[None, :]
    sign = jnp.concatenate([-jnp.ones((half,), F32), jnp.ones((half,), F32)])[None, :]
    return pl.pallas_call(
        _rope_kernel,
        grid=(t // tb,),
        in_specs=[pl.BlockSpec((tb, 1), lambda i: (i, 0)),
                  pl.BlockSpec((1, LANES), lambda i: (0, 0)),
                  pl.BlockSpec((1, LANES), lambda i: (0, 0))],
        out_specs=[pl.BlockSpec((tb, LANES), lambda i: (i, 0)),
                   pl.BlockSpec((tb, LANES), lambda i: (i, 0))],
        out_shape=[jax.ShapeDtypeStruct((t, LANES), F32)] * 2,
        name="rope_tables",
    )(positions.reshape(t, 1), invf, sign)


def _ffn_kernel(x_ref, pre_ref, win_ref, wout_ref, post_ref, o_ref, xn_ref, act_ref):
    xn_ref[...] = _rms(x_ref[...], pre_ref[...]).astype(BF16)
    for lo in range(0, D_FF, FFN_COL_CHUNK):
        gate = _dot(xn_ref[...], win_ref[:, lo:lo + FFN_COL_CHUNK])
        up = _dot(xn_ref[...], win_ref[:, D_FF + lo:D_FF + lo + FFN_COL_CHUNK])
        act_ref[:, lo:lo + FFN_COL_CHUNK] = (gate * _sigmoid(gate) * up).astype(BF16)
    h = _dot(act_ref[...], wout_ref[...])
    o_ref[...] = x_ref[...] + 0.5 * _rms(h, post_ref[...])


def _ffn(x2d, pre, w_in, w_out, post):
    t = x2d.shape[0]
    tm = min(FFN_BLOCK, t)
    return pl.pallas_call(
        _ffn_kernel,
        grid=(t // tm,),
        in_specs=[pl.BlockSpec((tm, D_MODEL), lambda i: (i, 0)),
                  _resident((1, D_MODEL)),
                  _resident((D_MODEL, 2 * D_FF)),
                  _resident((D_FF, D_MODEL)),
                  _resident((1, D_MODEL))],
        out_specs=pl.BlockSpec((tm, D_MODEL), lambda i: (i, 0)),
        out_shape=jax.ShapeDtypeStruct((t, D_MODEL), F32),
        scratch_shapes=[pltpu.VMEM((tm, D_MODEL), BF16),
                        pltpu.VMEM((tm, D_FF), BF16)],
        compiler_params=pltpu.CompilerParams(
            dimension_semantics=("arbitrary",), vmem_limit_bytes=VMEM_LIMIT),
        name="ffn",
    )(x2d, pre[None, :], w_in.astype(BF16), w_out.astype(BF16), post[None, :])


def _expand_heads(v, e_ref):
    hi = v.astype(BF16)
    lo = (v - hi.astype(F32)).astype(BF16)
    return _dot(hi, e_ref[...]) + _dot(lo, e_ref[...])


def _mixer_kernel(x_ref, cos_ref, sin_ref, pre_ref, wmain_ref, wdt_ref, wdtT_ref, convw_ref, convb_ref,
                  dtb_row_ref, dtb_col_ref, alog_row_ref, alog_col_ref, dskip_ref, ssmnorm_ref,
                  retnorm_ref, wout_ref, post_ref, expand_ref, dmat_ref, qdec_ref, kdec_ref, cdec_ref,
                  o_ref,
                  u_ref, z_ref, xbc_ref, act_ref, q_ref, k_ref, v_ref, g_ref, dt_ref, dtT_ref,
                  y_ref, cat_ref, sstate_ref, rstate_ref, *, tb):
    L = CHUNK

    @pl.when(pl.program_id(1) == 0)
    def _():
        xbc_ref[0:SUBLANES, :] = jnp.zeros((SUBLANES, CONV_DIM), F32)
        sstate_ref[...] = jnp.zeros(sstate_ref.shape, F32)
        rstate_ref[...] = jnp.zeros(rstate_ref.shape, F32)

    u_ref[...] = _rms(x_ref[...], pre_ref[...]).astype(BF16)

    def proj(lo, hi):
        return _dot(u_ref[...], wmain_ref[:, lo:hi])

    z_ref[...] = proj(_Z0, _Z1)
    xbc_ref[SUBLANES:SUBLANES + tb, :] = proj(_X0, _X1)
    q = proj(_Q0, _Q1)
    k = proj(_K0, _K1)
    v_ref[...] = proj(_V0, _V1).astype(BF16)
    g_ref[...] = proj(_G0, _G1)
    dt_ref[...] = _softplus(_dot(u_ref[...], wdt_ref[...]) + dtb_row_ref[...])
    dtT_ref[...] = _softplus(_dot_nt(wdtT_ref[...], u_ref[...]) + dtb_col_ref[...])

    cos = cos_ref[...]
    sin = sin_ref[...]
    for h in range(RET_HEADS):
        sl = slice(h * RET_QK_DIM, (h + 1) * RET_QK_DIM)
        qh = q[:, sl]
        kh = k[:, sl]
        q_ref[:, sl] = qh * cos + pltpu.roll(qh, RET_QK_DIM // 2, 1) * sin
        k_ref[:, sl] = (kh * cos + pltpu.roll(kh, RET_QK_DIM // 2, 1) * sin) * (RET_QK_DIM ** -0.5)

    conv = convb_ref[...]
    for kk in range(CONV_K):
        off = SUBLANES - (CONV_K - 1) + kk
        conv = conv + convw_ref[kk:kk + 1, :] * xbc_ref[off:off + tb, :]
    act_ref[...] = conv * _sigmoid(conv)
    xbc_ref[0:SUBLANES, :] = xbc_ref[tb:tb + SUBLANES, :]

    a_row = -jnp.exp(alog_row_ref[...])
    a_col = -jnp.exp(alog_col_ref[...])
    rows = lax.broadcasted_iota(jnp.int32, (L, L), 0)
    cols = lax.broadcasted_iota(jnp.int32, (L, L), 1)
    causal = rows >= cols
    tril = causal.astype(F32)
    triu = (rows <= cols).astype(F32)
    lane = lax.broadcasted_iota(jnp.int32, (L, LANES), 1)
    low_half = lane < SSM_HEAD_DIM

    for c in range(tb // L):
        r = slice(c * L, (c + 1) * L)
        dt_c = dt_ref[r, :]
        dtT_c = dtT_ref[:, r]
        acum = _dot_exact(tril, dt_c * a_row)
        acumT = _dot_exact(dtT_c * a_col, triu)
        ex_e = _expand_heads(jnp.exp(acum), expand_ref)
        ex_w = _expand_heads(jnp.exp(acum[L - 1:L, :] - acum) * dt_c, expand_ref)
        xs = act_ref[r, 0:SSM_WIDTH]
        xw = (xs * ex_w).astype(BF16)

        for gi in range(SSM_GROUPS):
            gsl = slice(gi * GROUP_WIDTH, (gi + 1) * GROUP_WIDTH)
            b_g = act_ref[r, SSM_WIDTH + gi * SSM_STATE:SSM_WIDTH + (gi + 1) * SSM_STATE]
            c_g = act_ref[r, SSM_WIDTH + BC_WIDTH + gi * SSM_STATE:SSM_WIDTH + BC_WIDTH + (gi + 1) * SSM_STATE]
            c_b = c_g.astype(BF16)
            cb = _dot_nt(c_b, b_g.astype(BF16))
            st = sstate_ref[gi]
            y_inter = _dot(c_b, st.astype(BF16)) * ex_e[:, gsl]
            for p in range(HEADS_PER_GROUP // 2):
                h0 = gi * HEADS_PER_GROUP + 2 * p
                gmats = []
                for h in (h0, h0 + 1):
                    diff = acum[:, h:h + 1] - acumT[h:h + 1, :]
                    gmats.append(cb * jnp.exp(jnp.where(causal, diff, -jnp.inf)) * dtT_c[h:h + 1, :])
                lhs = jnp.concatenate(gmats, axis=1).astype(BF16)
                xp = xs[:, h0 * SSM_HEAD_DIM:(h0 + 2) * SSM_HEAD_DIM]
                rhs = jnp.concatenate([jnp.where(low_half, xp, 0.0),
                                       jnp.where(low_half, 0.0, xp)], axis=0).astype(BF16)
                y_ref[r, h0 * SSM_HEAD_DIM:(h0 + 2) * SSM_HEAD_DIM] = (
                    _dot(lhs, rhs) + y_inter[:, 2 * p * SSM_HEAD_DIM:(2 * p + 2) * SSM_HEAD_DIM])
            sstate_ref[gi] = st * ex_e[L - 1:L, gsl] + _dot(b_g.T.astype(BF16), xw[:, gsl])

        for h in range(RET_HEADS):
            qsl = slice(h * RET_QK_DIM, (h + 1) * RET_QK_DIM)
            vsl = slice(h * RET_V_DIM, (h + 1) * RET_V_DIM)
            qh = q_ref[r, qsl]
            kh = k_ref[r, qsl]
            vh = v_ref[r, vsl]
            sc = _dot_nt(qh.astype(BF16), kh.astype(BF16)) * dmat_ref[h]
            st = rstate_ref[h]
            lhs = jnp.concatenate([sc, qh * qdec_ref[:, qsl]], axis=1).astype(BF16)
            rhs = jnp.concatenate([vh, st.astype(BF16)], axis=0)
            y_ref[r, SSM_WIDTH + h * RET_V_DIM:SSM_WIDTH + (h + 1) * RET_V_DIM] = _dot(lhs, rhs)
            kd = kh * kdec_ref[:, qsl]
            rstate_ref[h] = st * cdec_ref[:, vsl] + _dot(kd.T.astype(BF16), vh)

    y = y_ref[:, 0:SSM_WIDTH] + act_ref[:, 0:SSM_WIDTH] * dskip_ref[...]
    zz = z_ref[...]
    cat_ref[:, 0:SSM_WIDTH] = _rms(y * (zz * _sigmoid(zz)), ssmnorm_ref[...]).astype(BF16)
    for h in range(RET_HEADS):
        vsl = slice(h * RET_V_DIM, (h + 1) * RET_V_DIM)
        rh = y_ref[:, SSM_WIDTH + h * RET_V_DIM:SSM_WIDTH + (h + 1) * RET_V_DIM]
        d = rh - jnp.mean(rh, axis=-1, keepdims=True)
        var = jnp.mean(d * d, axis=-1, keepdims=True)
        rn = d * lax.rsqrt(var + RET_NORM_EPS) * retnorm_ref[:, vsl]
        gh = g_ref[:, vsl]
        cat_ref[:, SSM_WIDTH + h * RET_V_DIM:SSM_WIDTH + (h + 1) * RET_V_DIM] = (
            gh * _sigmoid(gh) * rn).astype(BF16)
    mixed = _dot(cat_ref[...], wout_ref[...])
    o_ref[...] = x_ref[...] + _rms(mixed, post_ref[...])


def _retention_tables():
    log_gamma = jnp.log(1.0 - 2.0 ** (-5.0 - jnp.arange(RET_HEADS, dtype=F32)))
    pos = jnp.arange(CHUNK, dtype=F32)
    rel = pos[:, None] - pos[None, :]
    dmat = jnp.exp(jnp.where((rel >= 0)[None], rel[None] * log_gamma[:, None, None], -jnp.inf))
    q_decay = jnp.exp((pos[:, None] + 1.0) * log_gamma[None])
    k_decay = jnp.exp((CHUNK - 1.0 - pos[:, None]) * log_gamma[None])
    chunk_decay = jnp.exp(CHUNK * log_gamma)
    qdec = jnp.repeat(q_decay, RET_QK_DIM, axis=1)
    kdec = jnp.repeat(k_decay, RET_QK_DIM, axis=1)
    cdec = jnp.repeat(chunk_decay, RET_V_DIM)[None, :]
    return dmat, qdec, kdec, cdec


def _expand_matrix():
    e = np.zeros((LANES, SSM_WIDTH), np.float32)
    for h in range(SSM_HEADS):
        e[h, h * SSM_HEAD_DIM:(h + 1) * SSM_HEAD_DIM] = 1.0
    return jnp.asarray(e, BF16)


def _mixer(x2d, batch, seq, cos, sin, pre, w_in, conv_w, conv_b, dt_bias, a_log, d_skip, ssm_norm,
           ret_norm, w_out, post):
    tb = min(MIX_BLOCK, seq)
    nj = seq // tb
    sizes = [SSM_WIDTH, CONV_DIM, SSM_HEADS, RET_QK_WIDTH, RET_QK_WIDTH, RET_WIDTH, RET_WIDTH]
    offs = np.concatenate([[0], np.cumsum(sizes)])
    w_z, w_xbc, w_dt, w_q, w_k, w_v, w_g = [w_in[:, offs[i]:offs[i + 1]] for i in range(7)]
    w_main = jnp.concatenate([w_z, w_xbc, w_q, w_k, w_v, w_g], axis=1).astype(BF16)
    w_dt_pad = jnp.pad(w_dt, ((0, 0), (0, LANES - SSM_HEADS))).astype(BF16)
    w_dt_t = w_dt.T.astype(BF16)
    pad16 = (0, LANES - SSM_HEADS)
    dtb_row = jnp.pad(dt_bias, pad16)[None, :]
    alog_row = jnp.pad(a_log, pad16)[None, :]
    dskip = jnp.repeat(d_skip, SSM_HEAD_DIM)[None, :]
    dmat, qdec, kdec, cdec = _retention_tables()

    row = lambda b, j: (b * nj + j, 0)
    operands = [
        (x2d, pl.BlockSpec((tb, D_MODEL), row)),
        (cos, pl.BlockSpec((tb, LANES), row)),
        (sin, pl.BlockSpec((tb, LANES), row)),
        (pre[None, :], None),
        (w_main, None),
        (w_dt_pad, None),
        (w_dt_t, None),
        (conv_w, None),
        (conv_b[None, :], None),
        (dtb_row, None),
        (dt_bias[:, None], None),
        (alog_row, None),
        (a_log[:, None], None),
        (dskip, None),
        (ssm_norm[None, :], None),
        (ret_norm[None, :], None),
        (w_out.astype(BF16), None),
        (post[None, :], None),
        (_expand_matrix(), None),
        (dmat, None),
        (qdec, None),
        (kdec, None),
        (cdec, None),
    ]
    args = [a for a, _ in operands]
    in_specs = [s if s is not None else _resident(a.shape) for a, s in operands]
    return pl.pallas_call(
        functools.partial(_mixer_kernel, tb=tb),
        grid=(batch, nj),
        in_specs=in_specs,
        out_specs=pl.BlockSpec((tb, D_MODEL), row),
        out_shape=jax.ShapeDtypeStruct(x2d.shape, F32),
        scratch_shapes=[
            pltpu.VMEM((tb, D_MODEL), BF16),
            pltpu.VMEM((tb, SSM_WIDTH), F32),
            pltpu.VMEM((SUBLANES + tb, CONV_DIM), F32),
            pltpu.VMEM((tb, CONV_DIM), F32),
            pltpu.VMEM((tb, RET_QK_WIDTH), F32),
            pltpu.VMEM((tb, RET_QK_WIDTH), F32),
            pltpu.VMEM((tb, RET_WIDTH), BF16),
            pltpu.VMEM((tb, RET_WIDTH), F32),
            pltpu.VMEM((tb, LANES), F32),
            pltpu.VMEM((SSM_HEADS, tb), F32),
            pltpu.VMEM((tb, D_INNER), F32),
            pltpu.VMEM((tb, D_INNER), BF16),
            pltpu.VMEM((SSM_GROUPS, SSM_STATE, GROUP_WIDTH), F32),
            pltpu.VMEM((RET_HEADS, RET_QK_DIM, RET_V_DIM), F32),
        ],
        compiler_params=pltpu.CompilerParams(
            dimension_semantics=("arbitrary", "arbitrary"), vmem_limit_bytes=VMEM_LIMIT),
        name="mixer",
    )(*args)


def kernel(x, positions, ffn1_pre_norm, ffn1_w_in, ffn1_w_out, ffn1_post_norm, mix_pre_norm, mix_w_in, conv_w, conv_b, dt_bias, a_log, d_skip, ssm_norm, ret_norm, mix_w_out, mix_post_norm, ffn2_pre_norm, ffn2_w_in, ffn2_w_out, ffn2_post_norm):
    batch, seq, _ = x.shape
    depth = ffn1_w_in.shape[0]
    cos, sin = _rope_tables(positions)
    h = x.reshape(batch * seq, D_MODEL)
    for l in range(depth):
        h = _ffn(h, ffn1_pre_norm[l], ffn1_w_in[l], ffn1_w_out[l], ffn1_post_norm[l])
        h = _mixer(h, batch, seq, cos, sin, mix_pre_norm[l], mix_w_in[l], conv_w[l], conv_b[l],
                   dt_bias[l], a_log[l], d_skip[l], ssm_norm[l], ret_norm[l], mix_w_out[l],
                   mix_post_norm[l])
        h = _ffn(h, ffn2_pre_norm[l], ffn2_w_in[l], ffn2_w_out[l], ffn2_post_norm[l])
    return h.reshape(batch, seq, D_MODEL)
```

```python
import collections
import functools

import jax
import jax.numpy as jnp
import numpy as np
from jax import lax
from jax.experimental import pallas as pl
from jax.experimental.pallas import tpu as pltpu

F32 = jnp.float32
BF16 = jnp.bfloat16

D_MODEL = 1024
D_FF = 2816
NORM_EPS = 1e-6
RET_NORM_EPS = 1e-5

D_INNER = 2 * D_MODEL
SSM_WIDTH = D_INNER // 2
RET_WIDTH = D_INNER - SSM_WIDTH
SSM_HEAD_DIM = 64
SSM_HEADS = SSM_WIDTH // SSM_HEAD_DIM
SSM_GROUPS = 2
HEADS_PER_GROUP = SSM_HEADS // SSM_GROUPS
GROUP_WIDTH = HEADS_PER_GROUP * SSM_HEAD_DIM
SSM_STATE = 128
BC_WIDTH = SSM_GROUPS * SSM_STATE
CONV_K = 4
CONV_DIM = SSM_WIDTH + 2 * BC_WIDTH
RET_HEADS = 4
RET_V_DIM = RET_WIDTH // RET_HEADS
RET_QK_DIM = RET_V_DIM // 2
RET_QK_WIDTH = RET_HEADS * RET_QK_DIM
ROPE_BASE = 10000.0

LANES = 128
SUBLANES = 8

CHUNK = 128
MIX_BLOCK = 256
FFN_BLOCK = 1024
FFN_COL_CHUNK = 256
ROPE_BLOCK = 2048
VMEM_LIMIT = 56 * 1024 * 1024
DT_COPIES = 4
SLAB = 256

_Z0, _Z1 = 0, SSM_WIDTH
_X0, _X1 = _Z1, _Z1 + CONV_DIM
_Q0, _Q1 = _X1, _X1 + RET_QK_WIDTH
_K0, _K1 = _Q1, _Q1 + RET_QK_WIDTH
_V0, _V1 = _K1, _K1 + RET_WIDTH
_G0, _G1 = _V1, _V1 + RET_WIDTH


def _resident(shape):
    nd = len(shape)
    return pl.BlockSpec(shape, lambda *_: (0,) * nd, pipeline_mode=pl.Buffered(1))


def _sigmoid(x):
    return 1.0 / (1.0 + jnp.exp(-x))


def _softplus(x):
    return jnp.maximum(x, 0.0) + jnp.log1p(jnp.exp(-jnp.abs(x)))


def _rms(x, w):
    ms = jnp.mean(x * x, axis=-1, keepdims=True)
    return x * lax.rsqrt(ms + NORM_EPS) * w


def _dot(a, b):
    return jnp.dot(a, b, preferred_element_type=F32)


def _dot_nt(a, b):
    return lax.dot_general(a, b, (((1,), (1,)), ((), ())), preferred_element_type=F32)


def _split3(v):
    p1 = v.astype(BF16)
    r1 = v - p1.astype(F32)
    p2 = r1.astype(BF16)
    p3 = (r1 - p2.astype(F32)).astype(BF16)
    return p1, p2, p3


def _rope_kernel(pos_ref, invf_ref, sign_ref, cos_ref, sin_ref):
    ang = pos_ref[...].astype(F32) * invf_ref[...]
    cos_ref[...] = jnp.cos(ang)
    sin_ref[...] = jnp.sin(ang) * sign_ref[...]


def _rope_tables(positions):
    t = positions.size
    tb = min(ROPE_BLOCK, t)
    half = RET_QK_DIM // 2
    inv_freq = ROPE_BASE ** (-jnp.arange(half, dtype=F32) / half)
    invf = jnp.concatenate([inv_freq, inv_freq])[None, :]
    sign = jnp.concatenate([-jnp.ones((half,), F32), jnp.ones((half,), F32)])[None, :]
    return pl.pallas_call(
        _rope_kernel,
        grid=(t // tb,),
        in_specs=[pl.BlockSpec((tb, 1), lambda i: (i, 0)),
                  pl.BlockSpec((1, LANES), lambda i: (0, 0)),
                  pl.BlockSpec((1, LANES), lambda i: (0, 0))],
        out_specs=[pl.BlockSpec((tb, LANES), lambda i: (i, 0)),
                   pl.BlockSpec((tb, LANES), lambda i: (i, 0))],
        out_shape=[jax.ShapeDtypeStruct((t, LANES), F32)] * 2,
        name="rope_tables",
    )(positions.reshape(t, 1), invf, sign)


def _ffn_kernel(x_ref, pre_ref, win_ref, wout_ref, post_ref, o_ref, xn_ref, act_ref):
    xn_ref[...] = _rms(x_ref[...], pre_ref[...]).astype(BF16)
    for lo in range(0, D_FF, FFN_COL_CHUNK):
        gate = _dot(xn_ref[...], win_ref[:, lo:lo + FFN_COL_CHUNK])
        up = _dot(xn_ref[...], win_ref[:, D_FF + lo:D_FF + lo + FFN_COL_CHUNK])
        act_ref[:, lo:lo + FFN_COL_CHUNK] = (gate * _sigmoid(gate) * up).astype(BF16)
    h = _dot(act_ref[...], wout_ref[...])
    o_ref[...] = x_ref[...] + 0.5 * _rms(h, post_ref[...])


def _ffn(x2d, pre, w_in, w_out, post):
    t = x2d.shape[0]
    tm = min(FFN_BLOCK, t)
    return pl.pallas_call(
        _ffn_kernel,
        grid=(t // tm,),
        in_specs=[pl.BlockSpec((tm, D_MODEL), lambda i: (i, 0)),
                  _resident((1, D_MODEL)),
                  _resident((D_MODEL, 2 * D_FF)),
                  _resident((D_FF, D_MODEL)),
                  _resident((1, D_MODEL))],
        out_specs=pl.BlockSpec((tm, D_MODEL), lambda i: (i, 0)),
        out_shape=jax.ShapeDtypeStruct((t, D_MODEL), F32),
        scratch_shapes=[pltpu.VMEM((tm, D_MODEL), BF16),
                        pltpu.VMEM((tm, D_FF), BF16)],
        compiler_params=pltpu.CompilerParams(
            dimension_semantics=("arbitrary",), vmem_limit_bytes=VMEM_LIMIT),
        name="ffn",
    )(x2d, pre[None, :], w_in.astype(BF16), w_out.astype(BF16), post[None, :])


_Slot = collections.namedtuple(
    "_Slot", ["zs", "xs", "bc", "bT", "q", "qd", "k", "kdT", "v", "gs", "dt", "dtT"])


def _mixer_kernel(xp_ref, xr_ref, cos_ref, sin_ref, pre_ref, wmain_ref, wdt_ref, wdtT_ref, convw_ref,
                  convb_ref, dtb_row_ref, dtb_col_ref, alog_row_ref, alog_col_ref, dskip_ref,
                  ssmnorm_ref, retnorm_ref, wout_ref, post_ref, expand_ref, tril_ref, triu_ref,
                  dmat_ref, qdec_ref, kdec_ref, cdec_ref,
                  o_ref,
                  u_ref, xbc_ref, y_ref, cat_ref, mixed_ref, sstate_ref, rstate_ref, *slot_refs,
                  tb, nj):
    s = pl.program_id(0)
    n = len(_Slot._fields)
    slot_a = _Slot(*slot_refs[:n])
    slot_b = _Slot(*slot_refs[n:])

    @pl.when(s == 0)
    def _():
        for ref in slot_b:
            ref[...] = jnp.zeros(ref.shape, ref.dtype)

    @pl.when(lax.rem(s, nj) == 0)
    def _():
        xbc_ref[0:SUBLANES, :] = jnp.zeros((SUBLANES, CONV_DIM), F32)

    @pl.when(jnp.logical_or(s == 0, lax.rem(s + nj - 1, nj) == 0))
    def _():
        sstate_ref[...] = jnp.zeros(sstate_ref.shape, F32)
        rstate_ref[...] = jnp.zeros(rstate_ref.shape, F32)

    refs = (xp_ref, xr_ref, cos_ref, sin_ref, pre_ref, wmain_ref, wdt_ref, wdtT_ref, convw_ref,
            convb_ref, dtb_row_ref, dtb_col_ref, alog_row_ref, alog_col_ref, dskip_ref, ssmnorm_ref,
            retnorm_ref, wout_ref, post_ref, expand_ref, tril_ref, triu_ref, dmat_ref, qdec_ref,
            kdec_ref, cdec_ref, o_ref, u_ref, xbc_ref, y_ref, cat_ref, mixed_ref, sstate_ref,
            rstate_ref)

    @pl.when(lax.rem(s, 2) == 0)
    def _():
        _mixer_step(*refs, slot_a, slot_b, tb=tb)

    @pl.when(lax.rem(s, 2) == 1)
    def _():
        _mixer_step(*refs, slot_b, slot_a, tb=tb)


def _mixer_step(xp_ref, xr_ref, cos_ref, sin_ref, pre_ref, wmain_ref, wdt_ref, wdtT_ref, convw_ref,
                convb_ref, dtb_row_ref, dtb_col_ref, alog_row_ref, alog_col_ref, dskip_ref,
                ssmnorm_ref, retnorm_ref, wout_ref, post_ref, expand_ref, tril_ref, triu_ref,
                dmat_ref, qdec_ref, kdec_ref, cdec_ref, o_ref,
                u_ref, xbc_ref, y_ref, cat_ref, mixed_ref, sstate_ref, rstate_ref, P, S, *, tb):
    L = CHUNK

    def proj(lo, hi):
        return _dot(u_ref[...], wmain_ref[:, lo:hi])

    def p_norm():
        u_ref[...] = _rms(xp_ref[...], pre_ref[...]).astype(BF16)

    def p_gate(dst, base, j):
        def piece():
            sl = slice(j * SLAB, (j + 1) * SLAB)
            t = proj(base + j * SLAB, base + (j + 1) * SLAB)
            dst[:, sl] = t * _sigmoid(t)
        return piece

    def p_value(j):
        def piece():
            sl = slice(j * SLAB, (j + 1) * SLAB)
            P.v[:, sl] = proj(_V0 + j * SLAB, _V0 + (j + 1) * SLAB).astype(BF16)
        return piece

    def p_rotary(j):
        def piece():
            q = proj(_Q0 + j * SLAB, _Q0 + (j + 1) * SLAB)
            k = proj(_K0 + j * SLAB, _K0 + (j + 1) * SLAB)
            cos = cos_ref[...]
            sin = sin_ref[...]
            for hh in range(SLAB // RET_QK_DIM):
                loc = slice(hh * RET_QK_DIM, (hh + 1) * RET_QK_DIM)
                sl = slice(j * SLAB + hh * RET_QK_DIM, j * SLAB + (hh + 1) * RET_QK_DIM)
                qh = q[:, loc]
                kh = k[:, loc]
                qr = qh * cos + pltpu.roll(qh, RET_QK_DIM // 2, 1) * sin
                kr = (kh * cos + pltpu.roll(kh, RET_QK_DIM // 2, 1) * sin) * (RET_QK_DIM ** -0.5)
                P.q[:, sl] = qr.astype(BF16)
                P.qd[:, sl] = (qr * qdec_ref[:, sl]).astype(BF16)
                P.k[:, sl] = kr.astype(BF16)
                P.kdT[sl, :] = (kr * kdec_ref[:, sl]).T.astype(BF16)
        return piece

    def p_dt():
        P.dt[...] = _softplus(_dot(u_ref[...], wdt_ref[...]) + dtb_row_ref[...])
        P.dtT[...] = _softplus(_dot_nt(wdtT_ref[...], u_ref[...]) + dtb_col_ref[...])

    def p_conv(j):
        def piece():
            sl = slice(j * SLAB, (j + 1) * SLAB)
            xbc_ref[SUBLANES:SUBLANES + tb, sl] = proj(_X0 + j * SLAB, _X0 + (j + 1) * SLAB)
            conv = convb_ref[:, sl]
            for kk in range(CONV_K):
                off = SUBLANES - (CONV_K - 1) + kk
                conv = conv + convw_ref[kk:kk + 1, sl] * xbc_ref[off:off + tb, sl]
            act = conv * _sigmoid(conv)
            xbc_ref[0:SUBLANES, sl] = xbc_ref[tb:tb + SUBLANES, sl]
            if (j + 1) * SLAB <= SSM_WIDTH:
                P.xs[:, sl] = act
            else:
                lo = j * SLAB - SSM_WIDTH
                P.bc[:, lo:lo + SLAB] = act.astype(BF16)
                if lo < BC_WIDTH:
                    P.bT[lo:lo + SLAB, :] = act.T.astype(BF16)
        return piece

    a_row = -jnp.exp(alog_row_ref[...])
    a_col = -jnp.exp(alog_col_ref[...])
    rows = lax.broadcasted_iota(jnp.int32, (L, L), 0)
    cols = lax.broadcasted_iota(jnp.int32, (L, L), 1)
    causal = rows >= cols
    lane = lax.broadcasted_iota(jnp.int32, (L, LANES), 1)
    low_half = lane < SSM_HEAD_DIM

    def s_chunk_pieces(c):
        r = slice(c * L, (c + 1) * L)
        ctx = {}

        def decays():
            dt_c = S.dt[r, :]
            dtT_c = S.dtT[:, r]
            parts = _dot(tril_ref[...], jnp.concatenate(_split3(dt_c * a_row), axis=1))
            acum = parts[:, 0:LANES] + parts[:, LANES:2 * LANES] + parts[:, 2 * LANES:3 * LANES]
            partsT = _dot(jnp.concatenate(_split3(dtT_c * a_col), axis=0), triu_ref[...])
            acumT = (partsT[0:SSM_HEADS] + partsT[SSM_HEADS:2 * SSM_HEADS]
                     + partsT[2 * SSM_HEADS:3 * SSM_HEADS])
            ctx["acum"] = acum
            ctx["lrow"] = acumT - jnp.log(dtT_c)
            e = jnp.exp(acum)
            w = jnp.exp(acum[L - 1:L, :] - acum) * dt_c
            e_hi = e.astype(BF16).astype(F32)
            w_hi = w.astype(BF16).astype(F32)
            packed = jnp.where(lane < SSM_HEADS, e_hi,
                               jnp.where(lane < 2 * SSM_HEADS, e - e_hi,
                                         jnp.where(lane < 3 * SSM_HEADS, w_hi,
                                                   jnp.where(lane < 4 * SSM_HEADS, w - w_hi, 0.0))))
            ex = _dot(packed.astype(BF16), expand_ref[...])
            ctx["ex_e"] = ex[:, 0:SSM_WIDTH]
            ctx["xw"] = (S.xs[r, :] * ex[:, SSM_WIDTH:2 * SSM_WIDTH]).astype(BF16)

        def group_open(gi):
            def piece():
                gsl = slice(gi * GROUP_WIDTH, (gi + 1) * GROUP_WIDTH)
                b_g = S.bc[r, gi * SSM_STATE:(gi + 1) * SSM_STATE]
                c_g = S.bc[r, BC_WIDTH + gi * SSM_STATE:BC_WIDTH + (gi + 1) * SSM_STATE]
                ctx["cb"] = _dot_nt(c_g, b_g)
                st = sstate_ref[gi]
                ctx["y_inter"] = _dot(c_g, st.astype(BF16)) * ctx["ex_e"][:, gsl]
                sstate_ref[gi] = (st * ctx["ex_e"][L - 1:L, gsl]
                                  + _dot(S.bT[gi * SSM_STATE:(gi + 1) * SSM_STATE, r], ctx["xw"][:, gsl]))
            return piece

        def head_pair(gi, p):
            def piece():
                h0 = gi * HEADS_PER_GROUP + 2 * p
                psl = slice(h0 * SSM_HEAD_DIM, (h0 + 2) * SSM_HEAD_DIM)
                gmats = []
                for h in (h0, h0 + 1):
                    diff = ctx["acum"][:, h:h + 1] - ctx["lrow"][h:h + 1, :]
                    gmats.append(ctx["cb"] * jnp.exp(jnp.where(causal, diff, -jnp.inf)))
                lhs = jnp.concatenate(gmats, axis=1).astype(BF16)
                xp = S.xs[r, psl]
                rhs = jnp.concatenate([jnp.where(low_half, xp, 0.0),
                                       jnp.where(low_half, 0.0, xp)], axis=0).astype(BF16)
                y = (_dot(lhs, rhs) + ctx["y_inter"][:, 2 * p * SSM_HEAD_DIM:(2 * p + 2) * SSM_HEAD_DIM]
                     + xp * dskip_ref[:, psl])
                y_ref[:, psl] = y * S.zs[r, psl]
            return piece

        def ssd_norm():
            cat_ref[r, 0:SSM_WIDTH] = _rms(y_ref[...], ssmnorm_ref[...]).astype(BF16)

        def retention(h):
            def piece():
                qsl = slice(h * RET_QK_DIM, (h + 1) * RET_QK_DIM)
                vsl = slice(h * RET_V_DIM, (h + 1) * RET_V_DIM)
                vh = S.v[r, vsl]
                sc = _dot_nt(S.q[r, qsl], S.k[r, qsl]) * dmat_ref[h]
                st = rstate_ref[h]
                lhs = jnp.concatenate([sc.astype(BF16), S.qd[r, qsl]], axis=1)
                rhs = jnp.concatenate([vh, st.astype(BF16)], axis=0)
                rh = _dot(lhs, rhs)
                rstate_ref[h] = st * cdec_ref[:, vsl] + _dot(S.kdT[qsl, r], vh)
                d = rh - jnp.mean(rh, axis=-1, keepdims=True)
                var = jnp.mean(d * d, axis=-1, keepdims=True)
                rn = d * lax.rsqrt(var + RET_NORM_EPS) * retnorm_ref[:, vsl]
                cat_ref[r, SSM_WIDTH + h * RET_V_DIM:SSM_WIDTH + (h + 1) * RET_V_DIM] = (
                    S.gs[r, vsl] * rn).astype(BF16)
            return piece

        pieces = [decays]
        for gi in range(SSM_GROUPS):
            pieces.append(group_open(gi))
            pieces += [head_pair(gi, p) for p in range(HEADS_PER_GROUP // 2)]
        pieces.append(ssd_norm)
        pieces += [retention(h) for h in range(RET_HEADS)]
        return pieces

    def s_out(j):
        def piece():
            sl = slice(j * SLAB, (j + 1) * SLAB)
            mixed_ref[:, sl] = _dot(cat_ref[...], wout_ref[:, sl])
        return piece

    def s_residual():
        o_ref[...] = xr_ref[...] + _rms(mixed_ref[...], post_ref[...])

    p_light = ([p_gate(P.zs, _Z0, j) for j in range(SSM_WIDTH // SLAB)]
               + [p_gate(P.gs, _G0, j) for j in range(RET_WIDTH // SLAB)]
               + [p_rotary(j) for j in range(RET_QK_WIDTH // SLAB)] + [p_dt])
    s_scan = [piece for c in range(tb // L) for piece in s_chunk_pieces(c)]
    p_norm()
    n_p, n_s = len(p_light), len(s_scan)
    done = 0
    for i, piece in enumerate(p_light):
        piece()
        upto = (i + 1) * n_s // n_p
        for sp in s_scan[done:upto]:
            sp()
        done = upto
    convs = [p_conv(j) for j in range(CONV_DIM // SLAB)]
    outs = [s_out(j) for j in range(D_MODEL // SLAB)]
    values = [p_value(j) for j in range(RET_WIDTH // SLAB)]
    for out_piece, conv_piece in zip(outs, convs):
        out_piece()
        conv_piece()
    s_residual()
    rest = convs[len(outs):]
    for i, piece in enumerate(values):
        piece()
        if i < len(rest):
            rest[i]()


def _retention_tables(tb):
    log_gamma = jnp.log(1.0 - 2.0 ** (-5.0 - jnp.arange(RET_HEADS, dtype=F32)))
    pos = jnp.arange(CHUNK, dtype=F32)
    rel = pos[:, None] - pos[None, :]
    dmat = jnp.exp(jnp.where((rel >= 0)[None], rel[None] * log_gamma[:, None, None], -jnp.inf))
    q_decay = jnp.exp((pos[:, None] + 1.0) * log_gamma[None])
    k_decay = jnp.exp((CHUNK - 1.0 - pos[:, None]) * log_gamma[None])
    chunk_decay = jnp.exp(CHUNK * log_gamma)
    qdec = jnp.tile(jnp.repeat(q_decay, RET_QK_DIM, axis=1), (tb // CHUNK, 1))
    kdec = jnp.tile(jnp.repeat(k_decay, RET_QK_DIM, axis=1), (tb // CHUNK, 1))
    cdec = jnp.repeat(chunk_decay, RET_V_DIM)[None, :]
    return dmat, qdec, kdec, cdec


def _constant_matrices():
    e = np.zeros((LANES, 2 * SSM_WIDTH), np.float32)
    for g in range(DT_COPIES):
        for h in range(SSM_HEADS):
            c0 = (g // 2) * SSM_WIDTH + h * SSM_HEAD_DIM
            e[g * SSM_HEADS + h, c0:c0 + SSM_HEAD_DIM] = 1.0
    tril = np.tril(np.ones((CHUNK, CHUNK), np.float32))
    return jnp.asarray(e, BF16), jnp.asarray(tril, BF16), jnp.asarray(tril.T, BF16)


def _slot_scratch(tb):
    return list(_Slot(
        zs=pltpu.VMEM((tb, SSM_WIDTH), F32),
        xs=pltpu.VMEM((tb, SSM_WIDTH), F32),
        bc=pltpu.VMEM((tb, 2 * BC_WIDTH), BF16),
        bT=pltpu.VMEM((BC_WIDTH, tb), BF16),
        q=pltpu.VMEM((tb, RET_QK_WIDTH), BF16),
        qd=pltpu.VMEM((tb, RET_QK_WIDTH), BF16),
        k=pltpu.VMEM((tb, RET_QK_WIDTH), BF16),
        kdT=pltpu.VMEM((RET_QK_WIDTH, tb), BF16),
        v=pltpu.VMEM((tb, RET_WIDTH), BF16),
        gs=pltpu.VMEM((tb, RET_WIDTH), F32),
        dt=pltpu.VMEM((tb, LANES), F32),
        dtT=pltpu.VMEM((SSM_HEADS, tb), F32),
    ))


def _mixer(x2d, batch, seq, cos, sin, pre, w_in, conv_w, conv_b, dt_bias, a_log, d_skip, ssm_norm,
           ret_norm, w_out, post):
    tb = min(MIX_BLOCK, seq)
    nj = seq // tb
    nblocks = batch * nj
    sizes = [SSM_WIDTH, CONV_DIM, SSM_HEADS, RET_QK_WIDTH, RET_QK_WIDTH, RET_WIDTH, RET_WIDTH]
    offs = np.concatenate([[0], np.cumsum(sizes)])
    w_z, w_xbc, w_dt, w_q, w_k, w_v, w_g = [w_in[:, offs[i]:offs[i + 1]] for i in range(7)]
    w_main = jnp.concatenate([w_z, w_xbc, w_q, w_k, w_v, w_g], axis=1).astype(BF16)
    lane_pad = (0, LANES - DT_COPIES * SSM_HEADS)
    w_dt_pad = jnp.pad(jnp.tile(w_dt, (1, DT_COPIES)), ((0, 0), lane_pad)).astype(BF16)
    w_dt_t = w_dt.T.astype(BF16)
    dtb_row = jnp.pad(jnp.tile(dt_bias, DT_COPIES), lane_pad)[None, :]
    alog_row = jnp.pad(jnp.tile(a_log, DT_COPIES), lane_pad)[None, :]
    dskip = jnp.repeat(d_skip, SSM_HEAD_DIM)[None, :]
    dmat, qdec, kdec, cdec = _retention_tables(tb)
    expand, tril, triu = _constant_matrices()

    cur = lambda s: (jnp.minimum(s, nblocks - 1), 0)
    prev = lambda s: (jnp.maximum(s - 1, 0), 0)
    operands = [
        (x2d, pl.BlockSpec((tb, D_MODEL), cur)),
        (x2d, pl.BlockSpec((tb, D_MODEL), prev)),
        (cos, pl.BlockSpec((tb, LANES), cur)),
        (sin, pl.BlockSpec((tb, LANES), cur)),
        (pre[None, :], None),
        (w_main, None),
        (w_dt_pad, None),
        (w_dt_t, None),
        (conv_w, None),
        (conv_b[None, :], None),
        (dtb_row, None),
        (dt_bias[:, None], None),
        (alog_row, None),
        (a_log[:, None], None),
        (dskip, None),
        (ssm_norm[None, :], None),
        (ret_norm[None, :], None),
        (w_out.astype(BF16), None),
        (post[None, :], None),
        (expand, None),
        (tril, None),
        (triu, None),
        (dmat, None),
        (qdec, None),
        (kdec, None),
        (cdec, None),
    ]
    args = [a for a, _ in operands]
    in_specs = [sp if sp is not None else _resident(a.shape) for a, sp in operands]
    return pl.pallas_call(
        functools.partial(_mixer_kernel, tb=tb, nj=nj),
        grid=(nblocks + 1,),
        in_specs=in_specs,
        out_specs=pl.BlockSpec((tb, D_MODEL), prev),
        out_shape=jax.ShapeDtypeStruct(x2d.shape, F32),
        scratch_shapes=[
            pltpu.VMEM((tb, D_MODEL), BF16),
            pltpu.VMEM((SUBLANES + tb, CONV_DIM), F32),
            pltpu.VMEM((CHUNK, SSM_WIDTH), F32),
            pltpu.VMEM((tb, D_INNER), BF16),
            pltpu.VMEM((tb, D_MODEL), F32),
            pltpu.VMEM((SSM_GROUPS, SSM_STATE, GROUP_WIDTH), F32),
            pltpu.VMEM((RET_HEADS, RET_QK_DIM, RET_V_DIM), F32),
        ] + 2 * _slot_scratch(tb),
        compiler_params=pltpu.CompilerParams(
            dimension_semantics=("arbitrary",), vmem_limit_bytes=VMEM_LIMIT),
        name="mixer",
    )(*args)


def kernel(x, positions, ffn1_pre_norm, ffn1_w_in, ffn1_w_out, ffn1_post_norm, mix_pre_norm, mix_w_in, conv_w, conv_b, dt_bias, a_log, d_skip, ssm_norm, ret_norm, mix_w_out, mix_post_norm, ffn2_pre_norm, ffn2_w_in, ffn2_w_out, ffn2_post_norm):
    batch, seq, _ = x.shape
    depth = ffn1_w_in.shape[0]
    cos, sin = _rope_tables(positions)
    h = x.reshape(batch * seq, D_MODEL)
    for l in range(depth):
        h = _ffn(h, ffn1_pre_norm[l], ffn1_w_in[l], ffn1_w_out[l], ffn1_post_norm[l])
        h = _mixer(h, batch, seq, cos, sin, mix_pre_norm[l], mix_w_in[l], conv_w[l], conv_b[l],
                   dt_bias[l], a_log[l], d_skip[l], ssm_norm[l], ret_norm[l], mix_w_out[l],
                   mix_post_norm[l])
        h = _ffn(h, ffn2_pre_norm[l], ffn2_w_in[l], ffn2_w_out[l], ffn2_post_norm[l])
    return h.reshape(batch, seq, D_MODEL)
```

```python
import collections
import functools

import jax
import jax.numpy as jnp
import numpy as np
from jax import lax
from jax.experimental import pallas as pl
from jax.experimental.pallas import tpu as pltpu

F32 = jnp.float32
BF16 = jnp.bfloat16

D_MODEL = 1024
D_FF = 2816
NORM_EPS = 1e-6
RET_NORM_EPS = 1e-5

D_INNER = 2 * D_MODEL
SSM_WIDTH = D_INNER // 2
RET_WIDTH = D_INNER - SSM_WIDTH
SSM_HEAD_DIM = 64
SSM_HEADS = SSM_WIDTH // SSM_HEAD_DIM
SSM_GROUPS = 2
HEADS_PER_GROUP = SSM_HEADS // SSM_GROUPS
GROUP_WIDTH = HEADS_PER_GROUP * SSM_HEAD_DIM
SSM_STATE = 128
BC_WIDTH = SSM_GROUPS * SSM_STATE
CONV_K = 4
CONV_DIM = SSM_WIDTH + 2 * BC_WIDTH
RET_HEADS = 4
RET_V_DIM = RET_WIDTH // RET_HEADS
RET_QK_DIM = RET_V_DIM // 2
RET_QK_WIDTH = RET_HEADS * RET_QK_DIM
ROPE_BASE = 10000.0

LANES = 128
SUBLANES = 8

CHUNK = 128
MIX_BLOCK = 256
FFN_BLOCK = 1024
FFN_COL_CHUNK = 256
ROPE_BLOCK = 2048
VMEM_LIMIT = 56 * 1024 * 1024
DT_COPIES = 4
SLAB = 256
PREP_ROWS = 256

_Z0, _Z1 = 0, SSM_WIDTH
_X0, _X1 = _Z1, _Z1 + CONV_DIM
_Q0, _Q1 = _X1, _X1 + RET_QK_WIDTH
_K0, _K1 = _Q1, _Q1 + RET_QK_WIDTH
_V0, _V1 = _K1, _K1 + RET_WIDTH
_G0, _G1 = _V1, _V1 + RET_WIDTH


def _resident(shape):
    nd = len(shape)
    return pl.BlockSpec(shape, lambda *_: (0,) * nd, pipeline_mode=pl.Buffered(1))


def _resident_layer(shape, layer):
    nd = len(shape)
    return pl.BlockSpec((None,) + tuple(shape), lambda *_: (layer,) + (0,) * nd,
                        pipeline_mode=pl.Buffered(1))


def _sigmoid(x):
    return 1.0 / (1.0 + jnp.exp(-x))


def _softplus(x):
    return jnp.maximum(x, 0.0) + jnp.log1p(jnp.exp(-jnp.abs(x)))


def _rms(x, w):
    ms = jnp.mean(x * x, axis=-1, keepdims=True)
    return x * lax.rsqrt(ms + NORM_EPS) * w


def _dot(a, b):
    return jnp.dot(a, b, preferred_element_type=F32)


def _dot_nt(a, b):
    return lax.dot_general(a, b, (((1,), (1,)), ((), ())), preferred_element_type=F32)


def _split3(v):
    p1 = v.astype(BF16)
    r1 = v - p1.astype(F32)
    p2 = r1.astype(BF16)
    p3 = (r1 - p2.astype(F32)).astype(BF16)
    return p1, p2, p3


def _cast_kernel(w_ref, o_ref):
    o_ref[...] = w_ref[...].astype(BF16)


def _mix_in_kernel(w_ref, o_ref):
    o_ref[:, 0:_X1] = w_ref[:, 0:_X1].astype(BF16)
    o_ref[:, _X1:_G1] = w_ref[:, _X1 + SSM_HEADS:_G1 + SSM_HEADS].astype(BF16)


def _prep_weights(w, body, out_cols):
    layers, k, n = w.shape
    return pl.pallas_call(
        body,
        grid=(layers, k // PREP_ROWS),
        in_specs=[pl.BlockSpec((None, PREP_ROWS, n), lambda l, i: (l, i, 0))],
        out_specs=pl.BlockSpec((None, PREP_ROWS, out_cols), lambda l, i: (l, i, 0)),
        out_shape=jax.ShapeDtypeStruct((layers, k, out_cols), BF16),
        compiler_params=pltpu.CompilerParams(
            dimension_semantics=("arbitrary", "arbitrary"), vmem_limit_bytes=VMEM_LIMIT),
        name="prep_weights",
    )(w)


def _rope_kernel(pos_ref, invf_ref, sign_ref, cos_ref, sin_ref):
    ang = pos_ref[...].astype(F32) * invf_ref[...]
    cos_ref[...] = jnp.cos(ang)
    sin_ref[...] = jnp.sin(ang) * sign_ref[...]


def _rope_tables(positions):
    t = positions.size
    tb = min(ROPE_BLOCK, t)
    half = RET_QK_DIM // 2
    inv_freq = ROPE_BASE ** (-jnp.arange(half, dtype=F32) / half)
    invf = jnp.concatenate([inv_freq, inv_freq])[None, :]
    sign = jnp.concatenate([-jnp.ones((half,), F32), jnp.ones((half,), F32)])[None, :]
    return pl.pallas_call(
        _rope_kernel,
        grid=(t // tb,),
        in_specs=[pl.BlockSpec((tb, 1), lambda i: (i, 0)),
                  pl.BlockSpec((1, LANES), lambda i: (0, 0)),
                  pl.BlockSpec((1, LANES), lambda i: (0, 0))],
        out_specs=[pl.BlockSpec((tb, LANES), lambda i: (i, 0)),
                   pl.BlockSpec((tb, LANES), lambda i: (i, 0))],
        out_shape=[jax.ShapeDtypeStruct((t, LANES), F32)] * 2,
        name="rope_tables",
    )(positions.reshape(t, 1), invf, sign)


def _ffn_kernel(x_ref, pre_ref, win_ref, wout_ref, post_ref, o_ref, xn_ref, act_ref):
    tm = x_ref.shape[0]
    halves = (slice(0, tm // 2), slice(tm // 2, tm))
    slabs = list(range(0, D_FF, FFN_COL_CHUNK))

    def prenorm(r):
        xn_ref[r, :] = _rms(x_ref[r, :], pre_ref[...]).astype(BF16)

    def swiglu(r, lo):
        gate = _dot(xn_ref[r, :], win_ref[:, lo:lo + FFN_COL_CHUNK])
        up = _dot(xn_ref[r, :], win_ref[:, D_FF + lo:D_FF + lo + FFN_COL_CHUNK])
        act_ref[r, lo:lo + FFN_COL_CHUNK] = (gate * _sigmoid(gate) * up).astype(BF16)

    def project(r):
        h = _dot(act_ref[r, :], wout_ref[...])
        o_ref[r, :] = x_ref[r, :] + 0.5 * _rms(h, post_ref[...])

    first, second = halves
    prenorm(first)
    swiglu(first, slabs[0])
    prenorm(second)
    for lo in slabs[1:]:
        swiglu(first, lo)
    swiglu(second, slabs[0])
    project(first)
    for lo in slabs[1:]:
        swiglu(second, lo)
    project(second)


def _ffn(x2d, layer, pre, w_in, w_out, post):
    t = x2d.shape[0]
    tm = min(FFN_BLOCK, t)
    return pl.pallas_call(
        _ffn_kernel,
        grid=(t // tm,),
        in_specs=[pl.BlockSpec((tm, D_MODEL), lambda i: (i, 0)),
                  _resident((1, D_MODEL)),
                  _resident_layer((D_MODEL, 2 * D_FF), layer),
                  _resident_layer((D_FF, D_MODEL), layer),
                  _resident((1, D_MODEL))],
        out_specs=pl.BlockSpec((tm, D_MODEL), lambda i: (i, 0)),
        out_shape=jax.ShapeDtypeStruct((t, D_MODEL), F32),
        scratch_shapes=[pltpu.VMEM((tm, D_MODEL), BF16),
                        pltpu.VMEM((tm, D_FF), BF16)],
        compiler_params=pltpu.CompilerParams(
            dimension_semantics=("arbitrary",), vmem_limit_bytes=VMEM_LIMIT),
        name="ffn",
    )(x2d, pre[layer][None, :], w_in, w_out, post[layer][None, :])


_Slot = collections.namedtuple(
    "_Slot", ["zs", "xs", "bc", "bT", "q", "qd", "k", "kdT", "v", "gs", "dt", "dtT"])


def _mixer_kernel(xp_ref, xr_ref, cos_ref, sin_ref, pre_ref, wmain_ref, wdt_ref, wdtT_ref, convw_ref,
                  convb_ref, dtb_row_ref, dtb_col_ref, alog_row_ref, alog_col_ref, dskip_ref,
                  ssmnorm_ref, retnorm_ref, wout_ref, post_ref, expand_ref, tril_ref, triu_ref,
                  dmat_ref, qdec_ref, kdec_ref, cdec_ref,
                  o_ref,
                  u_ref, xbc_ref, y_ref, cat_ref, mixed_ref, sstate_ref, rstate_ref, *slot_refs,
                  tb, nj):
    s = pl.program_id(0)
    n = len(_Slot._fields)
    slot_a = _Slot(*slot_refs[:n])
    slot_b = _Slot(*slot_refs[n:])

    @pl.when(s == 0)
    def _():
        for ref in slot_b:
            ref[...] = jnp.zeros(ref.shape, ref.dtype)

    @pl.when(lax.rem(s, nj) == 0)
    def _():
        xbc_ref[0:SUBLANES, :] = jnp.zeros((SUBLANES, CONV_DIM), F32)

    @pl.when(jnp.logical_or(s == 0, lax.rem(s + nj - 1, nj) == 0))
    def _():
        sstate_ref[...] = jnp.zeros(sstate_ref.shape, F32)
        rstate_ref[...] = jnp.zeros(rstate_ref.shape, F32)

    refs = (xp_ref, xr_ref, cos_ref, sin_ref, pre_ref, wmain_ref, wdt_ref, wdtT_ref, convw_ref,
            convb_ref, dtb_row_ref, dtb_col_ref, alog_row_ref, alog_col_ref, dskip_ref, ssmnorm_ref,
            retnorm_ref, wout_ref, post_ref, expand_ref, tril_ref, triu_ref, dmat_ref, qdec_ref,
            kdec_ref, cdec_ref, o_ref, u_ref, xbc_ref, y_ref, cat_ref, mixed_ref, sstate_ref,
            rstate_ref)

    @pl.when(lax.rem(s, 2) == 0)
    def _():
        _mixer_step(*refs, slot_a, slot_b, tb=tb)

    @pl.when(lax.rem(s, 2) == 1)
    def _():
        _mixer_step(*refs, slot_b, slot_a, tb=tb)


def _mixer_step(xp_ref, xr_ref, cos_ref, sin_ref, pre_ref, wmain_ref, wdt_ref, wdtT_ref, convw_ref,
                convb_ref, dtb_row_ref, dtb_col_ref, alog_row_ref, alog_col_ref, dskip_ref,
                ssmnorm_ref, retnorm_ref, wout_ref, post_ref, expand_ref, tril_ref, triu_ref,
                dmat_ref, qdec_ref, kdec_ref, cdec_ref, o_ref,
                u_ref, xbc_ref, y_ref, cat_ref, mixed_ref, sstate_ref, rstate_ref, P, S, *, tb):
    L = CHUNK

    def proj(lo, hi):
        return _dot(u_ref[...], wmain_ref[:, lo:hi])

    def p_norm():
        u_ref[...] = _rms(xp_ref[...], pre_ref[...]).astype(BF16)

    def p_gate(dst, base, j):
        def piece():
            sl = slice(j * SLAB, (j + 1) * SLAB)
            t = proj(base + j * SLAB, base + (j + 1) * SLAB)
            dst[:, sl] = t * _sigmoid(t)
        return piece

    def p_value(j):
        def piece():
            sl = slice(j * SLAB, (j + 1) * SLAB)
            P.v[:, sl] = proj(_V0 + j * SLAB, _V0 + (j + 1) * SLAB).astype(BF16)
        return piece

    def p_rotary(j):
        def piece():
            q = proj(_Q0 + j * SLAB, _Q0 + (j + 1) * SLAB)
            k = proj(_K0 + j * SLAB, _K0 + (j + 1) * SLAB)
            cos = cos_ref[...]
            sin = sin_ref[...]
            for hh in range(SLAB // RET_QK_DIM):
                loc = slice(hh * RET_QK_DIM, (hh + 1) * RET_QK_DIM)
                sl = slice(j * SLAB + hh * RET_QK_DIM, j * SLAB + (hh + 1) * RET_QK_DIM)
                qh = q[:, loc]
                kh = k[:, loc]
                qr = qh * cos + pltpu.roll(qh, RET_QK_DIM // 2, 1) * sin
                kr = (kh * cos + pltpu.roll(kh, RET_QK_DIM // 2, 1) * sin) * (RET_QK_DIM ** -0.5)
                P.q[:, sl] = qr.astype(BF16)
                P.qd[:, sl] = (qr * qdec_ref[:, sl]).astype(BF16)
                P.k[:, sl] = kr.astype(BF16)
                P.kdT[sl, :] = (kr * kdec_ref[:, sl]).T.astype(BF16)
        return piece

    def p_dt():
        P.dt[...] = _softplus(_dot(u_ref[...], wdt_ref[...]) + dtb_row_ref[...])
        P.dtT[...] = _softplus(_dot_nt(wdtT_ref[...], u_ref[...]) + dtb_col_ref[...])

    def p_conv(j):
        def piece():
            sl = slice(j * SLAB, (j + 1) * SLAB)
            xbc_ref[SUBLANES:SUBLANES + tb, sl] = proj(_X0 + j * SLAB, _X0 + (j + 1) * SLAB)
            conv = convb_ref[:, sl]
            for kk in range(CONV_K):
                off = SUBLANES - (CONV_K - 1) + kk
                conv = conv + convw_ref[kk:kk + 1, sl] * xbc_ref[off:off + tb, sl]
            act = conv * _sigmoid(conv)
            xbc_ref[0:SUBLANES, sl] = xbc_ref[tb:tb + SUBLANES, sl]
            if (j + 1) * SLAB <= SSM_WIDTH:
                P.xs[:, sl] = act
            else:
                lo = j * SLAB - SSM_WIDTH
                P.bc[:, lo:lo + SLAB] = act.astype(BF16)
                if lo < BC_WIDTH:
                    P.bT[lo:lo + SLAB, :] = act.T.astype(BF16)
        return piece

    a_row = -jnp.exp(alog_row_ref[...])
    a_col = -jnp.exp(alog_col_ref[...])
    rows = lax.broadcasted_iota(jnp.int32, (L, L), 0)
    cols = lax.broadcasted_iota(jnp.int32, (L, L), 1)
    causal = rows >= cols
    lane = lax.broadcasted_iota(jnp.int32, (L, LANES), 1)
    low_half = lane < SSM_HEAD_DIM

    def s_chunk_pieces(c):
        r = slice(c * L, (c + 1) * L)
        ctx = {}

        def decays():
            dt_c = S.dt[r, :]
            dtT_c = S.dtT[:, r]
            parts = _dot(tril_ref[...], jnp.concatenate(_split3(dt_c * a_row), axis=1))
            acum = parts[:, 0:LANES] + parts[:, LANES:2 * LANES] + parts[:, 2 * LANES:3 * LANES]
            partsT = _dot(jnp.concatenate(_split3(dtT_c * a_col), axis=0), triu_ref[...])
            acumT = (partsT[0:SSM_HEADS] + partsT[SSM_HEADS:2 * SSM_HEADS]
                     + partsT[2 * SSM_HEADS:3 * SSM_HEADS])
            ctx["acum"] = acum
            ctx["lrow"] = acumT - jnp.log(dtT_c)
            e = jnp.exp(acum)
            w = jnp.exp(acum[L - 1:L, :] - acum) * dt_c
            e_hi = e.astype(BF16).astype(F32)
            w_hi = w.astype(BF16).astype(F32)
            packed = jnp.where(lane < SSM_HEADS, e_hi,
                               jnp.where(lane < 2 * SSM_HEADS, e - e_hi,
                                         jnp.where(lane < 3 * SSM_HEADS, w_hi,
                                                   jnp.where(lane < 4 * SSM_HEADS, w - w_hi, 0.0))))
            ex = _dot(packed.astype(BF16), expand_ref[...])
            ctx["ex_e"] = ex[:, 0:SSM_WIDTH]
            ctx["xw"] = (S.xs[r, :] * ex[:, SSM_WIDTH:2 * SSM_WIDTH]).astype(BF16)

        def group_open(gi):
            def piece():
                gsl = slice(gi * GROUP_WIDTH, (gi + 1) * GROUP_WIDTH)
                b_g = S.bc[r, gi * SSM_STATE:(gi + 1) * SSM_STATE]
                c_g = S.bc[r, BC_WIDTH + gi * SSM_STATE:BC_WIDTH + (gi + 1) * SSM_STATE]
                ctx["cb"] = _dot_nt(c_g, b_g)
                st = sstate_ref[gi]
                ctx["y_inter"] = _dot(c_g, st.astype(BF16)) * ctx["ex_e"][:, gsl]
                sstate_ref[gi] = (st * ctx["ex_e"][L - 1:L, gsl]
                                  + _dot(S.bT[gi * SSM_STATE:(gi + 1) * SSM_STATE, r], ctx["xw"][:, gsl]))
            return piece

        def head_pair(gi, p):
            def piece():
                h0 = gi * HEADS_PER_GROUP + 2 * p
                psl = slice(h0 * SSM_HEAD_DIM, (h0 + 2) * SSM_HEAD_DIM)
                gmats = []
                for h in (h0, h0 + 1):
                    diff = ctx["acum"][:, h:h + 1] - ctx["lrow"][h:h + 1, :]
                    gmats.append(ctx["cb"] * jnp.exp(jnp.where(causal, diff, -jnp.inf)))
                lhs = jnp.concatenate(gmats, axis=1).astype(BF16)
                xp = S.xs[r, psl]
                rhs = jnp.concatenate([jnp.where(low_half, xp, 0.0),
                                       jnp.where(low_half, 0.0, xp)], axis=0).astype(BF16)
                y = (_dot(lhs, rhs) + ctx["y_inter"][:, 2 * p * SSM_HEAD_DIM:(2 * p + 2) * SSM_HEAD_DIM]
                     + xp * dskip_ref[:, psl])
                y_ref[:, psl] = y * S.zs[r, psl]
            return piece

        def ssd_norm():
            cat_ref[r, 0:SSM_WIDTH] = _rms(y_ref[...], ssmnorm_ref[...]).astype(BF16)

        def retention(h):
            def piece():
                qsl = slice(h * RET_QK_DIM, (h + 1) * RET_QK_DIM)
                vsl = slice(h * RET_V_DIM, (h + 1) * RET_V_DIM)
                vh = S.v[r, vsl]
                sc = _dot_nt(S.q[r, qsl], S.k[r, qsl]) * dmat_ref[h]
                st = rstate_ref[h]
                lhs = jnp.concatenate([sc.astype(BF16), S.qd[r, qsl]], axis=1)
                rhs = jnp.concatenate([vh, st.astype(BF16)], axis=0)
                rh = _dot(lhs, rhs)
                rstate_ref[h] = st * cdec_ref[:, vsl] + _dot(S.kdT[qsl, r], vh)
                d = rh - jnp.mean(rh, axis=-1, keepdims=True)
                var = jnp.mean(d * d, axis=-1, keepdims=True)
                rn = d * lax.rsqrt(var + RET_NORM_EPS) * retnorm_ref[:, vsl]
                cat_ref[r, SSM_WIDTH + h * RET_V_DIM:SSM_WIDTH + (h + 1) * RET_V_DIM] = (
                    S.gs[r, vsl] * rn).astype(BF16)
            return piece

        pieces = [decays]
        for gi in range(SSM_GROUPS):
            pieces.append(group_open(gi))
            pieces += [head_pair(gi, p) for p in range(HEADS_PER_GROUP // 2)]
        pieces.append(ssd_norm)
        pieces += [retention(h) for h in range(RET_HEADS)]
        return pieces

    def s_out(j):
        def piece():
            sl = slice(j * SLAB, (j + 1) * SLAB)
            mixed_ref[:, sl] = _dot(cat_ref[...], wout_ref[:, sl])
        return piece

    def s_residual():
        o_ref[...] = xr_ref[...] + _rms(mixed_ref[...], post_ref[...])

    p_light = ([p_gate(P.zs, _Z0, j) for j in range(SSM_WIDTH // SLAB)]
               + [p_gate(P.gs, _G0, j) for j in range(RET_WIDTH // SLAB)]
               + [p_rotary(j) for j in range(RET_QK_WIDTH // SLAB)] + [p_dt])
    s_scan = [piece for c in range(tb // L) for piece in s_chunk_pieces(c)]
    p_norm()
    n_p, n_s = len(p_light), len(s_scan)
    done = 0
    for i, piece in enumerate(p_light):
        piece()
        upto = (i + 1) * n_s // n_p
        for sp in s_scan[done:upto]:
            sp()
        done = upto
    convs = [p_conv(j) for j in range(CONV_DIM // SLAB)]
    outs = [s_out(j) for j in range(D_MODEL // SLAB)]
    values = [p_value(j) for j in range(RET_WIDTH // SLAB)]
    for out_piece, conv_piece in zip(outs, convs):
        out_piece()
        conv_piece()
    s_residual()
    rest = convs[len(outs):]
    for i, piece in enumerate(values):
        piece()
        if i < len(rest):
            rest[i]()


def _retention_tables(tb):
    log_gamma = jnp.log(1.0 - 2.0 ** (-5.0 - jnp.arange(RET_HEADS, dtype=F32)))
    pos = jnp.arange(CHUNK, dtype=F32)
    rel = pos[:, None] - pos[None, :]
    dmat = jnp.exp(jnp.where((rel >= 0)[None], rel[None] * log_gamma[:, None, None], -jnp.inf))
    q_decay = jnp.exp((pos[:, None] + 1.0) * log_gamma[None])
    k_decay = jnp.exp((CHUNK - 1.0 - pos[:, None]) * log_gamma[None])
    chunk_decay = jnp.exp(CHUNK * log_gamma)
    qdec = jnp.tile(jnp.repeat(q_decay, RET_QK_DIM, axis=1), (tb // CHUNK, 1))
    kdec = jnp.tile(jnp.repeat(k_decay, RET_QK_DIM, axis=1), (tb // CHUNK, 1))
    cdec = jnp.repeat(chunk_decay, RET_V_DIM)[None, :]
    return dmat, qdec, kdec, cdec


def _constant_matrices():
    e = np.zeros((LANES, 2 * SSM_WIDTH), np.float32)
    for g in range(DT_COPIES):
        for h in range(SSM_HEADS):
            c0 = (g // 2) * SSM_WIDTH + h * SSM_HEAD_DIM
            e[g * SSM_HEADS + h, c0:c0 + SSM_HEAD_DIM] = 1.0
    tril = np.tril(np.ones((CHUNK, CHUNK), np.float32))
    return jnp.asarray(e, BF16), jnp.asarray(tril, BF16), jnp.asarray(tril.T, BF16)


def _slot_scratch(tb):
    return list(_Slot(
        zs=pltpu.VMEM((tb, SSM_WIDTH), F32),
        xs=pltpu.VMEM((tb, SSM_WIDTH), F32),
        bc=pltpu.VMEM((tb, 2 * BC_WIDTH), BF16),
        bT=pltpu.VMEM((BC_WIDTH, tb), BF16),
        q=pltpu.VMEM((tb, RET_QK_WIDTH), BF16),
        qd=pltpu.VMEM((tb, RET_QK_WIDTH), BF16),
        k=pltpu.VMEM((tb, RET_QK_WIDTH), BF16),
        kdT=pltpu.VMEM((RET_QK_WIDTH, tb), BF16),
        v=pltpu.VMEM((tb, RET_WIDTH), BF16),
        gs=pltpu.VMEM((tb, RET_WIDTH), F32),
        dt=pltpu.VMEM((tb, LANES), F32),
        dtT=pltpu.VMEM((SSM_HEADS, tb), F32),
    ))


def _mixer(x2d, batch, seq, layer, cos, sin, pre, w_in, w_main, conv_w, conv_b, dt_bias, a_log, d_skip,
           ssm_norm, ret_norm, w_out, post):
    tb = min(MIX_BLOCK, seq)
    nj = seq // tb
    nblocks = batch * nj
    w_dt = w_in[:, _X1:_X1 + SSM_HEADS]
    lane_pad = (0, LANES - DT_COPIES * SSM_HEADS)
    w_dt_pad = jnp.pad(jnp.tile(w_dt, (1, DT_COPIES)), ((0, 0), lane_pad)).astype(BF16)
    w_dt_t = w_dt.T.astype(BF16)
    dtb_row = jnp.pad(jnp.tile(dt_bias, DT_COPIES), lane_pad)[None, :]
    alog_row = jnp.pad(jnp.tile(a_log, DT_COPIES), lane_pad)[None, :]
    dskip = jnp.repeat(d_skip, SSM_HEAD_DIM)[None, :]
    dmat, qdec, kdec, cdec = _retention_tables(tb)
    expand, tril, triu = _constant_matrices()

    cur = lambda s: (jnp.minimum(s, nblocks - 1), 0)
    prev = lambda s: (jnp.maximum(s - 1, 0), 0)
    operands = [
        (x2d, pl.BlockSpec((tb, D_MODEL), cur)),
        (x2d, pl.BlockSpec((tb, D_MODEL), prev)),
        (cos, pl.BlockSpec((tb, LANES), cur)),
        (sin, pl.BlockSpec((tb, LANES), cur)),
        (pre[None, :], None),
        (w_main, _resident_layer(w_main.shape[1:], layer)),
        (w_dt_pad, None),
        (w_dt_t, None),
        (conv_w, None),
        (conv_b[None, :], None),
        (dtb_row, None),
        (dt_bias[:, None], None),
        (alog_row, None),
        (a_log[:, None], None),
        (dskip, None),
        (ssm_norm[None, :], None),
        (ret_norm[None, :], None),
        (w_out, _resident_layer(w_out.shape[1:], layer)),
        (post[None, :], None),
        (expand, None),
        (tril, None),
        (triu, None),
        (dmat, None),
        (qdec, None),
        (kdec, None),
        (cdec, None),
    ]
    args = [a for a, _ in operands]
    in_specs = [sp if sp is not None else _resident(a.shape) for a, sp in operands]
    return pl.pallas_call(
        functools.partial(_mixer_kernel, tb=tb, nj=nj),
        grid=(nblocks + 1,),
        in_specs=in_specs,
        out_specs=pl.BlockSpec((tb, D_MODEL), prev),
        out_shape=jax.ShapeDtypeStruct(x2d.shape, F32),
        scratch_shapes=[
            pltpu.VMEM((tb, D_MODEL), BF16),
            pltpu.VMEM((SUBLANES + tb, CONV_DIM), F32),
            pltpu.VMEM((CHUNK, SSM_WIDTH), F32),
            pltpu.VMEM((tb, D_INNER), BF16),
            pltpu.VMEM((tb, D_MODEL), F32),
            pltpu.VMEM((SSM_GROUPS, SSM_STATE, GROUP_WIDTH), F32),
            pltpu.VMEM((RET_HEADS, RET_QK_DIM, RET_V_DIM), F32),
        ] + 2 * _slot_scratch(tb),
        compiler_params=pltpu.CompilerParams(
            dimension_semantics=("arbitrary",), vmem_limit_bytes=VMEM_LIMIT),
        name="mixer",
    )(*args)


def kernel(x, positions, ffn1_pre_norm, ffn1_w_in, ffn1_w_out, ffn1_post_norm, mix_pre_norm, mix_w_in, conv_w, conv_b, dt_bias, a_log, d_skip, ssm_norm, ret_norm, mix_w_out, mix_post_norm, ffn2_pre_norm, ffn2_w_in, ffn2_w_out, ffn2_post_norm):
    batch, seq, _ = x.shape
    depth = ffn1_w_in.shape[0]
    cos, sin = _rope_tables(positions)
    ffn1_in = _prep_weights(ffn1_w_in, _cast_kernel, 2 * D_FF)
    ffn1_out = _prep_weights(ffn1_w_out, _cast_kernel, D_MODEL)
    ffn2_in = _prep_weights(ffn2_w_in, _cast_kernel, 2 * D_FF)
    ffn2_out = _prep_weights(ffn2_w_out, _cast_kernel, D_MODEL)
    mix_in = _prep_weights(mix_w_in, _mix_in_kernel, _G1)
    mix_out = _prep_weights(mix_w_out, _cast_kernel, D_MODEL)
    h = x.reshape(batch * seq, D_MODEL)
    for l in range(depth):
        h = _ffn(h, l, ffn1_pre_norm, ffn1_in, ffn1_out, ffn1_post_norm)
        h = _mixer(h, batch, seq, l, cos, sin, mix_pre_norm[l], mix_w_in[l], mix_in, conv_w[l], conv_b[l],
                   dt_bias[l], a_log[l], d_skip[l], ssm_norm[l], ret_norm[l], mix_out, mix_post_norm[l])
        h = _ffn(h, l, ffn2_pre_norm, ffn2_in, ffn2_out, ffn2_post_norm)
    return h.reshape(batch, seq, D_MODEL)
```

```python
import collections
import functools

import jax
import jax.numpy as jnp
import numpy as np
from jax import lax
from jax.experimental import pallas as pl
from jax.experimental.pallas import tpu as pltpu

F32 = jnp.float32
BF16 = jnp.bfloat16

D_MODEL = 1024
D_FF = 2816
NORM_EPS = 1e-6
RET_NORM_EPS = 1e-5

D_INNER = 2 * D_MODEL
SSM_WIDTH = D_INNER // 2
RET_WIDTH = D_INNER - SSM_WIDTH
SSM_HEAD_DIM = 64
SSM_HEADS = SSM_WIDTH // SSM_HEAD_DIM
SSM_GROUPS = 2
HEADS_PER_GROUP = SSM_HEADS // SSM_GROUPS
GROUP_WIDTH = HEADS_PER_GROUP * SSM_HEAD_DIM
SSM_STATE = 128
BC_WIDTH = SSM_GROUPS * SSM_STATE
CONV_K = 4
CONV_DIM = SSM_WIDTH + 2 * BC_WIDTH
RET_HEADS = 4
RET_V_DIM = RET_WIDTH // RET_HEADS
RET_QK_DIM = RET_V_DIM // 2
RET_QK_WIDTH = RET_HEADS * RET_QK_DIM
ROPE_BASE = 10000.0

LANES = 128
SUBLANES = 8

CHUNK = 128
MIX_BLOCK = 256
FFN_BLOCK = 1024
FFN_COL_CHUNK = 256
ROPE_BLOCK = 2048
VMEM_LIMIT = 56 * 1024 * 1024
DT_COPIES = 4
SLAB = 256
PREP_ROWS = 256

_Z0, _Z1 = 0, SSM_WIDTH
_X0, _X1 = _Z1, _Z1 + CONV_DIM
_Q0, _Q1 = _X1, _X1 + RET_QK_WIDTH
_K0, _K1 = _Q1, _Q1 + RET_QK_WIDTH
_V0, _V1 = _K1, _K1 + RET_WIDTH
_G0, _G1 = _V1, _V1 + RET_WIDTH


def _resident(shape):
    del shape
    return pl.BlockSpec(memory_space=pltpu.VMEM)


def _resident_layer(shape, layer):
    nd = len(shape)
    return pl.BlockSpec((None,) + tuple(shape), lambda *_: (layer,) + (0,) * nd,
                        pipeline_mode=pl.Buffered(1))


def _sigmoid(x):
    return 1.0 / (1.0 + jnp.exp(-x))


def _softplus(x):
    return jnp.maximum(x, 0.0) + jnp.log1p(jnp.exp(-jnp.abs(x)))


def _rms(x, w):
    ms = jnp.mean(x * x, axis=-1, keepdims=True)
    return x * lax.rsqrt(ms + NORM_EPS) * w


def _dot(a, b):
    return jnp.dot(a, b, preferred_element_type=F32)


def _dot_nt(a, b):
    return lax.dot_general(a, b, (((1,), (1,)), ((), ())), preferred_element_type=F32)


def _split3(v):
    p1 = v.astype(BF16)
    r1 = v - p1.astype(F32)
    p2 = r1.astype(BF16)
    p3 = (r1 - p2.astype(F32)).astype(BF16)
    return p1, p2, p3


def _cast_kernel(w_ref, o_ref):
    o_ref[...] = w_ref[...].astype(BF16)


def _mix_in_kernel(w_ref, o_ref):
    o_ref[:, 0:_X1] = w_ref[:, 0:_X1].astype(BF16)
    o_ref[:, _X1:_G1] = w_ref[:, _X1 + SSM_HEADS:_G1 + SSM_HEADS].astype(BF16)


def _prep_weights(w, body, out_cols):
    layers, k, n = w.shape
    return pl.pallas_call(
        body,
        grid=(layers, k // PREP_ROWS),
        in_specs=[pl.BlockSpec((None, PREP_ROWS, n), lambda l, i: (l, i, 0))],
        out_specs=pl.BlockSpec((None, PREP_ROWS, out_cols), lambda l, i: (l, i, 0)),
        out_shape=jax.ShapeDtypeStruct((layers, k, out_cols), BF16),
        compiler_params=pltpu.CompilerParams(
            dimension_semantics=("arbitrary", "arbitrary"), vmem_limit_bytes=VMEM_LIMIT),
        name="prep_weights",
    )(w)


def _rope_kernel(pos_ref, invf_ref, sign_ref, cos_ref, sin_ref):
    ang = pos_ref[...].astype(F32) * invf_ref[...]
    cos_ref[...] = jnp.cos(ang)
    sin_ref[...] = jnp.sin(ang) * sign_ref[...]


def _rope_tables(positions):
    t = positions.size
    tb = min(ROPE_BLOCK, t)
    half = RET_QK_DIM // 2
    inv_freq = ROPE_BASE ** (-jnp.arange(half, dtype=F32) / half)
    invf = jnp.concatenate([inv_freq, inv_freq])[None, :]
    sign = jnp.concatenate([-jnp.ones((half,), F32), jnp.ones((half,), F32)])[None, :]
    return pl.pallas_call(
        _rope_kernel,
        grid=(t // tb,),
        in_specs=[pl.BlockSpec((tb, 1), lambda i: (i, 0)),
                  pl.BlockSpec((1, LANES), lambda i: (0, 0)),
                  pl.BlockSpec((1, LANES), lambda i: (0, 0))],
        out_specs=[pl.BlockSpec((tb, LANES), lambda i: (i, 0)),
                   pl.BlockSpec((tb, LANES), lambda i: (i, 0))],
        out_shape=[jax.ShapeDtypeStruct((t, LANES), F32)] * 2,
        name="rope_tables",
    )(positions.reshape(t, 1), invf, sign)


def _ffn_kernel(x_ref, pre_ref, win_ref, wout_ref, post_ref, o_ref, xn_ref, act_ref):
    tm = x_ref.shape[0]
    halves = (slice(0, tm // 2), slice(tm // 2, tm))
    slabs = list(range(0, D_FF, FFN_COL_CHUNK))

    def prenorm(r):
        xn_ref[r, :] = _rms(x_ref[r, :], pre_ref[...]).astype(BF16)

    def swiglu(r, lo):
        gate = _dot(xn_ref[r, :], win_ref[:, lo:lo + FFN_COL_CHUNK])
        up = _dot(xn_ref[r, :], win_ref[:, D_FF + lo:D_FF + lo + FFN_COL_CHUNK])
        act_ref[r, lo:lo + FFN_COL_CHUNK] = (gate * _sigmoid(gate) * up).astype(BF16)

    def project(r):
        h = _dot(act_ref[r, :], wout_ref[...])
        o_ref[r, :] = x_ref[r, :] + 0.5 * _rms(h, post_ref[...])

    first, second = halves
    prenorm(first)
    swiglu(first, slabs[0])
    prenorm(second)
    for lo in slabs[1:]:
        swiglu(first, lo)
    swiglu(second, slabs[0])
    project(first)
    for lo in slabs[1:]:
        swiglu(second, lo)
    project(second)


def _ffn(x2d, layer, pre, w_in, w_out, post):
    t = x2d.shape[0]
    tm = min(FFN_BLOCK, t)
    return pl.pallas_call(
        _ffn_kernel,
        grid=(t // tm,),
        in_specs=[pl.BlockSpec((tm, D_MODEL), lambda i: (i, 0)),
                  _resident((1, D_MODEL)),
                  _resident_layer((D_MODEL, 2 * D_FF), layer),
                  _resident_layer((D_FF, D_MODEL), layer),
                  _resident((1, D_MODEL))],
        out_specs=pl.BlockSpec((tm, D_MODEL), lambda i: (i, 0)),
        out_shape=jax.ShapeDtypeStruct((t, D_MODEL), F32),
        scratch_shapes=[pltpu.VMEM((tm, D_MODEL), BF16),
                        pltpu.VMEM((tm, D_FF), BF16)],
        compiler_params=pltpu.CompilerParams(
            dimension_semantics=("arbitrary",), vmem_limit_bytes=VMEM_LIMIT),
        name="ffn",
    )(x2d, pre[layer][None, :], w_in, w_out, post[layer][None, :])


_Slot = collections.namedtuple(
    "_Slot", ["zs", "xs", "bc", "bT", "q", "qd", "k", "kdT", "v", "gs", "dt", "dtT"])


def _mixer_kernel(xp1_ref, xp2_ref, xr_ref, cos1_ref, sin1_ref, cos2_ref, sin2_ref, pre_ref, wmain_ref,
                  wdt_ref, wdtT_ref, convw_ref, convb_ref, dtb_row_ref, dtb_col_ref, alog_row_ref,
                  alog_col_ref, dskip_ref, ssmnorm_ref, retnorm_ref, wout_ref, post_ref, expand_ref,
                  tril_ref, triu_ref, dmat_ref, qdec_ref, kdec_ref, cdec_ref,
                  o_ref,
                  u_ref, xbc_ref, y_ref, cat_ref, mixed_ref, sstate_ref, rstate_ref, *slot_refs,
                  tb, nj):
    i = pl.program_id(0)
    n = len(_Slot._fields)
    slot_a = _Slot(*slot_refs[:n])
    slot_b = _Slot(*slot_refs[n:])

    @pl.when(i == 0)
    def _():
        for ref in slot_a:
            ref[...] = jnp.zeros(ref.shape, ref.dtype)
        xbc_ref[0:SUBLANES, :] = jnp.zeros((SUBLANES, CONV_DIM), F32)

    @pl.when(jnp.logical_or(i == 0, lax.rem(2 * i + nj - 2, nj) == 0))
    def _():
        sstate_ref[...] = jnp.zeros(sstate_ref.shape, F32)
        rstate_ref[...] = jnp.zeros(rstate_ref.shape, F32)

    refs = (pre_ref, wmain_ref, wdt_ref, wdtT_ref, convw_ref, convb_ref, dtb_row_ref, dtb_col_ref,
            alog_row_ref, alog_col_ref, dskip_ref, ssmnorm_ref, retnorm_ref, wout_ref, post_ref,
            expand_ref, tril_ref, triu_ref, dmat_ref, qdec_ref, kdec_ref, cdec_ref,
            u_ref, xbc_ref, y_ref, cat_ref, mixed_ref, sstate_ref, rstate_ref)
    first = pl.ds(0, tb)
    second = pl.ds(tb, tb)
    _mixer_step(xp1_ref, cos1_ref, sin1_ref, xr_ref.at[first], o_ref.at[first], *refs,
                slot_b, slot_a, None, tb=tb)
    _mixer_step(xp2_ref, cos2_ref, sin2_ref, xr_ref.at[second], o_ref.at[second], *refs,
                slot_a, slot_b, lax.rem(2 * i, nj) == 0, tb=tb)


def _mixer_step(xp_ref, cos_ref, sin_ref, xr_ref, o_ref, pre_ref, wmain_ref, wdt_ref, wdtT_ref,
                convw_ref, convb_ref, dtb_row_ref, dtb_col_ref, alog_row_ref, alog_col_ref, dskip_ref,
                ssmnorm_ref, retnorm_ref, wout_ref, post_ref, expand_ref, tril_ref, triu_ref,
                dmat_ref, qdec_ref, kdec_ref, cdec_ref,
                u_ref, xbc_ref, y_ref, cat_ref, mixed_ref, sstate_ref, rstate_ref, P, S, new_sequence,
                *, tb):
    L = CHUNK
    if new_sequence is not None:
        tail = xbc_ref[0:SUBLANES, :]
        xbc_ref[0:SUBLANES, :] = jnp.where(new_sequence, jnp.zeros_like(tail), tail)

    def proj(lo, hi):
        return _dot(u_ref[...], wmain_ref[:, lo:hi])

    def p_norm():
        u_ref[...] = _rms(xp_ref[...], pre_ref[...]).astype(BF16)

    def p_gate(dst, base, j):
        def piece():
            sl = slice(j * SLAB, (j + 1) * SLAB)
            t = proj(base + j * SLAB, base + (j + 1) * SLAB)
            dst[:, sl] = t * _sigmoid(t)
        return piece

    def p_value(j):
        def piece():
            sl = slice(j * SLAB, (j + 1) * SLAB)
            P.v[:, sl] = proj(_V0 + j * SLAB, _V0 + (j + 1) * SLAB).astype(BF16)
        return piece

    def p_rotary(j):
        def piece():
            q = proj(_Q0 + j * SLAB, _Q0 + (j + 1) * SLAB)
            k = proj(_K0 + j * SLAB, _K0 + (j + 1) * SLAB)
            cos = cos_ref[...]
            sin = sin_ref[...]
            for hh in range(SLAB // RET_QK_DIM):
                loc = slice(hh * RET_QK_DIM, (hh + 1) * RET_QK_DIM)
                sl = slice(j * SLAB + hh * RET_QK_DIM, j * SLAB + (hh + 1) * RET_QK_DIM)
                qh = q[:, loc]
                kh = k[:, loc]
                qr = qh * cos + pltpu.roll(qh, RET_QK_DIM // 2, 1) * sin
                kr = (kh * cos + pltpu.roll(kh, RET_QK_DIM // 2, 1) * sin) * (RET_QK_DIM ** -0.5)
                P.q[:, sl] = qr.astype(BF16)
                P.qd[:, sl] = (qr * qdec_ref[:, sl]).astype(BF16)
                P.k[:, sl] = kr.astype(BF16)
                P.kdT[sl, :] = (kr * kdec_ref[:, sl]).T.astype(BF16)
        return piece

    def p_dt():
        P.dt[...] = _softplus(_dot(u_ref[...], wdt_ref[...]) + dtb_row_ref[...])
        P.dtT[...] = _softplus(_dot_nt(wdtT_ref[...], u_ref[...]) + dtb_col_ref[...])

    def p_conv(j):
        def piece():
            sl = slice(j * SLAB, (j + 1) * SLAB)
            xbc_ref[SUBLANES:SUBLANES + tb, sl] = proj(_X0 + j * SLAB, _X0 + (j + 1) * SLAB)
            conv = convb_ref[:, sl]
            for kk in range(CONV_K):
                off = SUBLANES - (CONV_K - 1) + kk
                conv = conv + convw_ref[kk:kk + 1, sl] * xbc_ref[off:off + tb, sl]
            act = conv * _sigmoid(conv)
            xbc_ref[0:SUBLANES, sl] = xbc_ref[tb:tb + SUBLANES, sl]
            if (j + 1) * SLAB <= SSM_WIDTH:
                P.xs[:, sl] = act
            else:
                lo = j * SLAB - SSM_WIDTH
                P.bc[:, lo:lo + SLAB] = act.astype(BF16)
                if lo < BC_WIDTH:
                    P.bT[lo:lo + SLAB, :] = act.T.astype(BF16)
        return piece

    a_row = -jnp.exp(alog_row_ref[...])
    a_col = -jnp.exp(alog_col_ref[...])
    rows = lax.broadcasted_iota(jnp.int32, (L, L), 0)
    cols = lax.broadcasted_iota(jnp.int32, (L, L), 1)
    causal = rows >= cols
    lane = lax.broadcasted_iota(jnp.int32, (L, LANES), 1)
    low_half = lane < SSM_HEAD_DIM

    def s_chunk_pieces(c):
        r = slice(c * L, (c + 1) * L)
        ctx = {}

        def decays():
            dt_c = S.dt[r, :]
            dtT_c = S.dtT[:, r]
            parts = _dot(tril_ref[...], jnp.concatenate(_split3(dt_c * a_row), axis=1))
            acum = parts[:, 0:LANES] + parts[:, LANES:2 * LANES] + parts[:, 2 * LANES:3 * LANES]
            partsT = _dot(jnp.concatenate(_split3(dtT_c * a_col), axis=0), triu_ref[...])
            acumT = (partsT[0:SSM_HEADS] + partsT[SSM_HEADS:2 * SSM_HEADS]
                     + partsT[2 * SSM_HEADS:3 * SSM_HEADS])
            ctx["acum"] = acum
            ctx["lrow"] = acumT - jnp.log(dtT_c)
            e = jnp.exp(acum)
            w = jnp.exp(acum[L - 1:L, :] - acum) * dt_c
            e_hi = e.astype(BF16).astype(F32)
            w_hi = w.astype(BF16).astype(F32)
            packed = jnp.where(lane < SSM_HEADS, e_hi,
                               jnp.where(lane < 2 * SSM_HEADS, e - e_hi,
                                         jnp.where(lane < 3 * SSM_HEADS, w_hi,
                                                   jnp.where(lane < 4 * SSM_HEADS, w - w_hi, 0.0))))
            ex = _dot(packed.astype(BF16), expand_ref[...])
            ctx["ex_e"] = ex[:, 0:SSM_WIDTH]
            ctx["xw"] = (S.xs[r, :] * ex[:, SSM_WIDTH:2 * SSM_WIDTH]).astype(BF16)

        def group_open(gi):
            def piece():
                gsl = slice(gi * GROUP_WIDTH, (gi + 1) * GROUP_WIDTH)
                b_g = S.bc[r, gi * SSM_STATE:(gi + 1) * SSM_STATE]
                c_g = S.bc[r, BC_WIDTH + gi * SSM_STATE:BC_WIDTH + (gi + 1) * SSM_STATE]
                ctx["cb"] = _dot_nt(c_g, b_g)
                st = sstate_ref[gi]
                ctx["y_inter"] = _dot(c_g, st.astype(BF16)) * ctx["ex_e"][:, gsl]
                sstate_ref[gi] = (st * ctx["ex_e"][L - 1:L, gsl]
                                  + _dot(S.bT[gi * SSM_STATE:(gi + 1) * SSM_STATE, r], ctx["xw"][:, gsl]))
            return piece

        def head_pair(gi, p):
            def piece():
                h0 = gi * HEADS_PER_GROUP + 2 * p
                psl = slice(h0 * SSM_HEAD_DIM, (h0 + 2) * SSM_HEAD_DIM)
                gmats = []
                for h in (h0, h0 + 1):
                    diff = ctx["acum"][:, h:h + 1] - ctx["lrow"][h:h + 1, :]
                    gmats.append(ctx["cb"] * jnp.exp(jnp.where(causal, diff, -jnp.inf)))
                lhs = jnp.concatenate(gmats, axis=1).astype(BF16)
                xp = S.xs[r, psl]
                rhs = jnp.concatenate([jnp.where(low_half, xp, 0.0),
                                       jnp.where(low_half, 0.0, xp)], axis=0).astype(BF16)
                y = (_dot(lhs, rhs) + ctx["y_inter"][:, 2 * p * SSM_HEAD_DIM:(2 * p + 2) * SSM_HEAD_DIM]
                     + xp * dskip_ref[:, psl])
                y_ref[:, psl] = y * S.zs[r, psl]
            return piece

        def ssd_norm():
            cat_ref[r, 0:SSM_WIDTH] = _rms(y_ref[...], ssmnorm_ref[...]).astype(BF16)

        def retention(h):
            def piece():
                qsl = slice(h * RET_QK_DIM, (h + 1) * RET_QK_DIM)
                vsl = slice(h * RET_V_DIM, (h + 1) * RET_V_DIM)
                vh = S.v[r, vsl]
                sc = _dot_nt(S.q[r, qsl], S.k[r, qsl]) * dmat_ref[h]
                st = rstate_ref[h]
                lhs = jnp.concatenate([sc.astype(BF16), S.qd[r, qsl]], axis=1)
                rhs = jnp.concatenate([vh, st.astype(BF16)], axis=0)
                rh = _dot(lhs, rhs)
                rstate_ref[h] = st * cdec_ref[:, vsl] + _dot(S.kdT[qsl, r], vh)
                d = rh - jnp.mean(rh, axis=-1, keepdims=True)
                var = jnp.mean(d * d, axis=-1, keepdims=True)
                rn = d * lax.rsqrt(var + RET_NORM_EPS) * retnorm_ref[:, vsl]
                cat_ref[r, SSM_WIDTH + h * RET_V_DIM:SSM_WIDTH + (h + 1) * RET_V_DIM] = (
                    S.gs[r, vsl] * rn).astype(BF16)
            return piece

        pieces = [decays]
        for gi in range(SSM_GROUPS):
            pieces.append(group_open(gi))
            pieces += [head_pair(gi, p) for p in range(HEADS_PER_GROUP // 2)]
        pieces.append(ssd_norm)
        pieces += [retention(h) for h in range(RET_HEADS)]
        return pieces

    def s_out(j):
        def piece():
            sl = slice(j * SLAB, (j + 1) * SLAB)
            mixed_ref[:, sl] = _dot(cat_ref[...], wout_ref[:, sl])
        return piece

    def s_residual():
        o_ref[...] = xr_ref[...] + _rms(mixed_ref[...], post_ref[...])

    p_light = ([p_gate(P.zs, _Z0, j) for j in range(SSM_WIDTH // SLAB)]
               + [p_gate(P.gs, _G0, j) for j in range(RET_WIDTH // SLAB)]
               + [p_rotary(j) for j in range(RET_QK_WIDTH // SLAB)] + [p_dt])
    s_scan = [piece for c in range(tb // L) for piece in s_chunk_pieces(c)]
    p_norm()
    n_p, n_s = len(p_light), len(s_scan)
    done = 0
    for i, piece in enumerate(p_light):
        piece()
        upto = (i + 1) * n_s // n_p
        for sp in s_scan[done:upto]:
            sp()
        done = upto
    convs = [p_conv(j) for j in range(CONV_DIM // SLAB)]
    outs = [s_out(j) for j in range(D_MODEL // SLAB)]
    values = [p_value(j) for j in range(RET_WIDTH // SLAB)]
    for out_piece, conv_piece in zip(outs, convs):
        out_piece()
        conv_piece()
    s_residual()
    rest = convs[len(outs):]
    for i, piece in enumerate(values):
        piece()
        if i < len(rest):
            rest[i]()


def _retention_tables(tb):
    log_gamma = jnp.log(1.0 - 2.0 ** (-5.0 - jnp.arange(RET_HEADS, dtype=F32)))
    pos = jnp.arange(CHUNK, dtype=F32)
    rel = pos[:, None] - pos[None, :]
    dmat = jnp.exp(jnp.where((rel >= 0)[None], rel[None] * log_gamma[:, None, None], -jnp.inf))
    q_decay = jnp.exp((pos[:, None] + 1.0) * log_gamma[None])
    k_decay = jnp.exp((CHUNK - 1.0 - pos[:, None]) * log_gamma[None])
    chunk_decay = jnp.exp(CHUNK * log_gamma)
    qdec = jnp.tile(jnp.repeat(q_decay, RET_QK_DIM, axis=1), (tb // CHUNK, 1))
    kdec = jnp.tile(jnp.repeat(k_decay, RET_QK_DIM, axis=1), (tb // CHUNK, 1))
    cdec = jnp.repeat(chunk_decay, RET_V_DIM)[None, :]
    return dmat, qdec, kdec, cdec


def _constant_matrices():
    e = np.zeros((LANES, 2 * SSM_WIDTH), np.float32)
    for g in range(DT_COPIES):
        for h in range(SSM_HEADS):
            c0 = (g // 2) * SSM_WIDTH + h * SSM_HEAD_DIM
            e[g * SSM_HEADS + h, c0:c0 + SSM_HEAD_DIM] = 1.0
    tril = np.tril(np.ones((CHUNK, CHUNK), np.float32))
    return jnp.asarray(e, BF16), jnp.asarray(tril, BF16), jnp.asarray(tril.T, BF16)


def _slot_scratch(tb):
    return list(_Slot(
        zs=pltpu.VMEM((tb, SSM_WIDTH), F32),
        xs=pltpu.VMEM((tb, SSM_WIDTH), F32),
        bc=pltpu.VMEM((tb, 2 * BC_WIDTH), BF16),
        bT=pltpu.VMEM((BC_WIDTH, tb), BF16),
        q=pltpu.VMEM((tb, RET_QK_WIDTH), BF16),
        qd=pltpu.VMEM((tb, RET_QK_WIDTH), BF16),
        k=pltpu.VMEM((tb, RET_QK_WIDTH), BF16),
        kdT=pltpu.VMEM((RET_QK_WIDTH, tb), BF16),
        v=pltpu.VMEM((tb, RET_WIDTH), BF16),
        gs=pltpu.VMEM((tb, RET_WIDTH), F32),
        dt=pltpu.VMEM((tb, LANES), F32),
        dtT=pltpu.VMEM((SSM_HEADS, tb), F32),
    ))


def _mixer(x2d, batch, seq, layer, cos, sin, pre, w_in, w_main, conv_w, conv_b, dt_bias, a_log, d_skip,
           ssm_norm, ret_norm, w_out, post):
    tb = min(MIX_BLOCK, seq)
    nj = seq // tb
    nblocks = batch * nj
    w_dt = w_in[:, _X1:_X1 + SSM_HEADS]
    lane_pad = (0, LANES - DT_COPIES * SSM_HEADS)
    w_dt_pad = jnp.pad(jnp.tile(w_dt, (1, DT_COPIES)), ((0, 0), lane_pad)).astype(BF16)
    w_dt_t = w_dt.T.astype(BF16)
    dtb_row = jnp.pad(jnp.tile(dt_bias, DT_COPIES), lane_pad)[None, :]
    alog_row = jnp.pad(jnp.tile(a_log, DT_COPIES), lane_pad)[None, :]
    dskip = jnp.repeat(d_skip, SSM_HEAD_DIM)[None, :]
    dmat, qdec, kdec, cdec = _retention_tables(tb)
    expand, tril, triu = _constant_matrices()

    assert nj % 2 == 0, "two blocks per grid step need an even number of blocks per sequence"
    odd = lambda i: (jnp.clip(2 * i - 1, 0, nblocks - 1), 0)
    even = lambda i: (jnp.minimum(2 * i, nblocks - 1), 0)
    prev = lambda i: (jnp.maximum(i - 1, 0), 0)
    operands = [
        (x2d, pl.BlockSpec((tb, D_MODEL), odd)),
        (x2d, pl.BlockSpec((tb, D_MODEL), even)),
        (x2d, pl.BlockSpec((2 * tb, D_MODEL), prev)),
        (cos, pl.BlockSpec((tb, LANES), odd)),
        (sin, pl.BlockSpec((tb, LANES), odd)),
        (cos, pl.BlockSpec((tb, LANES), even)),
        (sin, pl.BlockSpec((tb, LANES), even)),
        (pre[None, :], None),
        (w_main, _resident_layer(w_main.shape[1:], layer)),
        (w_dt_pad, None),
        (w_dt_t, None),
        (conv_w, None),
        (conv_b[None, :], None),
        (dtb_row, None),
        (dt_bias[:, None], None),
        (alog_row, None),
        (a_log[:, None], None),
        (dskip, None),
        (ssm_norm[None, :], None),
        (ret_norm[None, :], None),
        (w_out, _resident_layer(w_out.shape[1:], layer)),
        (post[None, :], None),
        (expand, None),
        (tril, None),
        (triu, None),
        (dmat, None),
        (qdec, None),
        (kdec, None),
        (cdec, None),
    ]
    args = [a for a, _ in operands]
    in_specs = [sp if sp is not None else _resident(a.shape) for a, sp in operands]
    return pl.pallas_call(
        functools.partial(_mixer_kernel, tb=tb, nj=nj),
        grid=(nblocks // 2 + 1,),
        in_specs=in_specs,
        out_specs=pl.BlockSpec((2 * tb, D_MODEL), prev),
        out_shape=jax.ShapeDtypeStruct(x2d.shape, F32),
        scratch_shapes=[
            pltpu.VMEM((tb, D_MODEL), BF16),
            pltpu.VMEM((SUBLANES + tb, CONV_DIM), F32),
            pltpu.VMEM((CHUNK, SSM_WIDTH), F32),
            pltpu.VMEM((tb, D_INNER), BF16),
            pltpu.VMEM((tb, D_MODEL), F32),
            pltpu.VMEM((SSM_GROUPS, SSM_STATE, GROUP_WIDTH), F32),
            pltpu.VMEM((RET_HEADS, RET_QK_DIM, RET_V_DIM), F32),
        ] + 2 * _slot_scratch(tb),
        compiler_params=pltpu.CompilerParams(
            dimension_semantics=("arbitrary",), vmem_limit_bytes=VMEM_LIMIT),
        name="mixer",
    )(*args)


def kernel(x, positions, ffn1_pre_norm, ffn1_w_in, ffn1_w_out, ffn1_post_norm, mix_pre_norm, mix_w_in, conv_w, conv_b, dt_bias, a_log, d_skip, ssm_norm, ret_norm, mix_w_out, mix_post_norm, ffn2_pre_norm, ffn2_w_in, ffn2_w_out, ffn2_post_norm):
    batch, seq, _ = x.shape
    depth = ffn1_w_in.shape[0]
    cos, sin = _rope_tables(positions)
    ffn1_in = _prep_weights(ffn1_w_in, _cast_kernel, 2 * D_FF)
    ffn1_out = _prep_weights(ffn1_w_out, _cast_kernel, D_MODEL)
    ffn2_in = _prep_weights(ffn2_w_in, _cast_kernel, 2 * D_FF)
    ffn2_out = _prep_weights(ffn2_w_out, _cast_kernel, D_MODEL)
    mix_in = _prep_weights(mix_w_in, _mix_in_kernel, _G1)
    mix_out = _prep_weights(mix_w_out, _cast_kernel, D_MODEL)
    h = x.reshape(batch * seq, D_MODEL)
    for l in range(depth):
        h = _ffn(h, l, ffn1_pre_norm, ffn1_in, ffn1_out, ffn1_post_norm)
        h = _mixer(h, batch, seq, l, cos, sin, mix_pre_norm[l], mix_w_in[l], mix_in, conv_w[l], conv_b[l],
                   dt_bias[l], a_log[l], d_skip[l], ssm_norm[l], ret_norm[l], mix_out, mix_post_norm[l])
        h = _ffn(h, l, ffn2_pre_norm, ffn2_in, ffn2_out, ffn2_post_norm)
    return h.reshape(batch, seq, D_MODEL)
```

```python
import collections
import functools

import jax
import jax.numpy as jnp
import numpy as np
from jax import lax
from jax.experimental import pallas as pl
from jax.experimental.pallas import tpu as pltpu

F32 = jnp.float32
BF16 = jnp.bfloat16

D_MODEL = 1024
D_FF = 2816
NORM_EPS = 1e-6
RET_NORM_EPS = 1e-5

D_INNER = 2 * D_MODEL
SSM_WIDTH = D_INNER // 2
RET_WIDTH = D_INNER - SSM_WIDTH
SSM_HEAD_DIM = 64
SSM_HEADS = SSM_WIDTH // SSM_HEAD_DIM
SSM_GROUPS = 2
HEADS_PER_GROUP = SSM_HEADS // SSM_GROUPS
GROUP_WIDTH = HEADS_PER_GROUP * SSM_HEAD_DIM
SSM_STATE = 128
BC_WIDTH = SSM_GROUPS * SSM_STATE
CONV_K = 4
CONV_DIM = SSM_WIDTH + 2 * BC_WIDTH
RET_HEADS = 4
RET_V_DIM = RET_WIDTH // RET_HEADS
RET_QK_DIM = RET_V_DIM // 2
RET_QK_WIDTH = RET_HEADS * RET_QK_DIM
ROPE_BASE = 10000.0

LANES = 128
SUBLANES = 8

CHUNK = 128
MIX_BLOCK = 256
FFN_BLOCK = 1024
FFN_COL_CHUNK = 256
ROPE_BLOCK = 2048
VMEM_LIMIT = 56 * 1024 * 1024
DT_COPIES = 4
SLAB = 256
PREP_ROWS = 256

_Z0, _Z1 = 0, SSM_WIDTH
_X0, _X1 = _Z1, _Z1 + CONV_DIM
_Q0, _Q1 = _X1, _X1 + RET_QK_WIDTH
_K0, _K1 = _Q1, _Q1 + RET_QK_WIDTH
_V0, _V1 = _K1, _K1 + RET_WIDTH
_G0, _G1 = _V1, _V1 + RET_WIDTH


def _resident(shape):
    del shape
    return pl.BlockSpec(memory_space=pltpu.VMEM)


def _resident_layer(shape, layer):
    nd = len(shape)
    return pl.BlockSpec((None,) + tuple(shape), lambda *_: (layer,) + (0,) * nd,
                        pipeline_mode=pl.Buffered(1))


def _sigmoid(x):
    return 1.0 / (1.0 + jnp.exp(-x))


def _softplus(x):
    return jnp.maximum(x, 0.0) + jnp.log1p(jnp.exp(-jnp.abs(x)))


def _rms(x, w):
    ms = jnp.mean(x * x, axis=-1, keepdims=True)
    return x * lax.rsqrt(ms + NORM_EPS) * w


def _dot(a, b):
    return jnp.dot(a, b, preferred_element_type=F32)


def _dot_nt(a, b):
    return lax.dot_general(a, b, (((1,), (1,)), ((), ())), preferred_element_type=F32)


def _split3(v):
    p1 = v.astype(BF16)
    r1 = v - p1.astype(F32)
    p2 = r1.astype(BF16)
    p3 = (r1 - p2.astype(F32)).astype(BF16)
    return p1, p2, p3


def _cast_kernel(w_ref, o_ref):
    o_ref[...] = w_ref[...].astype(BF16)


def _mix_in_kernel(w_ref, o_ref):
    o_ref[:, 0:_X1] = w_ref[:, 0:_X1].astype(BF16)
    o_ref[:, _X1:_G1] = w_ref[:, _X1 + SSM_HEADS:_G1 + SSM_HEADS].astype(BF16)


def _prep_weights(w, body, out_cols):
    layers, k, n = w.shape
    return pl.pallas_call(
        body,
        grid=(layers, k // PREP_ROWS),
        in_specs=[pl.BlockSpec((None, PREP_ROWS, n), lambda l, i: (l, i, 0))],
        out_specs=pl.BlockSpec((None, PREP_ROWS, out_cols), lambda l, i: (l, i, 0)),
        out_shape=jax.ShapeDtypeStruct((layers, k, out_cols), BF16),
        compiler_params=pltpu.CompilerParams(
            dimension_semantics=("arbitrary", "arbitrary"), vmem_limit_bytes=VMEM_LIMIT),
        name="prep_weights",
    )(w)


def _rope_kernel(pos_ref, invf_ref, cos_ref, sin_ref):
    ang = invf_ref[...] * pos_ref[...].astype(F32)
    c = jnp.cos(ang)
    s = jnp.sin(ang)
    cos_ref[...] = jnp.concatenate([c, c], axis=0).T
    sin_ref[...] = jnp.concatenate([-s, s], axis=0).T


def _rope_tables(positions):
    t = positions.size
    tb = min(ROPE_BLOCK, t)
    assert t % tb == 0
    half = RET_QK_DIM // 2
    inv_freq = ROPE_BASE ** (-jnp.arange(half, dtype=F32) / half)
    return pl.pallas_call(
        _rope_kernel,
        grid=(t // tb,),
        in_specs=[pl.BlockSpec((None, 1, tb), lambda i: (i, 0, 0)),
                  pl.BlockSpec((half, 1), lambda i: (0, 0))],
        out_specs=[pl.BlockSpec((tb, LANES), lambda i: (i, 0)),
                   pl.BlockSpec((tb, LANES), lambda i: (i, 0))],
        out_shape=[jax.ShapeDtypeStruct((t, LANES), F32)] * 2,
        name="rope_tables",
    )(positions.reshape(t // tb, 1, tb), inv_freq[:, None])


def _ffn_kernel(x_ref, pre_ref, win_ref, wout_ref, post_ref, o_ref, xn_ref, act_ref):
    tm = x_ref.shape[0]
    halves = (slice(0, tm // 2), slice(tm // 2, tm))
    slabs = list(range(0, D_FF, FFN_COL_CHUNK))

    def prenorm(r):
        xn_ref[r, :] = _rms(x_ref[r, :], pre_ref[...]).astype(BF16)

    def swiglu(r, lo):
        gate = _dot(xn_ref[r, :], win_ref[:, lo:lo + FFN_COL_CHUNK])
        up = _dot(xn_ref[r, :], win_ref[:, D_FF + lo:D_FF + lo + FFN_COL_CHUNK])
        act_ref[r, lo:lo + FFN_COL_CHUNK] = (gate * _sigmoid(gate) * up).astype(BF16)

    def project(r):
        h = _dot(act_ref[r, :], wout_ref[...])
        o_ref[r, :] = x_ref[r, :] + 0.5 * _rms(h, post_ref[...])

    first, second = halves
    prenorm(first)
    swiglu(first, slabs[0])
    prenorm(second)
    for lo in slabs[1:]:
        swiglu(first, lo)
    swiglu(second, slabs[0])
    project(first)
    for lo in slabs[1:]:
        swiglu(second, lo)
    project(second)


def _ffn(x2d, layer, pre, w_in, w_out, post):
    t = x2d.shape[0]
    tm = min(FFN_BLOCK, t)
    return pl.pallas_call(
        _ffn_kernel,
        grid=(t // tm,),
        in_specs=[pl.BlockSpec((tm, D_MODEL), lambda i: (i, 0)),
                  _resident((1, D_MODEL)),
                  _resident_layer((D_MODEL, 2 * D_FF), layer),
                  _resident_layer((D_FF, D_MODEL), layer),
                  _resident((1, D_MODEL))],
        out_specs=pl.BlockSpec((tm, D_MODEL), lambda i: (i, 0)),
        out_shape=jax.ShapeDtypeStruct((t, D_MODEL), F32),
        scratch_shapes=[pltpu.VMEM((tm, D_MODEL), BF16),
                        pltpu.VMEM((tm, D_FF), BF16)],
        compiler_params=pltpu.CompilerParams(
            dimension_semantics=("arbitrary",), vmem_limit_bytes=VMEM_LIMIT),
        name="ffn",
    )(x2d, pre[layer][None, :], w_in, w_out, post[layer][None, :])


_Slot = collections.namedtuple(
    "_Slot", ["zs", "xs", "bc", "bT", "q", "qd", "k", "kdT", "v", "gs", "dt", "dtT"])


def _mixer_kernel(xp1_ref, xp2_ref, xr_ref, cos1_ref, sin1_ref, cos2_ref, sin2_ref, pre_ref, wmain_ref,
                  wdt_ref, wdtT_ref, convw_ref, convb_ref, dtb_row_ref, dtb_col_ref, alog_row_ref,
                  alog_col_ref, dskip_ref, ssmnorm_ref, retnorm_ref, wout_ref, post_ref, expand_ref,
                  tril_ref, triu_ref, dmat_ref, qdec_ref, kdec_ref, cdec_ref,
                  o_ref,
                  u_ref, xbc_ref, y_ref, cat_ref, mixed_ref, sstate_ref, rstate_ref, *slot_refs,
                  tb, nj):
    i = pl.program_id(0)
    n = len(_Slot._fields)
    slot_a = _Slot(*slot_refs[:n])
    slot_b = _Slot(*slot_refs[n:])

    @pl.when(i == 0)
    def _():
        for ref in slot_a:
            ref[...] = jnp.zeros(ref.shape, ref.dtype)
        xbc_ref[0:SUBLANES, :] = jnp.zeros((SUBLANES, CONV_DIM), F32)

    @pl.when(jnp.logical_or(i == 0, lax.rem(2 * i + nj - 2, nj) == 0))
    def _():
        sstate_ref[...] = jnp.zeros(sstate_ref.shape, F32)
        rstate_ref[...] = jnp.zeros(rstate_ref.shape, F32)

    refs = (pre_ref, wmain_ref, wdt_ref, wdtT_ref, convw_ref, convb_ref, dtb_row_ref, dtb_col_ref,
            alog_row_ref, alog_col_ref, dskip_ref, ssmnorm_ref, retnorm_ref, wout_ref, post_ref,
            expand_ref, tril_ref, triu_ref, dmat_ref, qdec_ref, kdec_ref, cdec_ref,
            u_ref, xbc_ref, y_ref, cat_ref, mixed_ref, sstate_ref, rstate_ref)
    first = pl.ds(0, tb)
    second = pl.ds(tb, tb)
    _mixer_step(xp1_ref, cos1_ref, sin1_ref, xr_ref.at[first], o_ref.at[first], *refs,
                slot_b, slot_a, None, tb=tb)
    _mixer_step(xp2_ref, cos2_ref, sin2_ref, xr_ref.at[second], o_ref.at[second], *refs,
                slot_a, slot_b, lax.rem(2 * i, nj) == 0, tb=tb)


def _mixer_step(xp_ref, cos_ref, sin_ref, xr_ref, o_ref, pre_ref, wmain_ref, wdt_ref, wdtT_ref,
                convw_ref, convb_ref, dtb_row_ref, dtb_col_ref, alog_row_ref, alog_col_ref, dskip_ref,
                ssmnorm_ref, retnorm_ref, wout_ref, post_ref, expand_ref, tril_ref, triu_ref,
                dmat_ref, qdec_ref, kdec_ref, cdec_ref,
                u_ref, xbc_ref, y_ref, cat_ref, mixed_ref, sstate_ref, rstate_ref, P, S, new_sequence,
                *, tb):
    L = CHUNK
    if new_sequence is not None:
        tail = xbc_ref[0:SUBLANES, :]
        xbc_ref[0:SUBLANES, :] = jnp.where(new_sequence, jnp.zeros_like(tail), tail)

    def proj(lo, hi):
        return _dot(u_ref[...], wmain_ref[:, lo:hi])

    def p_norm():
        u_ref[...] = _rms(xp_ref[...], pre_ref[...]).astype(BF16)

    def p_gate(dst, base, j):
        def piece():
            sl = slice(j * SLAB, (j + 1) * SLAB)
            t = proj(base + j * SLAB, base + (j + 1) * SLAB)
            dst[:, sl] = t * _sigmoid(t)
        return piece

    def p_value(j):
        def piece():
            sl = slice(j * SLAB, (j + 1) * SLAB)
            P.v[:, sl] = proj(_V0 + j * SLAB, _V0 + (j + 1) * SLAB).astype(BF16)
        return piece

    def p_rotary(j):
        def piece():
            q = proj(_Q0 + j * SLAB, _Q0 + (j + 1) * SLAB)
            k = proj(_K0 + j * SLAB, _K0 + (j + 1) * SLAB)
            cos = cos_ref[...]
            sin = sin_ref[...]
            for hh in range(SLAB // RET_QK_DIM):
                loc = slice(hh * RET_QK_DIM, (hh + 1) * RET_QK_DIM)
                sl = slice(j * SLAB + hh * RET_QK_DIM, j * SLAB + (hh + 1) * RET_QK_DIM)
                qh = q[:, loc]
                kh = k[:, loc]
                qr = qh * cos + pltpu.roll(qh, RET_QK_DIM // 2, 1) * sin
                kr = (kh * cos + pltpu.roll(kh, RET_QK_DIM // 2, 1) * sin) * (RET_QK_DIM ** -0.5)
                P.q[:, sl] = qr.astype(BF16)
                P.qd[:, sl] = (qr * qdec_ref[:, sl]).astype(BF16)
                P.k[:, sl] = kr.astype(BF16)
                P.kdT[sl, :] = (kr * kdec_ref[:, sl]).T.astype(BF16)
        return piece

    def p_dt():
        P.dt[...] = _softplus(_dot(u_ref[...], wdt_ref[...]) + dtb_row_ref[...])
        P.dtT[...] = _softplus(_dot_nt(wdtT_ref[...], u_ref[...]) + dtb_col_ref[...])

    def p_conv(j):
        def piece():
            sl = slice(j * SLAB, (j + 1) * SLAB)
            xbc_ref[SUBLANES:SUBLANES + tb, sl] = proj(_X0 + j * SLAB, _X0 + (j + 1) * SLAB)
            conv = convb_ref[:, sl]
            for kk in range(CONV_K):
                off = SUBLANES - (CONV_K - 1) + kk
                conv = conv + convw_ref[kk:kk + 1, sl] * xbc_ref[off:off + tb, sl]
            act = conv * _sigmoid(conv)
            xbc_ref[0:SUBLANES, sl] = xbc_ref[tb:tb + SUBLANES, sl]
            if (j + 1) * SLAB <= SSM_WIDTH:
                P.xs[:, sl] = act
            else:
                lo = j * SLAB - SSM_WIDTH
                P.bc[:, lo:lo + SLAB] = act.astype(BF16)
                if lo < BC_WIDTH:
                    P.bT[lo:lo + SLAB, :] = act.T.astype(BF16)
        return piece

    a_row = -jnp.exp(alog_row_ref[...])
    a_col = -jnp.exp(alog_col_ref[...])
    rows = lax.broadcasted_iota(jnp.int32, (L, L), 0)
    cols = lax.broadcasted_iota(jnp.int32, (L, L), 1)
    causal = rows >= cols
    lane = lax.broadcasted_iota(jnp.int32, (L, LANES), 1)
    low_half = lane < SSM_HEAD_DIM

    def s_chunk_pieces(c):
        r = slice(c * L, (c + 1) * L)
        ctx = {}

        def decays():
            dt_c = S.dt[r, :]
            dtT_c = S.dtT[:, r]
            parts = _dot(tril_ref[...], jnp.concatenate(_split3(dt_c * a_row), axis=1))
            acum = parts[:, 0:LANES] + parts[:, LANES:2 * LANES] + parts[:, 2 * LANES:3 * LANES]
            partsT = _dot(jnp.concatenate(_split3(dtT_c * a_col), axis=0), triu_ref[...])
            acumT = (partsT[0:SSM_HEADS] + partsT[SSM_HEADS:2 * SSM_HEADS]
                     + partsT[2 * SSM_HEADS:3 * SSM_HEADS])
            ctx["acum"] = acum
            ctx["lrow"] = acumT - jnp.log(dtT_c)
            e = jnp.exp(acum)
            w = jnp.exp(acum[L - 1:L, :] - acum) * dt_c
            e_hi = e.astype(BF16).astype(F32)
            w_hi = w.astype(BF16).astype(F32)
            packed = jnp.where(lane < SSM_HEADS, e_hi,
                               jnp.where(lane < 2 * SSM_HEADS, e - e_hi,
                                         jnp.where(lane < 3 * SSM_HEADS, w_hi,
                                                   jnp.where(lane < 4 * SSM_HEADS, w - w_hi, 0.0))))
            ex = _dot(packed.astype(BF16), expand_ref[...])
            ctx["ex_e"] = ex[:, 0:SSM_WIDTH]
            ctx["xw"] = (S.xs[r, :] * ex[:, SSM_WIDTH:2 * SSM_WIDTH]).astype(BF16)

        def group_open(gi):
            def piece():
                gsl = slice(gi * GROUP_WIDTH, (gi + 1) * GROUP_WIDTH)
                b_g = S.bc[r, gi * SSM_STATE:(gi + 1) * SSM_STATE]
                c_g = S.bc[r, BC_WIDTH + gi * SSM_STATE:BC_WIDTH + (gi + 1) * SSM_STATE]
                ctx["cb"] = _dot_nt(c_g, b_g)
                st = sstate_ref[gi]
                ctx["y_inter"] = _dot(c_g, st.astype(BF16)) * ctx["ex_e"][:, gsl]
                sstate_ref[gi] = (st * ctx["ex_e"][L - 1:L, gsl]
                                  + _dot(S.bT[gi * SSM_STATE:(gi + 1) * SSM_STATE, r], ctx["xw"][:, gsl]))
            return piece

        def head_pair(gi, p):
            def piece():
                h0 = gi * HEADS_PER_GROUP + 2 * p
                psl = slice(h0 * SSM_HEAD_DIM, (h0 + 2) * SSM_HEAD_DIM)
                gmats = []
                for h in (h0, h0 + 1):
                    diff = ctx["acum"][:, h:h + 1] - ctx["lrow"][h:h + 1, :]
                    gmats.append(ctx["cb"] * jnp.exp(jnp.where(causal, diff, -jnp.inf)))
                lhs = jnp.concatenate(gmats, axis=1).astype(BF16)
                xp = S.xs[r, psl]
                rhs = jnp.concatenate([jnp.where(low_half, xp, 0.0),
                                       jnp.where(low_half, 0.0, xp)], axis=0).astype(BF16)
                y = (_dot(lhs, rhs) + ctx["y_inter"][:, 2 * p * SSM_HEAD_DIM:(2 * p + 2) * SSM_HEAD_DIM]
                     + xp * dskip_ref[:, psl])
                y_ref[:, psl] = y * S.zs[r, psl]
            return piece

        def ssd_norm():
            cat_ref[r, 0:SSM_WIDTH] = _rms(y_ref[...], ssmnorm_ref[...]).astype(BF16)

        def retention(h):
            def piece():
                qsl = slice(h * RET_QK_DIM, (h + 1) * RET_QK_DIM)
                vsl = slice(h * RET_V_DIM, (h + 1) * RET_V_DIM)
                vh = S.v[r, vsl]
                sc = _dot_nt(S.q[r, qsl], S.k[r, qsl]) * dmat_ref[h]
                st = rstate_ref[h]
                lhs = jnp.concatenate([sc.astype(BF16), S.qd[r, qsl]], axis=1)
                rhs = jnp.concatenate([vh, st.astype(BF16)], axis=0)
                rh = _dot(lhs, rhs)
                rstate_ref[h] = st * cdec_ref[:, vsl] + _dot(S.kdT[qsl, r], vh)
                d = rh - jnp.mean(rh, axis=-1, keepdims=True)
                var = jnp.mean(d * d, axis=-1, keepdims=True)
                rn = d * lax.rsqrt(var + RET_NORM_EPS) * retnorm_ref[:, vsl]
                cat_ref[r, SSM_WIDTH + h * RET_V_DIM:SSM_WIDTH + (h + 1) * RET_V_DIM] = (
                    S.gs[r, vsl] * rn).astype(BF16)
            return piece

        ssd = []
        for gi in range(SSM_GROUPS):
            ssd.append(group_open(gi))
            ssd += [head_pair(gi, p) for p in range(HEADS_PER_GROUP // 2)]
        ret = [retention(h) for h in range(RET_HEADS)]
        pieces = [decays]
        while ssd or ret:
            pieces += ssd[:3]
            ssd = ssd[3:]
            pieces += ret[:1]
            ret = ret[1:]
        pieces.append(ssd_norm)
        return pieces

    def s_out(j):
        def piece():
            sl = slice(j * SLAB, (j + 1) * SLAB)
            mixed_ref[:, sl] = _dot(cat_ref[...], wout_ref[:, sl])
        return piece

    def s_residual():
        o_ref[...] = xr_ref[...] + _rms(mixed_ref[...], post_ref[...])

    p_light = ([p_gate(P.zs, _Z0, j) for j in range(SSM_WIDTH // SLAB)]
               + [p_gate(P.gs, _G0, j) for j in range(RET_WIDTH // SLAB)]
               + [p_rotary(j) for j in range(RET_QK_WIDTH // SLAB)] + [p_dt])
    s_scan = [piece for c in range(tb // L) for piece in s_chunk_pieces(c)]
    p_norm()
    n_p, n_s = len(p_light), len(s_scan)
    done = 0
    for i, piece in enumerate(p_light):
        piece()
        upto = (i + 1) * n_s // n_p
        for sp in s_scan[done:upto]:
            sp()
        done = upto
    convs = [p_conv(j) for j in range(CONV_DIM // SLAB)]
    outs = [s_out(j) for j in range(D_MODEL // SLAB)]
    values = [p_value(j) for j in range(RET_WIDTH // SLAB)]
    for out_piece, conv_piece in zip(outs, convs):
        out_piece()
        conv_piece()
    s_residual()
    rest = convs[len(outs):]
    for i, piece in enumerate(values):
        piece()
        if i < len(rest):
            rest[i]()


def _retention_tables(tb):
    log_gamma = jnp.log(1.0 - 2.0 ** (-5.0 - jnp.arange(RET_HEADS, dtype=F32)))
    pos = jnp.arange(CHUNK, dtype=F32)
    rel = pos[:, None] - pos[None, :]
    dmat = jnp.exp(jnp.where((rel >= 0)[None], rel[None] * log_gamma[:, None, None], -jnp.inf))
    q_decay = jnp.exp((pos[:, None] + 1.0) * log_gamma[None])
    k_decay = jnp.exp((CHUNK - 1.0 - pos[:, None]) * log_gamma[None])
    chunk_decay = jnp.exp(CHUNK * log_gamma)
    qdec = jnp.tile(jnp.repeat(q_decay, RET_QK_DIM, axis=1), (tb // CHUNK, 1))
    kdec = jnp.tile(jnp.repeat(k_decay, RET_QK_DIM, axis=1), (tb // CHUNK, 1))
    cdec = jnp.repeat(chunk_decay, RET_V_DIM)[None, :]
    return dmat, qdec, kdec, cdec


def _constant_matrices():
    e = np.zeros((LANES, 2 * SSM_WIDTH), np.float32)
    for g in range(DT_COPIES):
        for h in range(SSM_HEADS):
            c0 = (g // 2) * SSM_WIDTH + h * SSM_HEAD_DIM
            e[g * SSM_HEADS + h, c0:c0 + SSM_HEAD_DIM] = 1.0
    tril = np.tril(np.ones((CHUNK, CHUNK), np.float32))
    return jnp.asarray(e, BF16), jnp.asarray(tril, BF16), jnp.asarray(tril.T, BF16)


def _slot_scratch(tb):
    return list(_Slot(
        zs=pltpu.VMEM((tb, SSM_WIDTH), F32),
        xs=pltpu.VMEM((tb, SSM_WIDTH), F32),
        bc=pltpu.VMEM((tb, 2 * BC_WIDTH), BF16),
        bT=pltpu.VMEM((BC_WIDTH, tb), BF16),
        q=pltpu.VMEM((tb, RET_QK_WIDTH), BF16),
        qd=pltpu.VMEM((tb, RET_QK_WIDTH), BF16),
        k=pltpu.VMEM((tb, RET_QK_WIDTH), BF16),
        kdT=pltpu.VMEM((RET_QK_WIDTH, tb), BF16),
        v=pltpu.VMEM((tb, RET_WIDTH), BF16),
        gs=pltpu.VMEM((tb, RET_WIDTH), F32),
        dt=pltpu.VMEM((tb, LANES), F32),
        dtT=pltpu.VMEM((SSM_HEADS, tb), F32),
    ))


def _mixer(x2d, batch, seq, layer, cos, sin, pre, w_in, w_main, conv_w, conv_b, dt_bias, a_log, d_skip,
           ssm_norm, ret_norm, w_out, post):
    tb = min(MIX_BLOCK, seq)
    nj = seq // tb
    nblocks = batch * nj
    w_dt = w_in[:, _X1:_X1 + SSM_HEADS]
    lane_pad = (0, LANES - DT_COPIES * SSM_HEADS)
    w_dt_pad = jnp.pad(jnp.tile(w_dt, (1, DT_COPIES)), ((0, 0), lane_pad)).astype(BF16)
    w_dt_t = w_dt.T.astype(BF16)
    dtb_row = jnp.pad(jnp.tile(dt_bias, DT_COPIES), lane_pad)[None, :]
    alog_row = jnp.pad(jnp.tile(a_log, DT_COPIES), lane_pad)[None, :]
    dskip = jnp.repeat(d_skip, SSM_HEAD_DIM)[None, :]
    dmat, qdec, kdec, cdec = _retention_tables(tb)
    expand, tril, triu = _constant_matrices()

    assert nj % 2 == 0, "two blocks per grid step need an even number of blocks per sequence"
    odd = lambda i: (jnp.clip(2 * i - 1, 0, nblocks - 1), 0)
    even = lambda i: (jnp.minimum(2 * i, nblocks - 1), 0)
    prev = lambda i: (jnp.maximum(i - 1, 0), 0)
    operands = [
        (x2d, pl.BlockSpec((tb, D_MODEL), odd)),
        (x2d, pl.BlockSpec((tb, D_MODEL), even)),
        (x2d, pl.BlockSpec((2 * tb, D_MODEL), prev)),
        (cos, pl.BlockSpec((tb, LANES), odd)),
        (sin, pl.BlockSpec((tb, LANES), odd)),
        (cos, pl.BlockSpec((tb, LANES), even)),
        (sin, pl.BlockSpec((tb, LANES), even)),
        (pre[None, :], None),
        (w_main, _resident_layer(w_main.shape[1:], layer)),
        (w_dt_pad, None),
        (w_dt_t, None),
        (conv_w, None),
        (conv_b[None, :], None),
        (dtb_row, None),
        (dt_bias[:, None], None),
        (alog_row, None),
        (a_log[:, None], None),
        (dskip, None),
        (ssm_norm[None, :], None),
        (ret_norm[None, :], None),
        (w_out, _resident_layer(w_out.shape[1:], layer)),
        (post[None, :], None),
        (expand, None),
        (tril, None),
        (triu, None),
        (dmat, None),
        (qdec, None),
        (kdec, None),
        (cdec, None),
    ]
    args = [a for a, _ in operands]
    in_specs = [sp if sp is not None else _resident(a.shape) for a, sp in operands]
    return pl.pallas_call(
        functools.partial(_mixer_kernel, tb=tb, nj=nj),
        grid=(nblocks // 2 + 1,),
        in_specs=in_specs,
        out_specs=pl.BlockSpec((2 * tb, D_MODEL), prev),
        out_shape=jax.ShapeDtypeStruct(x2d.shape, F32),
        scratch_shapes=[
            pltpu.VMEM((tb, D_MODEL), BF16),
            pltpu.VMEM((SUBLANES + tb, CONV_DIM), F32),
            pltpu.VMEM((CHUNK, SSM_WIDTH), F32),
            pltpu.VMEM((tb, D_INNER), BF16),
            pltpu.VMEM((tb, D_MODEL), F32),
            pltpu.VMEM((SSM_GROUPS, SSM_STATE, GROUP_WIDTH), F32),
            pltpu.VMEM((RET_HEADS, RET_QK_DIM, RET_V_DIM), F32),
        ] + 2 * _slot_scratch(tb),
        compiler_params=pltpu.CompilerParams(
            dimension_semantics=("arbitrary",), vmem_limit_bytes=VMEM_LIMIT),
        name="mixer",
    )(*args)


def kernel(x, positions, ffn1_pre_norm, ffn1_w_in, ffn1_w_out, ffn1_post_norm, mix_pre_norm, mix_w_in, conv_w, conv_b, dt_bias, a_log, d_skip, ssm_norm, ret_norm, mix_w_out, mix_post_norm, ffn2_pre_norm, ffn2_w_in, ffn2_w_out, ffn2_post_norm):
    batch, seq, _ = x.shape
    depth = ffn1_w_in.shape[0]
    cos, sin = _rope_tables(positions)
    ffn1_in = _prep_weights(ffn1_w_in, _cast_kernel, 2 * D_FF)
    ffn1_out = _prep_weights(ffn1_w_out, _cast_kernel, D_MODEL)
    ffn2_in = _prep_weights(ffn2_w_in, _cast_kernel, 2 * D_FF)
    ffn2_out = _prep_weights(ffn2_w_out, _cast_kernel, D_MODEL)
    mix_in = _prep_weights(mix_w_in, _mix_in_kernel, _G1)
    mix_out = _prep_weights(mix_w_out, _cast_kernel, D_MODEL)
    h = x.reshape(batch * seq, D_MODEL)
    for l in range(depth):
        h = _ffn(h, l, ffn1_pre_norm, ffn1_in, ffn1_out, ffn1_post_norm)
        h = _mixer(h, batch, seq, l, cos, sin, mix_pre_norm[l], mix_w_in[l], mix_in, conv_w[l], conv_b[l],
                   dt_bias[l], a_log[l], d_skip[l], ssm_norm[l], ret_norm[l], mix_out, mix_post_norm[l])
        h = _ffn(h, l, ffn2_pre_norm, ffn2_in, ffn2_out, ffn2_post_norm)
    return h.reshape(batch, seq, D_MODEL)
```

```python
import collections
import functools

import jax
import jax.numpy as jnp
import numpy as np
from jax import lax
from jax.experimental import pallas as pl
from jax.experimental.pallas import tpu as pltpu

F32 = jnp.float32
BF16 = jnp.bfloat16

D_MODEL = 1024
D_FF = 2816
NORM_EPS = 1e-6
RET_NORM_EPS = 1e-5

D_INNER = 2 * D_MODEL
SSM_WIDTH = D_INNER // 2
RET_WIDTH = D_INNER - SSM_WIDTH
SSM_HEAD_DIM = 64
SSM_HEADS = SSM_WIDTH // SSM_HEAD_DIM
SSM_GROUPS = 2
HEADS_PER_GROUP = SSM_HEADS // SSM_GROUPS
GROUP_WIDTH = HEADS_PER_GROUP * SSM_HEAD_DIM
SSM_STATE = 128
BC_WIDTH = SSM_GROUPS * SSM_STATE
CONV_K = 4
CONV_DIM = SSM_WIDTH + 2 * BC_WIDTH
RET_HEADS = 4
RET_V_DIM = RET_WIDTH // RET_HEADS
RET_QK_DIM = RET_V_DIM // 2
RET_QK_WIDTH = RET_HEADS * RET_QK_DIM
ROPE_BASE = 10000.0

LANES = 128
SUBLANES = 8

CHUNK = 128
MIX_BLOCK = 256
MIX_GROUP = 2
N_MIX_CONSTS = 22
FFN_BLOCK = 1024
FFN_COL_CHUNK = 256
ROPE_BLOCK = 2048
VMEM_LIMIT = 58 * 1024 * 1024
DT_COPIES = 4
SLAB = 256
PREP_ROWS = 256

_Z0, _Z1 = 0, SSM_WIDTH
_X0, _X1 = _Z1, _Z1 + CONV_DIM
_Q0, _Q1 = _X1, _X1 + RET_QK_WIDTH
_K0, _K1 = _Q1, _Q1 + RET_QK_WIDTH
_V0, _V1 = _K1, _K1 + RET_WIDTH
_G0, _G1 = _V1, _V1 + RET_WIDTH


def _resident(shape):
    del shape
    return pl.BlockSpec(memory_space=pltpu.VMEM)


def _resident_layer(shape, layer):
    nd = len(shape)
    return pl.BlockSpec((None,) + tuple(shape), lambda *_: (layer,) + (0,) * nd,
                        pipeline_mode=pl.Buffered(1))


def _sigmoid(x):
    return 1.0 / (1.0 + jnp.exp(-x))


def _softplus(x):
    return jnp.maximum(x, 0.0) + jnp.log1p(jnp.exp(-jnp.abs(x)))


def _rms(x, w):
    ms = jnp.mean(x * x, axis=-1, keepdims=True)
    return x * lax.rsqrt(ms + NORM_EPS) * w


def _dot(a, b):
    return jnp.dot(a, b, preferred_element_type=F32)


def _dot_nt(a, b):
    return lax.dot_general(a, b, (((1,), (1,)), ((), ())), preferred_element_type=F32)


def _split3(v):
    p1 = v.astype(BF16)
    r1 = v - p1.astype(F32)
    p2 = r1.astype(BF16)
    p3 = (r1 - p2.astype(F32)).astype(BF16)
    return p1, p2, p3


def _cast_kernel(w_ref, o_ref):
    o_ref[...] = w_ref[...].astype(BF16)


def _mix_in_kernel(w_ref, o_ref):
    o_ref[:, 0:_X1] = w_ref[:, 0:_X1].astype(BF16)
    o_ref[:, _X1:_G1] = w_ref[:, _X1 + SSM_HEADS:_G1 + SSM_HEADS].astype(BF16)


def _prep_weights(w, body, out_cols):
    layers, k, n = w.shape
    return pl.pallas_call(
        body,
        grid=(layers, k // PREP_ROWS),
        in_specs=[pl.BlockSpec((None, PREP_ROWS, n), lambda l, i: (l, i, 0))],
        out_specs=pl.BlockSpec((None, PREP_ROWS, out_cols), lambda l, i: (l, i, 0)),
        out_shape=jax.ShapeDtypeStruct((layers, k, out_cols), BF16),
        compiler_params=pltpu.CompilerParams(
            dimension_semantics=("arbitrary", "arbitrary"), vmem_limit_bytes=VMEM_LIMIT),
        name="prep_weights",
    )(w)


def _rope_kernel(pos_ref, invf_ref, cos_ref, sin_ref):
    ang = invf_ref[...] * pos_ref[...].astype(F32)
    c = jnp.cos(ang)
    s = jnp.sin(ang)
    cos_ref[...] = jnp.concatenate([c, c], axis=0).T
    sin_ref[...] = jnp.concatenate([-s, s], axis=0).T


def _rope_tables(positions):
    t = positions.size
    tb = min(ROPE_BLOCK, t)
    assert t % tb == 0
    half = RET_QK_DIM // 2
    inv_freq = ROPE_BASE ** (-jnp.arange(half, dtype=F32) / half)
    return pl.pallas_call(
        _rope_kernel,
        grid=(t // tb,),
        in_specs=[pl.BlockSpec((None, 1, tb), lambda i: (i, 0, 0)),
                  pl.BlockSpec((half, 1), lambda i: (0, 0))],
        out_specs=[pl.BlockSpec((tb, LANES), lambda i: (i, 0)),
                   pl.BlockSpec((tb, LANES), lambda i: (i, 0))],
        out_shape=[jax.ShapeDtypeStruct((t, LANES), F32)] * 2,
        name="rope_tables",
    )(positions.reshape(t // tb, 1, tb), inv_freq[:, None])


def _ffn_kernel(x_ref, pre_ref, win_ref, wout_ref, post_ref, o_ref, xn_ref, act_ref):
    tm = x_ref.shape[0]
    halves = (slice(0, tm // 2), slice(tm // 2, tm))
    slabs = list(range(0, D_FF, FFN_COL_CHUNK))

    def prenorm(r):
        xn_ref[r, :] = _rms(x_ref[r, :], pre_ref[...]).astype(BF16)

    def swiglu(r, lo):
        gate = _dot(xn_ref[r, :], win_ref[:, lo:lo + FFN_COL_CHUNK])
        up = _dot(xn_ref[r, :], win_ref[:, D_FF + lo:D_FF + lo + FFN_COL_CHUNK])
        act_ref[r, lo:lo + FFN_COL_CHUNK] = (gate * _sigmoid(gate) * up).astype(BF16)

    def project(r):
        h = _dot(act_ref[r, :], wout_ref[...])
        o_ref[r, :] = x_ref[r, :] + 0.5 * _rms(h, post_ref[...])

    first, second = halves
    prenorm(first)
    swiglu(first, slabs[0])
    prenorm(second)
    for lo in slabs[1:]:
        swiglu(first, lo)
    swiglu(second, slabs[0])
    project(first)
    for lo in slabs[1:]:
        swiglu(second, lo)
    project(second)


def _ffn(x2d, layer, pre, w_in, w_out, post):
    t = x2d.shape[0]
    tm = min(FFN_BLOCK, t)
    return pl.pallas_call(
        _ffn_kernel,
        grid=(t // tm,),
        in_specs=[pl.BlockSpec((tm, D_MODEL), lambda i: (i, 0)),
                  _resident((1, D_MODEL)),
                  _resident_layer((D_MODEL, 2 * D_FF), layer),
                  _resident_layer((D_FF, D_MODEL), layer),
                  _resident((1, D_MODEL))],
        out_specs=pl.BlockSpec((tm, D_MODEL), lambda i: (i, 0)),
        out_shape=jax.ShapeDtypeStruct((t, D_MODEL), F32),
        scratch_shapes=[pltpu.VMEM((tm, D_MODEL), BF16),
                        pltpu.VMEM((tm, D_FF), BF16)],
        compiler_params=pltpu.CompilerParams(
            dimension_semantics=("arbitrary",), vmem_limit_bytes=VMEM_LIMIT),
        name="ffn",
    )(x2d, pre[layer][None, :], w_in, w_out, post[layer][None, :])


_Slot = collections.namedtuple(
    "_Slot", ["x", "zs", "xs", "bc", "bT", "q", "qd", "k", "kdT", "v", "gs", "dt", "dtT"])


def _mixer_kernel(*refs, tb, nj):
    g = MIX_GROUP
    x_refs = refs[0:g]
    cos_refs = refs[g:2 * g]
    sin_refs = refs[2 * g:3 * g]
    consts = refs[3 * g:3 * g + N_MIX_CONSTS]
    o_ref = refs[3 * g + N_MIX_CONSTS]
    scratch = refs[3 * g + N_MIX_CONSTS + 1:]
    u_ref, xbc_ref, y_ref, cat_ref, mixed_ref, sstate_ref, rstate_ref = scratch[:7]
    n = len(_Slot._fields)
    slot_a = _Slot(*scratch[7:7 + n])
    slot_b = _Slot(*scratch[7 + n:7 + 2 * n])
    i = pl.program_id(0)

    @pl.when(i == 0)
    def _():
        for ref in slot_a:
            ref[...] = jnp.zeros(ref.shape, ref.dtype)
        xbc_ref[0:SUBLANES, :] = jnp.zeros((SUBLANES, CONV_DIM), F32)

    @pl.when(jnp.logical_or(i == 0, lax.rem(g * i + nj - g, nj) == 0))
    def _():
        sstate_ref[...] = jnp.zeros(sstate_ref.shape, F32)
        rstate_ref[...] = jnp.zeros(rstate_ref.shape, F32)

    shared = consts + (u_ref, xbc_ref, y_ref, cat_ref, mixed_ref, sstate_ref, rstate_ref)
    for k in range(g):
        p_slot, s_slot = (slot_b, slot_a) if k % 2 == 0 else (slot_a, slot_b)
        new_sequence = (lax.rem(g * i, nj) == 0) if k == g - 1 else None
        _mixer_step(x_refs[k], cos_refs[k], sin_refs[k], o_ref.at[pl.ds(k * tb, tb)], *shared,
                    p_slot, s_slot, new_sequence, tb=tb)


def _mixer_step(xp_ref, cos_ref, sin_ref, o_ref, pre_ref, wmain_ref, wdt_ref, wdtT_ref,
                convw_ref, convb_ref, dtb_row_ref, dtb_col_ref, alog_row_ref, alog_col_ref, dskip_ref,
                ssmnorm_ref, retnorm_ref, wout_ref, post_ref, expand_ref, tril_ref, triu_ref,
                dmat_ref, qdec_ref, kdec_ref, cdec_ref,
                u_ref, xbc_ref, y_ref, cat_ref, mixed_ref, sstate_ref, rstate_ref, P, S, new_sequence,
                *, tb):
    L = CHUNK
    if new_sequence is not None:
        tail = xbc_ref[0:SUBLANES, :]
        xbc_ref[0:SUBLANES, :] = jnp.where(new_sequence, jnp.zeros_like(tail), tail)

    def proj(lo, hi):
        return _dot(u_ref[...], wmain_ref[:, lo:hi])

    def p_norm():
        x = xp_ref[...]
        P.x[...] = x
        u_ref[...] = _rms(x, pre_ref[...]).astype(BF16)

    def p_gate(dst, base, j):
        def piece():
            sl = slice(j * SLAB, (j + 1) * SLAB)
            t = proj(base + j * SLAB, base + (j + 1) * SLAB)
            dst[:, sl] = t * _sigmoid(t)
        return piece

    def p_value(j):
        def piece():
            sl = slice(j * SLAB, (j + 1) * SLAB)
            P.v[:, sl] = proj(_V0 + j * SLAB, _V0 + (j + 1) * SLAB).astype(BF16)
        return piece

    def p_rotary(j):
        def piece():
            q = proj(_Q0 + j * SLAB, _Q0 + (j + 1) * SLAB)
            k = proj(_K0 + j * SLAB, _K0 + (j + 1) * SLAB)
            cos = cos_ref[...]
            sin = sin_ref[...]
            for hh in range(SLAB // RET_QK_DIM):
                loc = slice(hh * RET_QK_DIM, (hh + 1) * RET_QK_DIM)
                sl = slice(j * SLAB + hh * RET_QK_DIM, j * SLAB + (hh + 1) * RET_QK_DIM)
                qh = q[:, loc]
                kh = k[:, loc]
                qr = qh * cos + pltpu.roll(qh, RET_QK_DIM // 2, 1) * sin
                kr = (kh * cos + pltpu.roll(kh, RET_QK_DIM // 2, 1) * sin) * (RET_QK_DIM ** -0.5)
                P.q[:, sl] = qr.astype(BF16)
                P.qd[:, sl] = (qr * qdec_ref[:, sl]).astype(BF16)
                P.k[:, sl] = kr.astype(BF16)
                P.kdT[sl, :] = (kr * kdec_ref[:, sl]).T.astype(BF16)
        return piece

    def p_dt():
        P.dt[...] = _softplus(_dot(u_ref[...], wdt_ref[...]) + dtb_row_ref[...])
        P.dtT[...] = _softplus(_dot_nt(wdtT_ref[...], u_ref[...]) + dtb_col_ref[...])

    def p_conv(j):
        def piece():
            sl = slice(j * SLAB, (j + 1) * SLAB)
            xbc_ref[SUBLANES:SUBLANES + tb, sl] = proj(_X0 + j * SLAB, _X0 + (j + 1) * SLAB)
            conv = convb_ref[:, sl]
            for kk in range(CONV_K):
                off = SUBLANES - (CONV_K - 1) + kk
                conv = conv + convw_ref[kk:kk + 1, sl] * xbc_ref[off:off + tb, sl]
            act = conv * _sigmoid(conv)
            xbc_ref[0:SUBLANES, sl] = xbc_ref[tb:tb + SUBLANES, sl]
            if (j + 1) * SLAB <= SSM_WIDTH:
                P.xs[:, sl] = act
            else:
                lo = j * SLAB - SSM_WIDTH
                P.bc[:, lo:lo + SLAB] = act.astype(BF16)
                if lo < BC_WIDTH:
                    P.bT[lo:lo + SLAB, :] = act.T.astype(BF16)
        return piece

    a_row = -jnp.exp(alog_row_ref[...])
    a_col = -jnp.exp(alog_col_ref[...])
    rows = lax.broadcasted_iota(jnp.int32, (L, L), 0)
    cols = lax.broadcasted_iota(jnp.int32, (L, L), 1)
    causal = rows >= cols
    lane = lax.broadcasted_iota(jnp.int32, (L, LANES), 1)
    low_half = lane < SSM_HEAD_DIM

    def s_chunk_pieces(c):
        r = slice(c * L, (c + 1) * L)
        ctx = {}

        def decays():
            dt_c = S.dt[r, :]
            dtT_c = S.dtT[:, r]
            parts = _dot(tril_ref[...], jnp.concatenate(_split3(dt_c * a_row), axis=1))
            acum = parts[:, 0:LANES] + parts[:, LANES:2 * LANES] + parts[:, 2 * LANES:3 * LANES]
            partsT = _dot(jnp.concatenate(_split3(dtT_c * a_col), axis=0), triu_ref[...])
            acumT = (partsT[0:SSM_HEADS] + partsT[SSM_HEADS:2 * SSM_HEADS]
                     + partsT[2 * SSM_HEADS:3 * SSM_HEADS])
            ctx["acum"] = acum
            ctx["lrow"] = acumT - jnp.log(dtT_c)
            e = jnp.exp(acum)
            w = jnp.exp(acum[L - 1:L, :] - acum) * dt_c
            e_hi = e.astype(BF16).astype(F32)
            w_hi = w.astype(BF16).astype(F32)
            packed = jnp.where(lane < SSM_HEADS, e_hi,
                               jnp.where(lane < 2 * SSM_HEADS, e - e_hi,
                                         jnp.where(lane < 3 * SSM_HEADS, w_hi,
                                                   jnp.where(lane < 4 * SSM_HEADS, w - w_hi, 0.0))))
            ex = _dot(packed.astype(BF16), expand_ref[...])
            ctx["ex_e"] = ex[:, 0:SSM_WIDTH]
            ctx["xw"] = (S.xs[r, :] * ex[:, SSM_WIDTH:2 * SSM_WIDTH]).astype(BF16)

        def group_open(gi):
            def piece():
                gsl = slice(gi * GROUP_WIDTH, (gi + 1) * GROUP_WIDTH)
                b_g = S.bc[r, gi * SSM_STATE:(gi + 1) * SSM_STATE]
                c_g = S.bc[r, BC_WIDTH + gi * SSM_STATE:BC_WIDTH + (gi + 1) * SSM_STATE]
                ctx["cb"] = _dot_nt(c_g, b_g)
                st = sstate_ref[gi]
                ctx["y_inter"] = _dot(c_g, st.astype(BF16)) * ctx["ex_e"][:, gsl]
                sstate_ref[gi] = (st * ctx["ex_e"][L - 1:L, gsl]
                                  + _dot(S.bT[gi * SSM_STATE:(gi + 1) * SSM_STATE, r], ctx["xw"][:, gsl]))
            return piece

        def head_pair(gi, p):
            def piece():
                h0 = gi * HEADS_PER_GROUP + 2 * p
                psl = slice(h0 * SSM_HEAD_DIM, (h0 + 2) * SSM_HEAD_DIM)
                gmats = []
                for h in (h0, h0 + 1):
                    diff = ctx["acum"][:, h:h + 1] - ctx["lrow"][h:h + 1, :]
                    gmats.append(ctx["cb"] * jnp.exp(jnp.where(causal, diff, -jnp.inf)))
                lhs = jnp.concatenate(gmats, axis=1).astype(BF16)
                xp = S.xs[r, psl]
                rhs = jnp.concatenate([jnp.where(low_half, xp, 0.0),
                                       jnp.where(low_half, 0.0, xp)], axis=0).astype(BF16)
                y = (_dot(lhs, rhs) + ctx["y_inter"][:, 2 * p * SSM_HEAD_DIM:(2 * p + 2) * SSM_HEAD_DIM]
                     + xp * dskip_ref[:, psl])
                y_ref[:, psl] = y * S.zs[r, psl]
            return piece

        def ssd_norm():
            cat_ref[r, 0:SSM_WIDTH] = _rms(y_ref[...], ssmnorm_ref[...]).astype(BF16)

        def retention(h):
            def piece():
                qsl = slice(h * RET_QK_DIM, (h + 1) * RET_QK_DIM)
                vsl = slice(h * RET_V_DIM, (h + 1) * RET_V_DIM)
                vh = S.v[r, vsl]
                sc = _dot_nt(S.q[r, qsl], S.k[r, qsl]) * dmat_ref[h]
                st = rstate_ref[h]
                lhs = jnp.concatenate([sc.astype(BF16), S.qd[r, qsl]], axis=1)
                rhs = jnp.concatenate([vh, st.astype(BF16)], axis=0)
                rh = _dot(lhs, rhs)
                rstate_ref[h] = st * cdec_ref[:, vsl] + _dot(S.kdT[qsl, r], vh)
                d = rh - jnp.mean(rh, axis=-1, keepdims=True)
                var = jnp.mean(d * d, axis=-1, keepdims=True)
                rn = d * lax.rsqrt(var + RET_NORM_EPS) * retnorm_ref[:, vsl]
                cat_ref[r, SSM_WIDTH + h * RET_V_DIM:SSM_WIDTH + (h + 1) * RET_V_DIM] = (
                    S.gs[r, vsl] * rn).astype(BF16)
            return piece

        pieces = [decays]
        for gi in range(SSM_GROUPS):
            pieces.append(group_open(gi))
            pieces += [head_pair(gi, p) for p in range(HEADS_PER_GROUP // 2)]
        pieces.append(ssd_norm)
        pieces += [retention(h) for h in range(RET_HEADS)]
        return pieces

    def s_out(j):
        def piece():
            sl = slice(j * SLAB, (j + 1) * SLAB)
            mixed_ref[:, sl] = _dot(cat_ref[...], wout_ref[:, sl])
        return piece

    def s_residual():
        o_ref[...] = S.x[...] + _rms(mixed_ref[...], post_ref[...])

    p_light = ([p_gate(P.zs, _Z0, j) for j in range(SSM_WIDTH // SLAB)]
               + [p_gate(P.gs, _G0, j) for j in range(RET_WIDTH // SLAB)]
               + [p_rotary(j) for j in range(RET_QK_WIDTH // SLAB)] + [p_dt])
    s_scan = [piece for c in range(tb // L) for piece in s_chunk_pieces(c)]
    p_norm()
    n_p, n_s = len(p_light), len(s_scan)
    done = 0
    for i, piece in enumerate(p_light):
        piece()
        upto = (i + 1) * n_s // n_p
        for sp in s_scan[done:upto]:
            sp()
        done = upto
    convs = [p_conv(j) for j in range(CONV_DIM // SLAB)]
    outs = [s_out(j) for j in range(D_MODEL // SLAB)]
    values = [p_value(j) for j in range(RET_WIDTH // SLAB)]
    for out_piece, conv_piece in zip(outs, convs):
        out_piece()
        conv_piece()
    s_residual()
    rest = convs[len(outs):]
    for i, piece in enumerate(values):
        piece()
        if i < len(rest):
            rest[i]()


def _retention_tables(tb):
    log_gamma = jnp.log(1.0 - 2.0 ** (-5.0 - jnp.arange(RET_HEADS, dtype=F32)))
    pos = jnp.arange(CHUNK, dtype=F32)
    rel = pos[:, None] - pos[None, :]
    dmat = jnp.exp(jnp.where((rel >= 0)[None], rel[None] * log_gamma[:, None, None], -jnp.inf))
    q_decay = jnp.exp((pos[:, None] + 1.0) * log_gamma[None])
    k_decay = jnp.exp((CHUNK - 1.0 - pos[:, None]) * log_gamma[None])
    chunk_decay = jnp.exp(CHUNK * log_gamma)
    qdec = jnp.tile(jnp.repeat(q_decay, RET_QK_DIM, axis=1), (tb // CHUNK, 1))
    kdec = jnp.tile(jnp.repeat(k_decay, RET_QK_DIM, axis=1), (tb // CHUNK, 1))
    cdec = jnp.repeat(chunk_decay, RET_V_DIM)[None, :]
    return dmat, qdec, kdec, cdec


def _constant_matrices():
    e = np.zeros((LANES, 2 * SSM_WIDTH), np.float32)
    for g in range(DT_COPIES):
        for h in range(SSM_HEADS):
            c0 = (g // 2) * SSM_WIDTH + h * SSM_HEAD_DIM
            e[g * SSM_HEADS + h, c0:c0 + SSM_HEAD_DIM] = 1.0
    tril = np.tril(np.ones((CHUNK, CHUNK), np.float32))
    return jnp.asarray(e, BF16), jnp.asarray(tril, BF16), jnp.asarray(tril.T, BF16)


def _slot_scratch(tb):
    return list(_Slot(
        x=pltpu.VMEM((tb, D_MODEL), F32),
        zs=pltpu.VMEM((tb, SSM_WIDTH), F32),
        xs=pltpu.VMEM((tb, SSM_WIDTH), F32),
        bc=pltpu.VMEM((tb, 2 * BC_WIDTH), BF16),
        bT=pltpu.VMEM((BC_WIDTH, tb), BF16),
        q=pltpu.VMEM((tb, RET_QK_WIDTH), BF16),
        qd=pltpu.VMEM((tb, RET_QK_WIDTH), BF16),
        k=pltpu.VMEM((tb, RET_QK_WIDTH), BF16),
        kdT=pltpu.VMEM((RET_QK_WIDTH, tb), BF16),
        v=pltpu.VMEM((tb, RET_WIDTH), BF16),
        gs=pltpu.VMEM((tb, RET_WIDTH), F32),
        dt=pltpu.VMEM((tb, LANES), F32),
        dtT=pltpu.VMEM((SSM_HEADS, tb), F32),
    ))


def _mixer(x2d, batch, seq, layer, cos, sin, pre, w_in, w_main, conv_w, conv_b, dt_bias, a_log, d_skip,
           ssm_norm, ret_norm, w_out, post):
    tb = min(MIX_BLOCK, seq)
    nj = seq // tb
    nblocks = batch * nj
    w_dt = w_in[:, _X1:_X1 + SSM_HEADS]
    lane_pad = (0, LANES - DT_COPIES * SSM_HEADS)
    w_dt_pad = jnp.pad(jnp.tile(w_dt, (1, DT_COPIES)), ((0, 0), lane_pad)).astype(BF16)
    w_dt_t = w_dt.T.astype(BF16)
    dtb_row = jnp.pad(jnp.tile(dt_bias, DT_COPIES), lane_pad)[None, :]
    alog_row = jnp.pad(jnp.tile(a_log, DT_COPIES), lane_pad)[None, :]
    dskip = jnp.repeat(d_skip, SSM_HEAD_DIM)[None, :]
    dmat, qdec, kdec, cdec = _retention_tables(tb)
    expand, tril, triu = _constant_matrices()

    g = MIX_GROUP
    assert nj % g == 0, "a grid step's blocks must not straddle two sequences"

    def block(k):
        return lambda i: (jnp.clip(g * i - g + 1 + k, 0, nblocks - 1), 0)

    prev = lambda i: (jnp.maximum(i - 1, 0), 0)
    operands = (
        [(x2d, pl.BlockSpec((tb, D_MODEL), block(k))) for k in range(g)]
        + [(cos, pl.BlockSpec((tb, LANES), block(k))) for k in range(g)]
        + [(sin, pl.BlockSpec((tb, LANES), block(k))) for k in range(g)])
    consts = [
        (pre[None, :], None),
        (w_main, _resident_layer(w_main.shape[1:], layer)),
        (w_dt_pad, None),
        (w_dt_t, None),
        (conv_w, None),
        (conv_b[None, :], None),
        (dtb_row, None),
        (dt_bias[:, None], None),
        (alog_row, None),
        (a_log[:, None], None),
        (dskip, None),
        (ssm_norm[None, :], None),
        (ret_norm[None, :], None),
        (w_out, _resident_layer(w_out.shape[1:], layer)),
        (post[None, :], None),
        (expand, None),
        (tril, None),
        (triu, None),
        (dmat, None),
        (qdec, None),
        (kdec, None),
        (cdec, None),
    ]
    assert len(consts) == N_MIX_CONSTS
    operands += consts
    args = [a for a, _ in operands]
    in_specs = [sp if sp is not None else _resident(a.shape) for a, sp in operands]
    return pl.pallas_call(
        functools.partial(_mixer_kernel, tb=tb, nj=nj),
        grid=(nblocks // g + 1,),
        in_specs=in_specs,
        out_specs=pl.BlockSpec((g * tb, D_MODEL), prev),
        out_shape=jax.ShapeDtypeStruct(x2d.shape, F32),
        scratch_shapes=[
            pltpu.VMEM((tb, D_MODEL), BF16),
            pltpu.VMEM((SUBLANES + tb, CONV_DIM), F32),
            pltpu.VMEM((CHUNK, SSM_WIDTH), F32),
            pltpu.VMEM((tb, D_INNER), BF16),
            pltpu.VMEM((tb, D_MODEL), F32),
            pltpu.VMEM((SSM_GROUPS, SSM_STATE, GROUP_WIDTH), F32),
            pltpu.VMEM((RET_HEADS, RET_QK_DIM, RET_V_DIM), F32),
        ] + 2 * _slot_scratch(tb),
        compiler_params=pltpu.CompilerParams(
            dimension_semantics=("arbitrary",), vmem_limit_bytes=VMEM_LIMIT),
        name="mixer",
    )(*args)


def kernel(x, positions, ffn1_pre_norm, ffn1_w_in, ffn1_w_out, ffn1_post_norm, mix_pre_norm, mix_w_in, conv_w, conv_b, dt_bias, a_log, d_skip, ssm_norm, ret_norm, mix_w_out, mix_post_norm, ffn2_pre_norm, ffn2_w_in, ffn2_w_out, ffn2_post_norm):
    batch, seq, _ = x.shape
    depth = ffn1_w_in.shape[0]
    cos, sin = _rope_tables(positions)
    ffn1_in = _prep_weights(ffn1_w_in, _cast_kernel, 2 * D_FF)
    ffn1_out = _prep_weights(ffn1_w_out, _cast_kernel, D_MODEL)
    ffn2_in = _prep_weights(ffn2_w_in, _cast_kernel, 2 * D_FF)
    ffn2_out = _prep_weights(ffn2_w_out, _cast_kernel, D_MODEL)
    mix_in = _prep_weights(mix_w_in, _mix_in_kernel, _G1)
    mix_out = _prep_weights(mix_w_out, _cast_kernel, D_MODEL)
    h = x.reshape(batch * seq, D_MODEL)
    for l in range(depth):
        h = _ffn(h, l, ffn1_pre_norm, ffn1_in, ffn1_out, ffn1_post_norm)
        h = _mixer(h, batch, seq, l, cos, sin, mix_pre_norm[l], mix_w_in[l], mix_in, conv_w[l], conv_b[l],
                   dt_bias[l], a_log[l], d_skip[l], ssm_norm[l], ret_norm[l], mix_out, mix_post_norm[l])
        h = _ffn(h, l, ffn2_pre_norm, ffn2_in, ffn2_out, ffn2_post_norm)
    return h.reshape(batch, seq, D_MODEL)
```

```python
import collections
import functools

import jax
import jax.numpy as jnp
import numpy as np
from jax import lax
from jax.experimental import pallas as pl
from jax.experimental.pallas import tpu as pltpu

F32 = jnp.float32
BF16 = jnp.bfloat16

D_MODEL = 1024
D_FF = 2816
NORM_EPS = 1e-6
RET_NORM_EPS = 1e-5

D_INNER = 2 * D_MODEL
SSM_WIDTH = D_INNER // 2
RET_WIDTH = D_INNER - SSM_WIDTH
SSM_HEAD_DIM = 64
SSM_HEADS = SSM_WIDTH // SSM_HEAD_DIM
SSM_GROUPS = 2
HEADS_PER_GROUP = SSM_HEADS // SSM_GROUPS
GROUP_WIDTH = HEADS_PER_GROUP * SSM_HEAD_DIM
SSM_STATE = 128
BC_WIDTH = SSM_GROUPS * SSM_STATE
CONV_K = 4
CONV_DIM = SSM_WIDTH + 2 * BC_WIDTH
RET_HEADS = 4
RET_V_DIM = RET_WIDTH // RET_HEADS
RET_QK_DIM = RET_V_DIM // 2
RET_QK_WIDTH = RET_HEADS * RET_QK_DIM
ROPE_BASE = 10000.0

LANES = 128
SUBLANES = 8
BF16_ROWS = 16

CHUNK = 128
MIX_BLOCK = 256
MIX_GROUP = 2
N_MIX_CONSTS = 22
FFN_BLOCK = 1024
FFN_COL_CHUNK = 256
ROPE_BLOCK = 2048
VMEM_LIMIT = 58 * 1024 * 1024
DT_COPIES = 4
SLAB = 256
PREP_ROWS = 256

_Z0, _Z1 = 0, SSM_WIDTH
_X0, _X1 = _Z1, _Z1 + CONV_DIM
_Q0, _Q1 = _X1, _X1 + RET_QK_WIDTH
_K0, _K1 = _Q1, _Q1 + RET_QK_WIDTH
_V0, _V1 = _K1, _K1 + RET_WIDTH
_G0, _G1 = _V1, _V1 + RET_WIDTH


def _resident(shape):
    del shape
    return pl.BlockSpec(memory_space=pltpu.VMEM)


def _resident_block(shape):
    nd = len(shape)
    return pl.BlockSpec(shape, lambda *_: (0,) * nd, pipeline_mode=pl.Buffered(1))


def _sigmoid(x):
    return 1.0 / (1.0 + jnp.exp(-x))


def _softplus(x):
    return jnp.maximum(x, 0.0) + jnp.log1p(jnp.exp(-jnp.abs(x)))


def _rms(x, w):
    ms = jnp.mean(x * x, axis=-1, keepdims=True)
    return x * lax.rsqrt(ms + NORM_EPS) * w


def _dot(a, b):
    return jnp.dot(a, b, preferred_element_type=F32)


def _dot_nt(a, b):
    return lax.dot_general(a, b, (((1,), (1,)), ((), ())), preferred_element_type=F32)


def _split3(v):
    p1 = v.astype(BF16)
    r1 = v - p1.astype(F32)
    p2 = r1.astype(BF16)
    p3 = (r1 - p2.astype(F32)).astype(BF16)
    return p1, p2, p3


def _cast_kernel(w_ref, o_ref):
    o_ref[...] = w_ref[...].astype(BF16)


def _mix_in_kernel(w_ref, o_ref):
    o_ref[:, 0:_X1] = w_ref[:, 0:_X1].astype(BF16)
    o_ref[:, _X1:_G1] = w_ref[:, _X1 + SSM_HEADS:_G1 + SSM_HEADS].astype(BF16)


def _prep_weights(w, layer, body, out_cols):
    _, k, n = w.shape
    return pl.pallas_call(
        body,
        grid=(k // PREP_ROWS,),
        in_specs=[pl.BlockSpec((None, PREP_ROWS, n), lambda i: (layer, i, 0))],
        out_specs=pl.BlockSpec((PREP_ROWS, out_cols), lambda i: (i, 0)),
        out_shape=jax.ShapeDtypeStruct((k, out_cols), BF16),
        compiler_params=pltpu.CompilerParams(
            dimension_semantics=("arbitrary",), vmem_limit_bytes=VMEM_LIMIT),
        name="prep_weights",
    )(w)


_PrepJob = collections.namedtuple("_PrepJob", ["w", "layer", "body", "out_cols"])


def _job_specs(job, n_steps):
    _, k, n = job.w.shape
    nb = next(c for c in range(min(n_steps, k // BF16_ROWS), 0, -1)
              if k % c == 0 and (k // c) % BF16_ROWS == 0)
    rows = k // nb
    layer = job.layer
    return (pl.BlockSpec((None, rows, n), lambda i: (layer, i * nb // n_steps, 0)),
            pl.BlockSpec((rows, job.out_cols), lambda i: (i * nb // n_steps, 0)),
            jax.ShapeDtypeStruct((k, job.out_cols), BF16))


def _rope_kernel(pos_ref, invf_ref, cos_ref, sin_ref):
    ang = invf_ref[...] * pos_ref[...].astype(F32)
    c = jnp.cos(ang)
    s = jnp.sin(ang)
    cos_ref[...] = jnp.concatenate([c, c], axis=0).T
    sin_ref[...] = jnp.concatenate([-s, s], axis=0).T


def _rope_tables(positions):
    t = positions.size
    tb = min(ROPE_BLOCK, t)
    assert t % tb == 0
    half = RET_QK_DIM // 2
    inv_freq = ROPE_BASE ** (-jnp.arange(half, dtype=F32) / half)
    return pl.pallas_call(
        _rope_kernel,
        grid=(t // tb,),
        in_specs=[pl.BlockSpec((None, 1, tb), lambda i: (i, 0, 0)),
                  pl.BlockSpec((half, 1), lambda i: (0, 0))],
        out_specs=[pl.BlockSpec((tb, LANES), lambda i: (i, 0)),
                   pl.BlockSpec((tb, LANES), lambda i: (i, 0))],
        out_shape=[jax.ShapeDtypeStruct((t, LANES), F32)] * 2,
        name="rope_tables",
    )(positions.reshape(t // tb, 1, tb), inv_freq[:, None])


def _ffn_kernel(*refs, job_bodies):
    n_jobs = len(job_bodies)
    x_ref, pre_ref, win_ref, wout_ref, post_ref = refs[:5]
    job_in = refs[5:5 + n_jobs]
    o_ref = refs[5 + n_jobs]
    job_out = refs[6 + n_jobs:6 + 2 * n_jobs]
    xn_ref, act_ref = refs[6 + 2 * n_jobs:]

    tm = x_ref.shape[0]
    halves = (slice(0, tm // 2), slice(tm // 2, tm))
    slabs = list(range(0, D_FF, FFN_COL_CHUNK))

    def prenorm(r):
        xn_ref[r, :] = _rms(x_ref[r, :], pre_ref[...]).astype(BF16)

    def swiglu(r, lo):
        gate = _dot(xn_ref[r, :], win_ref[:, lo:lo + FFN_COL_CHUNK])
        up = _dot(xn_ref[r, :], win_ref[:, D_FF + lo:D_FF + lo + FFN_COL_CHUNK])
        act_ref[r, lo:lo + FFN_COL_CHUNK] = (gate * _sigmoid(gate) * up).astype(BF16)

    def project(r):
        h = _dot(act_ref[r, :], wout_ref[...])
        o_ref[r, :] = x_ref[r, :] + 0.5 * _rms(h, post_ref[...])

    first, second = halves
    prenorm(first)
    swiglu(first, slabs[0])
    prenorm(second)
    for lo in slabs[1:]:
        swiglu(first, lo)
    for body, w_ref, out_ref in zip(job_bodies, job_in, job_out):
        body(w_ref, out_ref)
    swiglu(second, slabs[0])
    project(first)
    for lo in slabs[1:]:
        swiglu(second, lo)
    project(second)


def _ffn(x2d, pre, w_in, w_out, post, jobs=()):
    t = x2d.shape[0]
    tm = min(FFN_BLOCK, t)
    n_steps = t // tm
    job_specs = [_job_specs(job, n_steps) for job in jobs]
    row = pl.BlockSpec((tm, D_MODEL), lambda i: (i, 0))
    return pl.pallas_call(
        functools.partial(_ffn_kernel, job_bodies=tuple(job.body for job in jobs)),
        grid=(n_steps,),
        in_specs=[row, _resident((1, D_MODEL)), _resident_block(w_in.shape),
                  _resident_block(w_out.shape), _resident((1, D_MODEL))] + [js[0] for js in job_specs],
        out_specs=[row] + [js[1] for js in job_specs],
        out_shape=[jax.ShapeDtypeStruct((t, D_MODEL), F32)] + [js[2] for js in job_specs],
        scratch_shapes=[pltpu.VMEM((tm, D_MODEL), BF16),
                        pltpu.VMEM((tm, D_FF), BF16)],
        compiler_params=pltpu.CompilerParams(
            dimension_semantics=("arbitrary",), vmem_limit_bytes=VMEM_LIMIT),
        name="ffn",
    )(x2d, pre[None, :], w_in, w_out, post[None, :], *[job.w for job in jobs])


_Slot = collections.namedtuple(
    "_Slot", ["x", "zs", "xs", "bc", "bT", "q", "qd", "k", "kdT", "v", "gs", "dt", "dtT"])


def _mixer_kernel(*refs, tb, nj):
    g = MIX_GROUP
    x_refs = refs[0:g]
    cos_refs = refs[g:2 * g]
    sin_refs = refs[2 * g:3 * g]
    consts = refs[3 * g:3 * g + N_MIX_CONSTS]
    o_ref = refs[3 * g + N_MIX_CONSTS]
    scratch = refs[3 * g + N_MIX_CONSTS + 1:]
    u_ref, xbc_ref, y_ref, cat_ref, mixed_ref, sstate_ref, rstate_ref = scratch[:7]
    n = len(_Slot._fields)
    slot_a = _Slot(*scratch[7:7 + n])
    slot_b = _Slot(*scratch[7 + n:7 + 2 * n])
    i = pl.program_id(0)

    @pl.when(i == 0)
    def _():
        for ref in slot_a:
            ref[...] = jnp.zeros(ref.shape, ref.dtype)
        xbc_ref[0:SUBLANES, :] = jnp.zeros((SUBLANES, CONV_DIM), F32)

    @pl.when(jnp.logical_or(i == 0, lax.rem(g * i + nj - g, nj) == 0))
    def _():
        sstate_ref[...] = jnp.zeros(sstate_ref.shape, F32)
        rstate_ref[...] = jnp.zeros(rstate_ref.shape, F32)

    shared = consts + (u_ref, xbc_ref, y_ref, cat_ref, mixed_ref, sstate_ref, rstate_ref)
    for k in range(g):
        p_slot, s_slot = (slot_b, slot_a) if k % 2 == 0 else (slot_a, slot_b)
        new_sequence = (lax.rem(g * i, nj) == 0) if k == g - 1 else None
        _mixer_step(x_refs[k], cos_refs[k], sin_refs[k], o_ref.at[pl.ds(k * tb, tb)], *shared,
                    p_slot, s_slot, new_sequence, tb=tb)


def _mixer_step(xp_ref, cos_ref, sin_ref, o_ref, pre_ref, wmain_ref, wdt_ref, wdtT_ref,
                convw_ref, convb_ref, dtb_row_ref, dtb_col_ref, alog_row_ref, alog_col_ref, dskip_ref,
                ssmnorm_ref, retnorm_ref, wout_ref, post_ref, expand_ref, tril_ref, triu_ref,
                dmat_ref, qdec_ref, kdec_ref, cdec_ref,
                u_ref, xbc_ref, y_ref, cat_ref, mixed_ref, sstate_ref, rstate_ref, P, S, new_sequence,
                *, tb):
    L = CHUNK
    if new_sequence is not None:
        tail = xbc_ref[0:SUBLANES, :]
        xbc_ref[0:SUBLANES, :] = jnp.where(new_sequence, jnp.zeros_like(tail), tail)

    def proj(lo, hi):
        return _dot(u_ref[...], wmain_ref[:, lo:hi])

    def p_norm():
        x = xp_ref[...]
        P.x[...] = x
        u_ref[...] = _rms(x, pre_ref[...]).astype(BF16)

    def p_gate(dst, base, j):
        def piece():
            sl = slice(j * SLAB, (j + 1) * SLAB)
            t = proj(base + j * SLAB, base + (j + 1) * SLAB)
            dst[:, sl] = t * _sigmoid(t)
        return piece

    def p_value(j):
        def piece():
            sl = slice(j * SLAB, (j + 1) * SLAB)
            P.v[:, sl] = proj(_V0 + j * SLAB, _V0 + (j + 1) * SLAB).astype(BF16)
        return piece

    def p_rotary(j):
        def piece():
            q = proj(_Q0 + j * SLAB, _Q0 + (j + 1) * SLAB)
            k = proj(_K0 + j * SLAB, _K0 + (j + 1) * SLAB)
            cos = cos_ref[...]
            sin = sin_ref[...]
            for hh in range(SLAB // RET_QK_DIM):
                loc = slice(hh * RET_QK_DIM, (hh + 1) * RET_QK_DIM)
                sl = slice(j * SLAB + hh * RET_QK_DIM, j * SLAB + (hh + 1) * RET_QK_DIM)
                qh = q[:, loc]
                kh = k[:, loc]
                qr = qh * cos + pltpu.roll(qh, RET_QK_DIM // 2, 1) * sin
                kr = (kh * cos + pltpu.roll(kh, RET_QK_DIM // 2, 1) * sin) * (RET_QK_DIM ** -0.5)
                P.q[:, sl] = qr.astype(BF16)
                P.qd[:, sl] = (qr * qdec_ref[:, sl]).astype(BF16)
                P.k[:, sl] = kr.astype(BF16)
                P.kdT[sl, :] = (kr * kdec_ref[:, sl]).T.astype(BF16)
        return piece

    def p_dt():
        P.dt[...] = _softplus(_dot(u_ref[...], wdt_ref[...]) + dtb_row_ref[...])
        P.dtT[...] = _softplus(_dot_nt(wdtT_ref[...], u_ref[...]) + dtb_col_ref[...])

    def p_conv(j):
        def piece():
            sl = slice(j * SLAB, (j + 1) * SLAB)
            xbc_ref[SUBLANES:SUBLANES + tb, sl] = proj(_X0 + j * SLAB, _X0 + (j + 1) * SLAB)
            conv = convb_ref[:, sl]
            for kk in range(CONV_K):
                off = SUBLANES - (CONV_K - 1) + kk
                conv = conv + convw_ref[kk:kk + 1, sl] * xbc_ref[off:off + tb, sl]
            act = conv * _sigmoid(conv)
            xbc_ref[0:SUBLANES, sl] = xbc_ref[tb:tb + SUBLANES, sl]
            if (j + 1) * SLAB <= SSM_WIDTH:
                P.xs[:, sl] = act
            else:
                lo = j * SLAB - SSM_WIDTH
                P.bc[:, lo:lo + SLAB] = act.astype(BF16)
                if lo < BC_WIDTH:
                    P.bT[lo:lo + SLAB, :] = act.T.astype(BF16)
        return piece

    a_row = -jnp.exp(alog_row_ref[...])
    a_col = -jnp.exp(alog_col_ref[...])
    rows = lax.broadcasted_iota(jnp.int32, (L, L), 0)
    cols = lax.broadcasted_iota(jnp.int32, (L, L), 1)
    causal = rows >= cols
    lane = lax.broadcasted_iota(jnp.int32, (L, LANES), 1)
    low_half = lane < SSM_HEAD_DIM

    def s_chunk_pieces(c):
        r = slice(c * L, (c + 1) * L)
        ctx = {}

        def decays():
            dt_c = S.dt[r, :]
            dtT_c = S.dtT[:, r]
            parts = _dot(tril_ref[...], jnp.concatenate(_split3(dt_c * a_row), axis=1))
            acum = parts[:, 0:LANES] + parts[:, LANES:2 * LANES] + parts[:, 2 * LANES:3 * LANES]
            partsT = _dot(jnp.concatenate(_split3(dtT_c * a_col), axis=0), triu_ref[...])
            acumT = (partsT[0:SSM_HEADS] + partsT[SSM_HEADS:2 * SSM_HEADS]
                     + partsT[2 * SSM_HEADS:3 * SSM_HEADS])
            ctx["acum"] = acum
            ctx["lrow"] = acumT - jnp.log(dtT_c)
            e = jnp.exp(acum)
            w = jnp.exp(acum[L - 1:L, :] - acum) * dt_c
            e_hi = e.astype(BF16).astype(F32)
            w_hi = w.astype(BF16).astype(F32)
            packed = jnp.where(lane < SSM_HEADS, e_hi,
                               jnp.where(lane < 2 * SSM_HEADS, e - e_hi,
                                         jnp.where(lane < 3 * SSM_HEADS, w_hi,
                                                   jnp.where(lane < 4 * SSM_HEADS, w - w_hi, 0.0))))
            ex = _dot(packed.astype(BF16), expand_ref[...])
            ctx["ex_e"] = ex[:, 0:SSM_WIDTH]
            ctx["xw"] = (S.xs[r, :] * ex[:, SSM_WIDTH:2 * SSM_WIDTH]).astype(BF16)

        def group_open(gi):
            def piece():
                gsl = slice(gi * GROUP_WIDTH, (gi + 1) * GROUP_WIDTH)
                b_g = S.bc[r, gi * SSM_STATE:(gi + 1) * SSM_STATE]
                c_g = S.bc[r, BC_WIDTH + gi * SSM_STATE:BC_WIDTH + (gi + 1) * SSM_STATE]
                ctx["cb"] = _dot_nt(c_g, b_g)
                st = sstate_ref[gi]
                ctx["y_inter"] = _dot(c_g, st.astype(BF16)) * ctx["ex_e"][:, gsl]
                sstate_ref[gi] = (st * ctx["ex_e"][L - 1:L, gsl]
                                  + _dot(S.bT[gi * SSM_STATE:(gi + 1) * SSM_STATE, r], ctx["xw"][:, gsl]))
            return piece

        def head_pair(gi, p):
            def piece():
                h0 = gi * HEADS_PER_GROUP + 2 * p
                psl = slice(h0 * SSM_HEAD_DIM, (h0 + 2) * SSM_HEAD_DIM)
                gmats = []
                for h in (h0, h0 + 1):
                    diff = ctx["acum"][:, h:h + 1] - ctx["lrow"][h:h + 1, :]
                    gmats.append(ctx["cb"] * jnp.exp(jnp.where(causal, diff, -jnp.inf)))
                lhs = jnp.concatenate(gmats, axis=1).astype(BF16)
                xp = S.xs[r, psl]
                rhs = jnp.concatenate([jnp.where(low_half, xp, 0.0),
                                       jnp.where(low_half, 0.0, xp)], axis=0).astype(BF16)
                y = (_dot(lhs, rhs) + ctx["y_inter"][:, 2 * p * SSM_HEAD_DIM:(2 * p + 2) * SSM_HEAD_DIM]
                     + xp * dskip_ref[:, psl])
                y_ref[:, psl] = y * S.zs[r, psl]
            return piece

        def ssd_norm():
            cat_ref[r, 0:SSM_WIDTH] = _rms(y_ref[...], ssmnorm_ref[...]).astype(BF16)

        def retention(h):
            def piece():
                qsl = slice(h * RET_QK_DIM, (h + 1) * RET_QK_DIM)
                vsl = slice(h * RET_V_DIM, (h + 1) * RET_V_DIM)
                vh = S.v[r, vsl]
                sc = _dot_nt(S.q[r, qsl], S.k[r, qsl]) * dmat_ref[h]
                st = rstate_ref[h]
                lhs = jnp.concatenate([sc.astype(BF16), S.qd[r, qsl]], axis=1)
                rhs = jnp.concatenate([vh, st.astype(BF16)], axis=0)
                rh = _dot(lhs, rhs)
                rstate_ref[h] = st * cdec_ref[:, vsl] + _dot(S.kdT[qsl, r], vh)
                d = rh - jnp.mean(rh, axis=-1, keepdims=True)
                var = jnp.mean(d * d, axis=-1, keepdims=True)
                rn = d * lax.rsqrt(var + RET_NORM_EPS) * retnorm_ref[:, vsl]
                cat_ref[r, SSM_WIDTH + h * RET_V_DIM:SSM_WIDTH + (h + 1) * RET_V_DIM] = (
                    S.gs[r, vsl] * rn).astype(BF16)
            return piece

        pieces = [decays]
        for gi in range(SSM_GROUPS):
            pieces.append(group_open(gi))
            pieces += [head_pair(gi, p) for p in range(HEADS_PER_GROUP // 2)]
        pieces.append(ssd_norm)
        pieces += [retention(h) for h in range(RET_HEADS)]
        return pieces

    def s_out(j):
        def piece():
            sl = slice(j * SLAB, (j + 1) * SLAB)
            mixed_ref[:, sl] = _dot(cat_ref[...], wout_ref[:, sl])
        return piece

    def s_residual():
        o_ref[...] = S.x[...] + _rms(mixed_ref[...], post_ref[...])

    p_light = ([p_gate(P.zs, _Z0, j) for j in range(SSM_WIDTH // SLAB)]
               + [p_gate(P.gs, _G0, j) for j in range(RET_WIDTH // SLAB)]
               + [p_rotary(j) for j in range(RET_QK_WIDTH // SLAB)] + [p_dt])
    s_scan = [piece for c in range(tb // L) for piece in s_chunk_pieces(c)]
    p_norm()
    n_p, n_s = len(p_light), len(s_scan)
    done = 0
    for i, piece in enumerate(p_light):
        piece()
        upto = (i + 1) * n_s // n_p
        for sp in s_scan[done:upto]:
            sp()
        done = upto
    convs = [p_conv(j) for j in range(CONV_DIM // SLAB)]
    outs = [s_out(j) for j in range(D_MODEL // SLAB)]
    values = [p_value(j) for j in range(RET_WIDTH // SLAB)]
    for out_piece, conv_piece in zip(outs, convs):
        out_piece()
        conv_piece()
    s_residual()
    rest = convs[len(outs):]
    for i, piece in enumerate(values):
        piece()
        if i < len(rest):
            rest[i]()


def _retention_tables(tb):
    log_gamma = jnp.log(1.0 - 2.0 ** (-5.0 - jnp.arange(RET_HEADS, dtype=F32)))
    pos = jnp.arange(CHUNK, dtype=F32)
    rel = pos[:, None] - pos[None, :]
    dmat = jnp.exp(jnp.where((rel >= 0)[None], rel[None] * log_gamma[:, None, None], -jnp.inf))
    q_decay = jnp.exp((pos[:, None] + 1.0) * log_gamma[None])
    k_decay = jnp.exp((CHUNK - 1.0 - pos[:, None]) * log_gamma[None])
    chunk_decay = jnp.exp(CHUNK * log_gamma)
    qdec = jnp.tile(jnp.repeat(q_decay, RET_QK_DIM, axis=1), (tb // CHUNK, 1))
    kdec = jnp.tile(jnp.repeat(k_decay, RET_QK_DIM, axis=1), (tb // CHUNK, 1))
    cdec = jnp.repeat(chunk_decay, RET_V_DIM)[None, :]
    return dmat, qdec, kdec, cdec


def _constant_matrices():
    e = np.zeros((LANES, 2 * SSM_WIDTH), np.float32)
    for g in range(DT_COPIES):
        for h in range(SSM_HEADS):
            c0 = (g // 2) * SSM_WIDTH + h * SSM_HEAD_DIM
            e[g * SSM_HEADS + h, c0:c0 + SSM_HEAD_DIM] = 1.0
    tril = np.tril(np.ones((CHUNK, CHUNK), np.float32))
    return jnp.asarray(e, BF16), jnp.asarray(tril, BF16), jnp.asarray(tril.T, BF16)


def _slot_scratch(tb):
    return list(_Slot(
        x=pltpu.VMEM((tb, D_MODEL), F32),
        zs=pltpu.VMEM((tb, SSM_WIDTH), F32),
        xs=pltpu.VMEM((tb, SSM_WIDTH), F32),
        bc=pltpu.VMEM((tb, 2 * BC_WIDTH), BF16),
        bT=pltpu.VMEM((BC_WIDTH, tb), BF16),
        q=pltpu.VMEM((tb, RET_QK_WIDTH), BF16),
        qd=pltpu.VMEM((tb, RET_QK_WIDTH), BF16),
        k=pltpu.VMEM((tb, RET_QK_WIDTH), BF16),
        kdT=pltpu.VMEM((RET_QK_WIDTH, tb), BF16),
        v=pltpu.VMEM((tb, RET_WIDTH), BF16),
        gs=pltpu.VMEM((tb, RET_WIDTH), F32),
        dt=pltpu.VMEM((tb, LANES), F32),
        dtT=pltpu.VMEM((SSM_HEADS, tb), F32),
    ))


def _mixer(x2d, batch, seq, cos, sin, pre, w_in, w_main, conv_w, conv_b, dt_bias, a_log, d_skip,
           ssm_norm, ret_norm, w_out, post):
    tb = min(MIX_BLOCK, seq)
    nj = seq // tb
    nblocks = batch * nj
    w_dt = w_in[:, _X1:_X1 + SSM_HEADS]
    lane_pad = (0, LANES - DT_COPIES * SSM_HEADS)
    w_dt_pad = jnp.pad(jnp.tile(w_dt, (1, DT_COPIES)), ((0, 0), lane_pad)).astype(BF16)
    w_dt_t = w_dt.T.astype(BF16)
    dtb_row = jnp.pad(jnp.tile(dt_bias, DT_COPIES), lane_pad)[None, :]
    alog_row = jnp.pad(jnp.tile(a_log, DT_COPIES), lane_pad)[None, :]
    dskip = jnp.repeat(d_skip, SSM_HEAD_DIM)[None, :]
    dmat, qdec, kdec, cdec = _retention_tables(tb)
    expand, tril, triu = _constant_matrices()

    g = MIX_GROUP
    assert nj % g == 0, "a grid step's blocks must not straddle two sequences"

    def block(k):
        return lambda i: (jnp.clip(g * i - g + 1 + k, 0, nblocks - 1), 0)

    prev = lambda i: (jnp.maximum(i - 1, 0), 0)
    operands = (
        [(x2d, pl.BlockSpec((tb, D_MODEL), block(k))) for k in range(g)]
        + [(cos, pl.BlockSpec((tb, LANES), block(k))) for k in range(g)]
        + [(sin, pl.BlockSpec((tb, LANES), block(k))) for k in range(g)])
    consts = [
        (pre[None, :], None),
        (w_main, _resident_block(w_main.shape)),
        (w_dt_pad, None),
        (w_dt_t, None),
        (conv_w, None),
        (conv_b[None, :], None),
        (dtb_row, None),
        (dt_bias[:, None], None),
        (alog_row, None),
        (a_log[:, None], None),
        (dskip, None),
        (ssm_norm[None, :], None),
        (ret_norm[None, :], None),
        (w_out, _resident_block(w_out.shape)),
        (post[None, :], None),
        (expand, None),
        (tril, None),
        (triu, None),
        (dmat, None),
        (qdec, None),
        (kdec, None),
        (cdec, None),
    ]
    assert len(consts) == N_MIX_CONSTS
    operands += consts
    args = [a for a, _ in operands]
    in_specs = [sp if sp is not None else _resident(a.shape) for a, sp in operands]
    return pl.pallas_call(
        functools.partial(_mixer_kernel, tb=tb, nj=nj),
        grid=(nblocks // g + 1,),
        in_specs=in_specs,
        out_specs=pl.BlockSpec((g * tb, D_MODEL), prev),
        out_shape=jax.ShapeDtypeStruct(x2d.shape, F32),
        scratch_shapes=[
            pltpu.VMEM((tb, D_MODEL), BF16),
            pltpu.VMEM((SUBLANES + tb, CONV_DIM), F32),
            pltpu.VMEM((CHUNK, SSM_WIDTH), F32),
            pltpu.VMEM((tb, D_INNER), BF16),
            pltpu.VMEM((tb, D_MODEL), F32),
            pltpu.VMEM((SSM_GROUPS, SSM_STATE, GROUP_WIDTH), F32),
            pltpu.VMEM((RET_HEADS, RET_QK_DIM, RET_V_DIM), F32),
        ] + 2 * _slot_scratch(tb),
        compiler_params=pltpu.CompilerParams(
            dimension_semantics=("arbitrary",), vmem_limit_bytes=VMEM_LIMIT),
        name="mixer",
    )(*args)


def kernel(x, positions, ffn1_pre_norm, ffn1_w_in, ffn1_w_out, ffn1_post_norm, mix_pre_norm, mix_w_in, conv_w, conv_b, dt_bias, a_log, d_skip, ssm_norm, ret_norm, mix_w_out, mix_post_norm, ffn2_pre_norm, ffn2_w_in, ffn2_w_out, ffn2_post_norm):
    batch, seq, _ = x.shape
    depth = ffn1_w_in.shape[0]
    cos, sin = _rope_tables(positions)

    def cast_job(w, layer):
        return _PrepJob(w, layer, _cast_kernel, w.shape[2])

    f1_in = _prep_weights(ffn1_w_in, 0, _cast_kernel, 2 * D_FF)
    f1_out = _prep_weights(ffn1_w_out, 0, _cast_kernel, D_MODEL)
    h = x.reshape(batch * seq, D_MODEL)
    for l in range(depth):
        jobs = [_PrepJob(mix_w_in, l, _mix_in_kernel, _G1), cast_job(mix_w_out, l),
                cast_job(ffn2_w_in, l), cast_job(ffn2_w_out, l)]
        h, m_in, m_out, f2_in, f2_out = _ffn(h, ffn1_pre_norm[l], f1_in, f1_out, ffn1_post_norm[l], jobs)
        h = _mixer(h, batch, seq, cos, sin, mix_pre_norm[l], mix_w_in[l], m_in, conv_w[l], conv_b[l],
                   dt_bias[l], a_log[l], d_skip[l], ssm_norm[l], ret_norm[l], m_out, mix_post_norm[l])
        jobs = [cast_job(ffn1_w_in, l + 1), cast_job(ffn1_w_out, l + 1)] if l + 1 < depth else []
        h, *prepared = _ffn(h, ffn2_pre_norm[l], f2_in, f2_out, ffn2_post_norm[l], jobs)
        if prepared:
            f1_in, f1_out = prepared
    return h.reshape(batch, seq, D_MODEL)
```

```python
import collections
import functools

import jax
import jax.numpy as jnp
import numpy as np
from jax import lax
from jax.experimental import pallas as pl
from jax.experimental.pallas import tpu as pltpu

F32 = jnp.float32
BF16 = jnp.bfloat16

D_MODEL = 1024
D_FF = 2816
NORM_EPS = 1e-6
RET_NORM_EPS = 1e-5

D_INNER = 2 * D_MODEL
SSM_WIDTH = D_INNER // 2
RET_WIDTH = D_INNER - SSM_WIDTH
SSM_HEAD_DIM = 64
SSM_HEADS = SSM_WIDTH // SSM_HEAD_DIM
SSM_GROUPS = 2
HEADS_PER_GROUP = SSM_HEADS // SSM_GROUPS
GROUP_WIDTH = HEADS_PER_GROUP * SSM_HEAD_DIM
SSM_STATE = 128
BC_WIDTH = SSM_GROUPS * SSM_STATE
CONV_K = 4
CONV_DIM = SSM_WIDTH + 2 * BC_WIDTH
RET_HEADS = 4
RET_V_DIM = RET_WIDTH // RET_HEADS
RET_QK_DIM = RET_V_DIM // 2
RET_QK_WIDTH = RET_HEADS * RET_QK_DIM
ROPE_BASE = 10000.0

LANES = 128
SUBLANES = 8
BF16_ROWS = 16

CHUNK = 128
MIX_BLOCK = 256
MIX_GROUP = 4
N_MIX_CONSTS = 22
FFN_BLOCK = 1024
FFN_COL_CHUNK = 256
ROPE_BLOCK = 2048
VMEM_LIMIT = 58 * 1024 * 1024
DT_COPIES = 4
SLAB = 256
PREP_ROWS = 256

_Z0, _Z1 = 0, SSM_WIDTH
_X0, _X1 = _Z1, _Z1 + CONV_DIM
_Q0, _Q1 = _X1, _X1 + RET_QK_WIDTH
_K0, _K1 = _Q1, _Q1 + RET_QK_WIDTH
_V0, _V1 = _K1, _K1 + RET_WIDTH
_G0, _G1 = _V1, _V1 + RET_WIDTH


def _resident(shape):
    del shape
    return pl.BlockSpec(memory_space=pltpu.VMEM)


def _resident_block(shape):
    nd = len(shape)
    return pl.BlockSpec(shape, lambda *_: (0,) * nd, pipeline_mode=pl.Buffered(1))


def _sigmoid(x):
    return 1.0 / (1.0 + jnp.exp(-x))


def _softplus(x):
    return jnp.maximum(x, 0.0) + jnp.log1p(jnp.exp(-jnp.abs(x)))


def _rms(x, w):
    ms = jnp.mean(x * x, axis=-1, keepdims=True)
    return x * lax.rsqrt(ms + NORM_EPS) * w


def _dot(a, b):
    return jnp.dot(a, b, preferred_element_type=F32)


def _dot_nt(a, b):
    return lax.dot_general(a, b, (((1,), (1,)), ((), ())), preferred_element_type=F32)


def _split3(v):
    p1 = v.astype(BF16)
    r1 = v - p1.astype(F32)
    p2 = r1.astype(BF16)
    p3 = (r1 - p2.astype(F32)).astype(BF16)
    return p1, p2, p3


def _cast_kernel(w_ref, o_ref):
    o_ref[...] = w_ref[...].astype(BF16)


def _mix_in_kernel(w_ref, o_ref):
    o_ref[:, 0:_X1] = w_ref[:, 0:_X1].astype(BF16)
    o_ref[:, _X1:_G1] = w_ref[:, _X1 + SSM_HEADS:_G1 + SSM_HEADS].astype(BF16)


def _mix_dt_kernel(w_ref, o_ref):
    lane = lax.broadcasted_iota(jnp.int32, (w_ref.shape[0], LANES), 1)
    d = jnp.where(lane < SSM_HEADS, w_ref[:, _X1:_X1 + LANES], 0.0)
    out = d
    for c in range(1, DT_COPIES):
        out = out + pltpu.roll(d, c * SSM_HEADS, 1)
    o_ref[...] = out.astype(BF16)


def _prep_weights(w, layer, body, out_cols):
    _, k, n = w.shape
    return pl.pallas_call(
        body,
        grid=(k // PREP_ROWS,),
        in_specs=[pl.BlockSpec((None, PREP_ROWS, n), lambda i: (layer, i, 0))],
        out_specs=pl.BlockSpec((PREP_ROWS, out_cols), lambda i: (i, 0)),
        out_shape=jax.ShapeDtypeStruct((k, out_cols), BF16),
        compiler_params=pltpu.CompilerParams(
            dimension_semantics=("arbitrary",), vmem_limit_bytes=VMEM_LIMIT),
        name="prep_weights",
    )(w)


_PrepJob = collections.namedtuple("_PrepJob", ["w", "layer", "body", "out_cols"])


def _job_specs(job, n_steps):
    _, k, n = job.w.shape
    nb = next(c for c in range(min(n_steps, k // BF16_ROWS), 0, -1)
              if k % c == 0 and (k // c) % BF16_ROWS == 0)
    rows = k // nb
    layer = job.layer
    return (pl.BlockSpec((None, rows, n), lambda i: (layer, i * nb // n_steps, 0)),
            pl.BlockSpec((rows, job.out_cols), lambda i: (i * nb // n_steps, 0)),
            jax.ShapeDtypeStruct((k, job.out_cols), BF16))


def _rope_kernel(pos_ref, invf_ref, cos_ref, sin_ref):
    ang = invf_ref[...] * pos_ref[...].astype(F32)
    c = jnp.cos(ang)
    s = jnp.sin(ang)
    cos_ref[...] = jnp.concatenate([c, c], axis=0).T
    sin_ref[...] = jnp.concatenate([-s, s], axis=0).T


def _rope_tables(positions):
    t = positions.size
    tb = min(ROPE_BLOCK, t)
    assert t % tb == 0
    half = RET_QK_DIM // 2
    inv_freq = ROPE_BASE ** (-jnp.arange(half, dtype=F32) / half)
    return pl.pallas_call(
        _rope_kernel,
        grid=(t // tb,),
        in_specs=[pl.BlockSpec((None, 1, tb), lambda i: (i, 0, 0)),
                  pl.BlockSpec((half, 1), lambda i: (0, 0))],
        out_specs=[pl.BlockSpec((tb, LANES), lambda i: (i, 0)),
                   pl.BlockSpec((tb, LANES), lambda i: (i, 0))],
        out_shape=[jax.ShapeDtypeStruct((t, LANES), F32)] * 2,
        name="rope_tables",
    )(positions.reshape(t // tb, 1, tb), inv_freq[:, None])


def _ffn_kernel(*refs, job_bodies):
    n_jobs = len(job_bodies)
    x_ref, pre_ref, win_ref, wout_ref, post_ref = refs[:5]
    job_in = refs[5:5 + n_jobs]
    o_ref = refs[5 + n_jobs]
    job_out = refs[6 + n_jobs:6 + 2 * n_jobs]
    xn_ref, act_ref = refs[6 + 2 * n_jobs:]

    tm = x_ref.shape[0]
    halves = (slice(0, tm // 2), slice(tm // 2, tm))
    slabs = list(range(0, D_FF, FFN_COL_CHUNK))

    def prenorm(r):
        xn_ref[r, :] = _rms(x_ref[r, :], pre_ref[...]).astype(BF16)

    def swiglu(r, lo):
        gate = _dot(xn_ref[r, :], win_ref[:, lo:lo + FFN_COL_CHUNK])
        up = _dot(xn_ref[r, :], win_ref[:, D_FF + lo:D_FF + lo + FFN_COL_CHUNK])
        act_ref[r, lo:lo + FFN_COL_CHUNK] = (gate * _sigmoid(gate) * up).astype(BF16)

    def project(r):
        h = _dot(act_ref[r, :], wout_ref[...])
        o_ref[r, :] = x_ref[r, :] + 0.5 * _rms(h, post_ref[...])

    first, second = halves
    prenorm(first)
    swiglu(first, slabs[0])
    prenorm(second)
    for lo in slabs[1:]:
        swiglu(first, lo)
    for body, w_ref, out_ref in zip(job_bodies, job_in, job_out):
        body(w_ref, out_ref)
    swiglu(second, slabs[0])
    project(first)
    for lo in slabs[1:]:
        swiglu(second, lo)
    project(second)


def _ffn(x2d, pre, w_in, w_out, post, jobs=()):
    t = x2d.shape[0]
    tm = min(FFN_BLOCK, t)
    n_steps = t // tm
    job_specs = [_job_specs(job, n_steps) for job in jobs]
    row = pl.BlockSpec((tm, D_MODEL), lambda i: (i, 0))
    return pl.pallas_call(
        functools.partial(_ffn_kernel, job_bodies=tuple(job.body for job in jobs)),
        grid=(n_steps,),
        in_specs=[row, _resident((1, D_MODEL)), _resident_block(w_in.shape),
                  _resident_block(w_out.shape), _resident((1, D_MODEL))] + [js[0] for js in job_specs],
        out_specs=[row] + [js[1] for js in job_specs],
        out_shape=[jax.ShapeDtypeStruct((t, D_MODEL), F32)] + [js[2] for js in job_specs],
        scratch_shapes=[pltpu.VMEM((tm, D_MODEL), BF16),
                        pltpu.VMEM((tm, D_FF), BF16)],
        compiler_params=pltpu.CompilerParams(
            dimension_semantics=("arbitrary",), vmem_limit_bytes=VMEM_LIMIT),
        name="ffn",
    )(x2d, pre[None, :], w_in, w_out, post[None, :], *[job.w for job in jobs])


_Slot = collections.namedtuple(
    "_Slot", ["x", "zs", "xs", "bc", "bT", "q", "qd", "k", "kdT", "v", "gs", "dt", "dtT"])


def _mixer_kernel(*refs, tb, nj):
    g = MIX_GROUP
    x_refs = refs[0:g]
    cos_refs = refs[g:2 * g]
    sin_refs = refs[2 * g:3 * g]
    consts = refs[3 * g:3 * g + N_MIX_CONSTS]
    o_ref = refs[3 * g + N_MIX_CONSTS]
    scratch = refs[3 * g + N_MIX_CONSTS + 1:]
    u_ref, xbc_ref, y_ref, cat_ref, mixed_ref, sstate_ref, rstate_ref = scratch[:7]
    n = len(_Slot._fields)
    slot_a = _Slot(*scratch[7:7 + n])
    slot_b = _Slot(*scratch[7 + n:7 + 2 * n])
    i = pl.program_id(0)

    @pl.when(i == 0)
    def _():
        for ref in slot_a:
            ref[...] = jnp.zeros(ref.shape, ref.dtype)
        xbc_ref[0:SUBLANES, :] = jnp.zeros((SUBLANES, CONV_DIM), F32)

    @pl.when(jnp.logical_or(i == 0, lax.rem(g * i + nj - g, nj) == 0))
    def _():
        sstate_ref[...] = jnp.zeros(sstate_ref.shape, F32)
        rstate_ref[...] = jnp.zeros(rstate_ref.shape, F32)

    shared = consts + (u_ref, xbc_ref, y_ref, cat_ref, mixed_ref, sstate_ref, rstate_ref)
    for k in range(g):
        p_slot, s_slot = (slot_b, slot_a) if k % 2 == 0 else (slot_a, slot_b)
        new_sequence = (lax.rem(g * i, nj) == 0) if k == g - 1 else None
        _mixer_step(x_refs[k], cos_refs[k], sin_refs[k], o_ref.at[pl.ds(k * tb, tb)], *shared,
                    p_slot, s_slot, new_sequence, tb=tb)


def _mixer_step(xp_ref, cos_ref, sin_ref, o_ref, pre_ref, wmain_ref, wdt_ref, wdtT_ref,
                convw_ref, convb_ref, dtb_row_ref, dtb_col_ref, alog_row_ref, alog_col_ref, dskip_ref,
                ssmnorm_ref, retnorm_ref, wout_ref, post_ref, expand_ref, tril_ref, triu_ref,
                dmat_ref, qdec_ref, kdec_ref, cdec_ref,
                u_ref, xbc_ref, y_ref, cat_ref, mixed_ref, sstate_ref, rstate_ref, P, S, new_sequence,
                *, tb):
    L = CHUNK
    if new_sequence is not None:
        tail = xbc_ref[0:SUBLANES, :]
        xbc_ref[0:SUBLANES, :] = jnp.where(new_sequence, jnp.zeros_like(tail), tail)

    def proj(lo, hi):
        return _dot(u_ref[...], wmain_ref[:, lo:hi])

    def p_norm():
        x = xp_ref[...]
        P.x[...] = x
        u_ref[...] = _rms(x, pre_ref[...]).astype(BF16)

    def p_gate(dst, base, j):
        def piece():
            sl = slice(j * SLAB, (j + 1) * SLAB)
            t = proj(base + j * SLAB, base + (j + 1) * SLAB)
            dst[:, sl] = t * _sigmoid(t)
        return piece

    def p_value(j):
        def piece():
            sl = slice(j * SLAB, (j + 1) * SLAB)
            P.v[:, sl] = proj(_V0 + j * SLAB, _V0 + (j + 1) * SLAB).astype(BF16)
        return piece

    def p_rotary(j):
        def piece():
            q = proj(_Q0 + j * SLAB, _Q0 + (j + 1) * SLAB)
            k = proj(_K0 + j * SLAB, _K0 + (j + 1) * SLAB)
            cos = cos_ref[...]
            sin = sin_ref[...]
            for hh in range(SLAB // RET_QK_DIM):
                loc = slice(hh * RET_QK_DIM, (hh + 1) * RET_QK_DIM)
                sl = slice(j * SLAB + hh * RET_QK_DIM, j * SLAB + (hh + 1) * RET_QK_DIM)
                qh = q[:, loc]
                kh = k[:, loc]
                qr = qh * cos + pltpu.roll(qh, RET_QK_DIM // 2, 1) * sin
                kr = (kh * cos + pltpu.roll(kh, RET_QK_DIM // 2, 1) * sin) * (RET_QK_DIM ** -0.5)
                P.q[:, sl] = qr.astype(BF16)
                P.qd[:, sl] = (qr * qdec_ref[:, sl]).astype(BF16)
                P.k[:, sl] = kr.astype(BF16)
                P.kdT[sl, :] = (kr * kdec_ref[:, sl]).T.astype(BF16)
        return piece

    def p_dt():
        P.dt[...] = _softplus(_dot(u_ref[...], wdt_ref[...]) + dtb_row_ref[...])
        P.dtT[...] = _softplus(_dot_nt(wdtT_ref[...], u_ref[...]) + dtb_col_ref[...])

    def p_conv(j):
        def piece():
            sl = slice(j * SLAB, (j + 1) * SLAB)
            xbc_ref[SUBLANES:SUBLANES + tb, sl] = proj(_X0 + j * SLAB, _X0 + (j + 1) * SLAB)
            conv = convb_ref[:, sl]
            for kk in range(CONV_K):
                off = SUBLANES - (CONV_K - 1) + kk
                conv = conv + convw_ref[kk:kk + 1, sl] * xbc_ref[off:off + tb, sl]
            act = conv * _sigmoid(conv)
            xbc_ref[0:SUBLANES, sl] = xbc_ref[tb:tb + SUBLANES, sl]
            if (j + 1) * SLAB <= SSM_WIDTH:
                P.xs[:, sl] = act
            else:
                lo = j * SLAB - SSM_WIDTH
                P.bc[:, lo:lo + SLAB] = act.astype(BF16)
                if lo < BC_WIDTH:
                    P.bT[lo:lo + SLAB, :] = act.T.astype(BF16)
        return piece

    a_row = -jnp.exp(alog_row_ref[...])
    a_col = -jnp.exp(alog_col_ref[...])
    rows = lax.broadcasted_iota(jnp.int32, (L, L), 0)
    cols = lax.broadcasted_iota(jnp.int32, (L, L), 1)
    causal = rows >= cols
    lane = lax.broadcasted_iota(jnp.int32, (L, LANES), 1)
    low_half = lane < SSM_HEAD_DIM

    def s_chunk_pieces(c):
        r = slice(c * L, (c + 1) * L)
        ctx = {}

        def decays():
            dt_c = S.dt[r, :]
            dtT_c = S.dtT[:, r]
            parts = _dot(tril_ref[...], jnp.concatenate(_split3(dt_c * a_row), axis=1))
            acum = parts[:, 0:LANES] + parts[:, LANES:2 * LANES] + parts[:, 2 * LANES:3 * LANES]
            partsT = _dot(jnp.concatenate(_split3(dtT_c * a_col), axis=0), triu_ref[...])
            acumT = (partsT[0:SSM_HEADS] + partsT[SSM_HEADS:2 * SSM_HEADS]
                     + partsT[2 * SSM_HEADS:3 * SSM_HEADS])
            ctx["acum"] = acum
            ctx["lrow"] = acumT - jnp.log(dtT_c)
            e = jnp.exp(acum)
            w = jnp.exp(acum[L - 1:L, :] - acum) * dt_c
            e_hi = e.astype(BF16).astype(F32)
            w_hi = w.astype(BF16).astype(F32)
            packed = jnp.where(lane < SSM_HEADS, e_hi,
                               jnp.where(lane < 2 * SSM_HEADS, e - e_hi,
                                         jnp.where(lane < 3 * SSM_HEADS, w_hi,
                                                   jnp.where(lane < 4 * SSM_HEADS, w - w_hi, 0.0))))
            ex = _dot(packed.astype(BF16), expand_ref[...])
            ctx["ex_e"] = ex[:, 0:SSM_WIDTH]
            ctx["xw"] = (S.xs[r, :] * ex[:, SSM_WIDTH:2 * SSM_WIDTH]).astype(BF16)

        def group_open(gi):
            def piece():
                gsl = slice(gi * GROUP_WIDTH, (gi + 1) * GROUP_WIDTH)
                b_g = S.bc[r, gi * SSM_STATE:(gi + 1) * SSM_STATE]
                c_g = S.bc[r, BC_WIDTH + gi * SSM_STATE:BC_WIDTH + (gi + 1) * SSM_STATE]
                ctx["cb"] = _dot_nt(c_g, b_g)
                st = sstate_ref[gi]
                ctx["y_inter"] = _dot(c_g, st.astype(BF16)) * ctx["ex_e"][:, gsl]
                sstate_ref[gi] = (st * ctx["ex_e"][L - 1:L, gsl]
                                  + _dot(S.bT[gi * SSM_STATE:(gi + 1) * SSM_STATE, r], ctx["xw"][:, gsl]))
            return piece

        def head_pair(gi, p):
            def piece():
                h0 = gi * HEADS_PER_GROUP + 2 * p
                psl = slice(h0 * SSM_HEAD_DIM, (h0 + 2) * SSM_HEAD_DIM)
                gmats = []
                for h in (h0, h0 + 1):
                    diff = ctx["acum"][:, h:h + 1] - ctx["lrow"][h:h + 1, :]
                    gmats.append(ctx["cb"] * jnp.exp(jnp.where(causal, diff, -jnp.inf)))
                lhs = jnp.concatenate(gmats, axis=1).astype(BF16)
                xp = S.xs[r, psl]
                rhs = jnp.concatenate([jnp.where(low_half, xp, 0.0),
                                       jnp.where(low_half, 0.0, xp)], axis=0).astype(BF16)
                y = (_dot(lhs, rhs) + ctx["y_inter"][:, 2 * p * SSM_HEAD_DIM:(2 * p + 2) * SSM_HEAD_DIM]
                     + xp * dskip_ref[:, psl])
                y_ref[:, psl] = y * S.zs[r, psl]
            return piece

        def ssd_norm():
            cat_ref[r, 0:SSM_WIDTH] = _rms(y_ref[...], ssmnorm_ref[...]).astype(BF16)

        def retention(h):
            def piece():
                qsl = slice(h * RET_QK_DIM, (h + 1) * RET_QK_DIM)
                vsl = slice(h * RET_V_DIM, (h + 1) * RET_V_DIM)
                vh = S.v[r, vsl]
                sc = _dot_nt(S.q[r, qsl], S.k[r, qsl]) * dmat_ref[h]
                st = rstate_ref[h]
                lhs = jnp.concatenate([sc.astype(BF16), S.qd[r, qsl]], axis=1)
                rhs = jnp.concatenate([vh, st.astype(BF16)], axis=0)
                rh = _dot(lhs, rhs)
                rstate_ref[h] = st * cdec_ref[:, vsl] + _dot(S.kdT[qsl, r], vh)
                d = rh - jnp.mean(rh, axis=-1, keepdims=True)
                var = jnp.mean(d * d, axis=-1, keepdims=True)
                rn = d * lax.rsqrt(var + RET_NORM_EPS) * retnorm_ref[:, vsl]
                cat_ref[r, SSM_WIDTH + h * RET_V_DIM:SSM_WIDTH + (h + 1) * RET_V_DIM] = (
                    S.gs[r, vsl] * rn).astype(BF16)
            return piece

        pieces = [decays]
        for gi in range(SSM_GROUPS):
            pieces.append(group_open(gi))
            pieces += [head_pair(gi, p) for p in range(HEADS_PER_GROUP // 2)]
        pieces.append(ssd_norm)
        pieces += [retention(h) for h in range(RET_HEADS)]
        return pieces

    def s_out(j):
        def piece():
            sl = slice(j * SLAB, (j + 1) * SLAB)
            mixed_ref[:, sl] = _dot(cat_ref[...], wout_ref[:, sl])
        return piece

    def s_residual():
        o_ref[...] = S.x[...] + _rms(mixed_ref[...], post_ref[...])

    p_light = ([p_gate(P.zs, _Z0, j) for j in range(SSM_WIDTH // SLAB)]
               + [p_gate(P.gs, _G0, j) for j in range(RET_WIDTH // SLAB)]
               + [p_rotary(j) for j in range(RET_QK_WIDTH // SLAB)] + [p_dt])
    s_scan = [piece for c in range(tb // L) for piece in s_chunk_pieces(c)]
    p_norm()
    n_p, n_s = len(p_light), len(s_scan)
    done = 0
    for i, piece in enumerate(p_light):
        piece()
        upto = (i + 1) * n_s // n_p
        for sp in s_scan[done:upto]:
            sp()
        done = upto
    convs = [p_conv(j) for j in range(CONV_DIM // SLAB)]
    outs = [s_out(j) for j in range(D_MODEL // SLAB)]
    values = [p_value(j) for j in range(RET_WIDTH // SLAB)]
    for out_piece, conv_piece in zip(outs, convs):
        out_piece()
        conv_piece()
    s_residual()
    rest = convs[len(outs):]
    for i, piece in enumerate(values):
        piece()
        if i < len(rest):
            rest[i]()


def _retention_tables(tb):
    log_gamma = jnp.log(1.0 - 2.0 ** (-5.0 - jnp.arange(RET_HEADS, dtype=F32)))
    pos = jnp.arange(CHUNK, dtype=F32)
    rel = pos[:, None] - pos[None, :]
    dmat = jnp.exp(jnp.where((rel >= 0)[None], rel[None] * log_gamma[:, None, None], -jnp.inf))
    q_decay = jnp.exp((pos[:, None] + 1.0) * log_gamma[None])
    k_decay = jnp.exp((CHUNK - 1.0 - pos[:, None]) * log_gamma[None])
    chunk_decay = jnp.exp(CHUNK * log_gamma)
    qdec = jnp.tile(jnp.repeat(q_decay, RET_QK_DIM, axis=1), (tb // CHUNK, 1))
    kdec = jnp.tile(jnp.repeat(k_decay, RET_QK_DIM, axis=1), (tb // CHUNK, 1))
    cdec = jnp.repeat(chunk_decay, RET_V_DIM)[None, :]
    return dmat, qdec, kdec, cdec


def _constant_matrices():
    e = np.zeros((LANES, 2 * SSM_WIDTH), np.float32)
    for g in range(DT_COPIES):
        for h in range(SSM_HEADS):
            c0 = (g // 2) * SSM_WIDTH + h * SSM_HEAD_DIM
            e[g * SSM_HEADS + h, c0:c0 + SSM_HEAD_DIM] = 1.0
    tril = np.tril(np.ones((CHUNK, CHUNK), np.float32))
    return jnp.asarray(e, BF16), jnp.asarray(tril, BF16), jnp.asarray(tril.T, BF16)


def _slot_scratch(tb):
    return list(_Slot(
        x=pltpu.VMEM((tb, D_MODEL), F32),
        zs=pltpu.VMEM((tb, SSM_WIDTH), F32),
        xs=pltpu.VMEM((tb, SSM_WIDTH), F32),
        bc=pltpu.VMEM((tb, 2 * BC_WIDTH), BF16),
        bT=pltpu.VMEM((BC_WIDTH, tb), BF16),
        q=pltpu.VMEM((tb, RET_QK_WIDTH), BF16),
        qd=pltpu.VMEM((tb, RET_QK_WIDTH), BF16),
        k=pltpu.VMEM((tb, RET_QK_WIDTH), BF16),
        kdT=pltpu.VMEM((RET_QK_WIDTH, tb), BF16),
        v=pltpu.VMEM((tb, RET_WIDTH), BF16),
        gs=pltpu.VMEM((tb, RET_WIDTH), F32),
        dt=pltpu.VMEM((tb, LANES), F32),
        dtT=pltpu.VMEM((SSM_HEADS, tb), F32),
    ))


def _mixer(x2d, batch, seq, cos, sin, pre, w_main, w_dt_pad, conv_w, conv_b, dt_bias, a_log, d_skip,
           ssm_norm, ret_norm, w_out, post):
    tb = min(MIX_BLOCK, seq)
    nj = seq // tb
    nblocks = batch * nj
    lane_pad = (0, LANES - DT_COPIES * SSM_HEADS)
    w_dt_t = w_dt_pad[:, 0:SSM_HEADS].T
    dtb_row = jnp.pad(jnp.tile(dt_bias, DT_COPIES), lane_pad)[None, :]
    alog_row = jnp.pad(jnp.tile(a_log, DT_COPIES), lane_pad)[None, :]
    dskip = jnp.repeat(d_skip, SSM_HEAD_DIM)[None, :]
    dmat, qdec, kdec, cdec = _retention_tables(tb)
    expand, tril, triu = _constant_matrices()

    g = MIX_GROUP
    assert nj % g == 0, "a grid step's blocks must not straddle two sequences"

    def block(k):
        return lambda i: (jnp.clip(g * i - g + 1 + k, 0, nblocks - 1), 0)

    prev = lambda i: (jnp.maximum(i - 1, 0), 0)
    operands = (
        [(x2d, pl.BlockSpec((tb, D_MODEL), block(k))) for k in range(g)]
        + [(cos, pl.BlockSpec((tb, LANES), block(k))) for k in range(g)]
        + [(sin, pl.BlockSpec((tb, LANES), block(k))) for k in range(g)])
    consts = [
        (pre[None, :], None),
        (w_main, _resident_block(w_main.shape)),
        (w_dt_pad, None),
        (w_dt_t, None),
        (conv_w, None),
        (conv_b[None, :], None),
        (dtb_row, None),
        (dt_bias[:, None], None),
        (alog_row, None),
        (a_log[:, None], None),
        (dskip, None),
        (ssm_norm[None, :], None),
        (ret_norm[None, :], None),
        (w_out, _resident_block(w_out.shape)),
        (post[None, :], None),
        (expand, None),
        (tril, None),
        (triu, None),
        (dmat, None),
        (qdec, None),
        (kdec, None),
        (cdec, None),
    ]
    assert len(consts) == N_MIX_CONSTS
    operands += consts
    args = [a for a, _ in operands]
    in_specs = [sp if sp is not None else _resident(a.shape) for a, sp in operands]
    return pl.pallas_call(
        functools.partial(_mixer_kernel, tb=tb, nj=nj),
        grid=(nblocks // g + 1,),
        in_specs=in_specs,
        out_specs=pl.BlockSpec((g * tb, D_MODEL), prev),
        out_shape=jax.ShapeDtypeStruct(x2d.shape, F32),
        scratch_shapes=[
            pltpu.VMEM((tb, D_MODEL), BF16),
            pltpu.VMEM((SUBLANES + tb, CONV_DIM), F32),
            pltpu.VMEM((CHUNK, SSM_WIDTH), F32),
            pltpu.VMEM((tb, D_INNER), BF16),
            pltpu.VMEM((tb, D_MODEL), F32),
            pltpu.VMEM((SSM_GROUPS, SSM_STATE, GROUP_WIDTH), F32),
            pltpu.VMEM((RET_HEADS, RET_QK_DIM, RET_V_DIM), F32),
        ] + 2 * _slot_scratch(tb),
        compiler_params=pltpu.CompilerParams(
            dimension_semantics=("arbitrary",), vmem_limit_bytes=VMEM_LIMIT),
        name="mixer",
    )(*args)


def kernel(x, positions, ffn1_pre_norm, ffn1_w_in, ffn1_w_out, ffn1_post_norm, mix_pre_norm, mix_w_in, conv_w, conv_b, dt_bias, a_log, d_skip, ssm_norm, ret_norm, mix_w_out, mix_post_norm, ffn2_pre_norm, ffn2_w_in, ffn2_w_out, ffn2_post_norm):
    batch, seq, _ = x.shape
    depth = ffn1_w_in.shape[0]
    cos, sin = _rope_tables(positions)

    def cast_job(w, layer):
        return _PrepJob(w, layer, _cast_kernel, w.shape[2])

    f1_in = _prep_weights(ffn1_w_in, 0, _cast_kernel, 2 * D_FF)
    f1_out = _prep_weights(ffn1_w_out, 0, _cast_kernel, D_MODEL)
    h = x.reshape(batch * seq, D_MODEL)
    for l in range(depth):
        jobs = [_PrepJob(mix_w_in, l, _mix_in_kernel, _G1), _PrepJob(mix_w_in, l, _mix_dt_kernel, LANES),
                cast_job(mix_w_out, l), cast_job(ffn2_w_in, l), cast_job(ffn2_w_out, l)]
        h, m_in, m_dt, m_out, f2_in, f2_out = _ffn(h, ffn1_pre_norm[l], f1_in, f1_out, ffn1_post_norm[l],
                                                   jobs)
        h = _mixer(h, batch, seq, cos, sin, mix_pre_norm[l], m_in, m_dt, conv_w[l], conv_b[l],
                   dt_bias[l], a_log[l], d_skip[l], ssm_norm[l], ret_norm[l], m_out, mix_post_norm[l])
        jobs = [cast_job(ffn1_w_in, l + 1), cast_job(ffn1_w_out, l + 1)] if l + 1 < depth else []
        h, *prepared = _ffn(h, ffn2_pre_norm[l], f2_in, f2_out, ffn2_post_norm[l], jobs)
        if prepared:
            f1_in, f1_out = prepared
    return h.reshape(batch, seq, D_MODEL)
```

```python
import collections
import functools

import jax
import jax.numpy as jnp
import numpy as np
from jax import lax
from jax.experimental import pallas as pl
from jax.experimental.pallas import tpu as pltpu

F32 = jnp.float32
BF16 = jnp.bfloat16

D_MODEL = 1024
D_FF = 2816
NORM_EPS = 1e-6
RET_NORM_EPS = 1e-5

D_INNER = 2 * D_MODEL
SSM_WIDTH = D_INNER // 2
RET_WIDTH = D_INNER - SSM_WIDTH
SSM_HEAD_DIM = 64
SSM_HEADS = SSM_WIDTH // SSM_HEAD_DIM
SSM_GROUPS = 2
HEADS_PER_GROUP = SSM_HEADS // SSM_GROUPS
GROUP_WIDTH = HEADS_PER_GROUP * SSM_HEAD_DIM
SSM_STATE = 128
BC_WIDTH = SSM_GROUPS * SSM_STATE
CONV_K = 4
CONV_DIM = SSM_WIDTH + 2 * BC_WIDTH
RET_HEADS = 4
RET_V_DIM = RET_WIDTH // RET_HEADS
RET_QK_DIM = RET_V_DIM // 2
RET_QK_WIDTH = RET_HEADS * RET_QK_DIM
ROPE_BASE = 10000.0

LANES = 128
SUBLANES = 8
BF16_ROWS = 16

CHUNK = 128
MIX_BLOCK = 256
MIX_GROUP = 2
N_MIX_CONSTS = 22
FFN_BLOCK = 1024
FFN_COL_CHUNK = 256
ROPE_BLOCK = 2048
VMEM_LIMIT = 58 * 1024 * 1024
DT_COPIES = 4
SLAB = 256
PREP_ROWS = 256

_Z0, _Z1 = 0, SSM_WIDTH
_X0, _X1 = _Z1, _Z1 + CONV_DIM
_Q0, _Q1 = _X1, _X1 + RET_QK_WIDTH
_K0, _K1 = _Q1, _Q1 + RET_QK_WIDTH
_V0, _V1 = _K1, _K1 + RET_WIDTH
_G0, _G1 = _V1, _V1 + RET_WIDTH


def _resident(shape):
    del shape
    return pl.BlockSpec(memory_space=pltpu.VMEM)


def _resident_block(shape):
    nd = len(shape)
    return pl.BlockSpec(shape, lambda *_: (0,) * nd, pipeline_mode=pl.Buffered(1))


def _sigmoid(x):
    return 1.0 / (1.0 + jnp.exp(-x))


def _softplus(x):
    return jnp.maximum(x, 0.0) + jnp.log1p(jnp.exp(-jnp.abs(x)))


def _rms(x, w):
    ms = jnp.mean(x * x, axis=-1, keepdims=True)
    return x * lax.rsqrt(ms + NORM_EPS) * w


def _dot(a, b):
    return jnp.dot(a, b, preferred_element_type=F32)


def _dot_nt(a, b):
    return lax.dot_general(a, b, (((1,), (1,)), ((), ())), preferred_element_type=F32)


def _split3(v):
    p1 = v.astype(BF16)
    r1 = v - p1.astype(F32)
    p2 = r1.astype(BF16)
    p3 = (r1 - p2.astype(F32)).astype(BF16)
    return p1, p2, p3


def _cast_kernel(w_ref, o_ref):
    o_ref[...] = w_ref[...].astype(BF16)


def _mix_in_kernel(w_ref, o_ref):
    o_ref[:, 0:_X1] = w_ref[:, 0:_X1].astype(BF16)
    o_ref[:, _X1:_G1] = w_ref[:, _X1 + SSM_HEADS:_G1 + SSM_HEADS].astype(BF16)


def _mix_dt_kernel(w_ref, o_ref):
    lane = lax.broadcasted_iota(jnp.int32, (w_ref.shape[0], LANES), 1)
    d = jnp.where(lane < SSM_HEADS, w_ref[:, _X1:_X1 + LANES], 0.0)
    out = d
    for c in range(1, DT_COPIES):
        out = out + pltpu.roll(d, c * SSM_HEADS, 1)
    o_ref[...] = out.astype(BF16)


def _prep_weights(w, layer, body, out_cols):
    _, k, n = w.shape
    return pl.pallas_call(
        body,
        grid=(k // PREP_ROWS,),
        in_specs=[pl.BlockSpec((None, PREP_ROWS, n), lambda i: (layer, i, 0))],
        out_specs=pl.BlockSpec((PREP_ROWS, out_cols), lambda i: (i, 0)),
        out_shape=jax.ShapeDtypeStruct((k, out_cols), BF16),
        compiler_params=pltpu.CompilerParams(
            dimension_semantics=("arbitrary",), vmem_limit_bytes=VMEM_LIMIT),
        name="prep_weights",
    )(w)


_PrepJob = collections.namedtuple("_PrepJob", ["w", "layer", "body", "out_cols"])


def _job_specs(job, n_steps):
    _, k, n = job.w.shape
    nb = next(c for c in range(min(n_steps, k // BF16_ROWS), 0, -1)
              if k % c == 0 and (k // c) % BF16_ROWS == 0)
    rows = k // nb
    layer = job.layer
    return (pl.BlockSpec((None, rows, n), lambda i: (layer, i * nb // n_steps, 0)),
            pl.BlockSpec((rows, job.out_cols), lambda i: (i * nb // n_steps, 0)),
            jax.ShapeDtypeStruct((k, job.out_cols), BF16))


def _rope_kernel(pos_ref, invf_ref, cos_ref, sin_ref):
    ang = invf_ref[...] * pos_ref[...].astype(F32)
    c = jnp.cos(ang)
    s = jnp.sin(ang)
    cos_ref[...] = jnp.concatenate([c, c], axis=0).T
    sin_ref[...] = jnp.concatenate([-s, s], axis=0).T


def _rope_tables(positions):
    t = positions.size
    tb = min(ROPE_BLOCK, t)
    assert t % tb == 0
    half = RET_QK_DIM // 2
    inv_freq = ROPE_BASE ** (-jnp.arange(half, dtype=F32) / half)
    return pl.pallas_call(
        _rope_kernel,
        grid=(t // tb,),
        in_specs=[pl.BlockSpec((None, 1, tb), lambda i: (i, 0, 0)),
                  pl.BlockSpec((half, 1), lambda i: (0, 0))],
        out_specs=[pl.BlockSpec((tb, LANES), lambda i: (i, 0)),
                   pl.BlockSpec((tb, LANES), lambda i: (i, 0))],
        out_shape=[jax.ShapeDtypeStruct((t, LANES), F32)] * 2,
        name="rope_tables",
    )(positions.reshape(t // tb, 1, tb), inv_freq[:, None])


def _ffn_kernel(*refs, job_bodies):
    n_jobs = len(job_bodies)
    x_ref, pre_ref, win_ref, wout_ref, post_ref = refs[:5]
    job_in = refs[5:5 + n_jobs]
    o_ref = refs[5 + n_jobs]
    job_out = refs[6 + n_jobs:6 + 2 * n_jobs]
    xn_ref, act_ref = refs[6 + 2 * n_jobs:]

    tm = x_ref.shape[0]
    halves = (slice(0, tm // 2), slice(tm // 2, tm))
    slabs = list(range(0, D_FF, FFN_COL_CHUNK))

    def prenorm(r):
        xn_ref[r, :] = _rms(x_ref[r, :], pre_ref[...]).astype(BF16)

    def swiglu(r, lo):
        gate = _dot(xn_ref[r, :], win_ref[:, lo:lo + FFN_COL_CHUNK])
        up = _dot(xn_ref[r, :], win_ref[:, D_FF + lo:D_FF + lo + FFN_COL_CHUNK])
        act_ref[r, lo:lo + FFN_COL_CHUNK] = (gate * _sigmoid(gate) * up).astype(BF16)

    def project(r):
        h = _dot(act_ref[r, :], wout_ref[...])
        o_ref[r, :] = x_ref[r, :] + 0.5 * _rms(h, post_ref[...])

    first, second = halves
    prenorm(first)
    swiglu(first, slabs[0])
    prenorm(second)
    for lo in slabs[1:]:
        swiglu(first, lo)
    for body, w_ref, out_ref in zip(job_bodies, job_in, job_out):
        body(w_ref, out_ref)
    swiglu(second, slabs[0])
    project(first)
    for lo in slabs[1:]:
        swiglu(second, lo)
    project(second)


def _ffn(x2d, pre, w_in, w_out, post, jobs=()):
    t = x2d.shape[0]
    tm = min(FFN_BLOCK, t)
    n_steps = t // tm
    job_specs = [_job_specs(job, n_steps) for job in jobs]
    row = pl.BlockSpec((tm, D_MODEL), lambda i: (i, 0))
    return pl.pallas_call(
        functools.partial(_ffn_kernel, job_bodies=tuple(job.body for job in jobs)),
        grid=(n_steps,),
        in_specs=[row, _resident((1, D_MODEL)), _resident_block(w_in.shape),
                  _resident_block(w_out.shape), _resident((1, D_MODEL))] + [js[0] for js in job_specs],
        out_specs=[row] + [js[1] for js in job_specs],
        out_shape=[jax.ShapeDtypeStruct((t, D_MODEL), F32)] + [js[2] for js in job_specs],
        scratch_shapes=[pltpu.VMEM((tm, D_MODEL), BF16),
                        pltpu.VMEM((tm, D_FF), BF16)],
        compiler_params=pltpu.CompilerParams(
            dimension_semantics=("arbitrary",), vmem_limit_bytes=VMEM_LIMIT),
        name="ffn",
    )(x2d, pre[None, :], w_in, w_out, post[None, :], *[job.w for job in jobs])


_Slot = collections.namedtuple(
    "_Slot", ["x", "zs", "xs", "bc", "bT", "q", "qd", "k", "kdT", "v", "gs", "dt", "dtT"])


def _mixer_kernel(*refs, tb, nj):
    g = MIX_GROUP
    x_refs = refs[0:g]
    cos_refs = refs[g:2 * g]
    sin_refs = refs[2 * g:3 * g]
    consts = refs[3 * g:3 * g + N_MIX_CONSTS]
    o_ref = refs[3 * g + N_MIX_CONSTS]
    scratch = refs[3 * g + N_MIX_CONSTS + 1:]
    u_ref, xbc_ref, y_ref, cat_ref, mixed_ref, sstate_ref, rstate_ref = scratch[:7]
    n = len(_Slot._fields)
    slot_a = _Slot(*scratch[7:7 + n])
    slot_b = _Slot(*scratch[7 + n:7 + 2 * n])
    i = pl.program_id(0)

    @pl.when(i == 0)
    def _():
        for ref in slot_a:
            ref[...] = jnp.zeros(ref.shape, ref.dtype)
        xbc_ref[0:SUBLANES, :] = jnp.zeros((SUBLANES, CONV_DIM), F32)

    @pl.when(jnp.logical_or(i == 0, lax.rem(g * i + nj - g, nj) == 0))
    def _():
        sstate_ref[...] = jnp.zeros(sstate_ref.shape, F32)
        rstate_ref[...] = jnp.zeros(rstate_ref.shape, F32)

    shared = consts + (u_ref, xbc_ref, y_ref, cat_ref, mixed_ref, sstate_ref, rstate_ref)
    for k in range(g):
        p_slot, s_slot = (slot_b, slot_a) if k % 2 == 0 else (slot_a, slot_b)
        new_sequence = (lax.rem(g * i, nj) == 0) if k == g - 1 else None
        _mixer_step(x_refs[k], cos_refs[k], sin_refs[k], o_ref.at[pl.ds(k * tb, tb)], *shared,
                    p_slot, s_slot, new_sequence, tb=tb)


def _mixer_step(xp_ref, cos_ref, sin_ref, o_ref, pre_ref, wmain_ref, wdt_ref, wdtT_ref,
                convw_ref, convb_ref, dtb_row_ref, dtb_col_ref, alog_row_ref, alog_col_ref, dskip_ref,
                ssmnorm_ref, retnorm_ref, wout_ref, post_ref, expand_ref, tril_ref, triu_ref,
                dmat_ref, qdec_ref, kdec_ref, cdec_ref,
                u_ref, xbc_ref, y_ref, cat_ref, mixed_ref, sstate_ref, rstate_ref, P, S, new_sequence,
                *, tb):
    L = CHUNK
    if new_sequence is not None:
        tail = xbc_ref[0:SUBLANES, :]
        xbc_ref[0:SUBLANES, :] = jnp.where(new_sequence, jnp.zeros_like(tail), tail)

    def proj(lo, hi):
        return _dot(u_ref[...], wmain_ref[:, lo:hi])

    def p_norm():
        x = xp_ref[...]
        P.x[...] = x
        u_ref[...] = _rms(x, pre_ref[...]).astype(BF16)

    def p_gate(dst, base, j):
        def piece():
            sl = slice(j * SLAB, (j + 1) * SLAB)
            t = proj(base + j * SLAB, base + (j + 1) * SLAB)
            dst[:, sl] = t * _sigmoid(t)
        return piece

    def p_value(j):
        def piece():
            sl = slice(j * SLAB, (j + 1) * SLAB)
            P.v[:, sl] = proj(_V0 + j * SLAB, _V0 + (j + 1) * SLAB).astype(BF16)
        return piece

    def p_rotary(j):
        def piece():
            q = proj(_Q0 + j * SLAB, _Q0 + (j + 1) * SLAB)
            k = proj(_K0 + j * SLAB, _K0 + (j + 1) * SLAB)
            cos = cos_ref[...]
            sin = sin_ref[...]
            for hh in range(SLAB // RET_QK_DIM):
                loc = slice(hh * RET_QK_DIM, (hh + 1) * RET_QK_DIM)
                sl = slice(j * SLAB + hh * RET_QK_DIM, j * SLAB + (hh + 1) * RET_QK_DIM)
                qh = q[:, loc]
                kh = k[:, loc]
                qr = qh * cos + pltpu.roll(qh, RET_QK_DIM // 2, 1) * sin
                kr = (kh * cos + pltpu.roll(kh, RET_QK_DIM // 2, 1) * sin) * (RET_QK_DIM ** -0.5)
                P.q[:, sl] = qr.astype(BF16)
                P.qd[:, sl] = (qr * qdec_ref[:, sl]).astype(BF16)
                P.k[:, sl] = kr.astype(BF16)
                P.kdT[sl, :] = (kr * kdec_ref[:, sl]).T.astype(BF16)
        return piece

    def p_dt():
        P.dt[...] = _softplus(_dot(u_ref[...], wdt_ref[...]) + dtb_row_ref[...])
        P.dtT[...] = _softplus(_dot_nt(wdtT_ref[...], u_ref[...]) + dtb_col_ref[...])

    def p_conv(j):
        def piece():
            sl = slice(j * SLAB, (j + 1) * SLAB)
            xbc_ref[SUBLANES:SUBLANES + tb, sl] = proj(_X0 + j * SLAB, _X0 + (j + 1) * SLAB)
            conv = convb_ref[:, sl]
            for kk in range(CONV_K):
                off = SUBLANES - (CONV_K - 1) + kk
                conv = conv + convw_ref[kk:kk + 1, sl] * xbc_ref[off:off + tb, sl]
            act = conv * _sigmoid(conv)
            xbc_ref[0:SUBLANES, sl] = xbc_ref[tb:tb + SUBLANES, sl]
            if (j + 1) * SLAB <= SSM_WIDTH:
                P.xs[:, sl] = act
            else:
                lo = j * SLAB - SSM_WIDTH
                P.bc[:, lo:lo + SLAB] = act.astype(BF16)
                if lo < BC_WIDTH:
                    P.bT[lo:lo + SLAB, :] = act.T.astype(BF16)
        return piece

    a_row = -jnp.exp(alog_row_ref[...])
    a_col = -jnp.exp(alog_col_ref[...])
    rows = lax.broadcasted_iota(jnp.int32, (L, L), 0)
    cols = lax.broadcasted_iota(jnp.int32, (L, L), 1)
    causal = rows >= cols
    lane = lax.broadcasted_iota(jnp.int32, (L, LANES), 1)
    low_half = lane < SSM_HEAD_DIM

    def s_chunk_pieces(c):
        r = slice(c * L, (c + 1) * L)
        ctx = {}

        def decays():
            dt_c = S.dt[r, :]
            dtT_c = S.dtT[:, r]
            parts = _dot(tril_ref[...], jnp.concatenate(_split3(dt_c * a_row), axis=1))
            acum = parts[:, 0:LANES] + parts[:, LANES:2 * LANES] + parts[:, 2 * LANES:3 * LANES]
            partsT = _dot(jnp.concatenate(_split3(dtT_c * a_col), axis=0), triu_ref[...])
            acumT = (partsT[0:SSM_HEADS] + partsT[SSM_HEADS:2 * SSM_HEADS]
                     + partsT[2 * SSM_HEADS:3 * SSM_HEADS])
            ctx["acum"] = acum
            ctx["lrow"] = acumT - jnp.log(dtT_c)
            e = jnp.exp(acum)
            w = jnp.exp(acum[L - 1:L, :] - acum) * dt_c
            e_hi = e.astype(BF16).astype(F32)
            w_hi = w.astype(BF16).astype(F32)
            packed = jnp.where(lane < SSM_HEADS, e_hi,
                               jnp.where(lane < 2 * SSM_HEADS, e - e_hi,
                                         jnp.where(lane < 3 * SSM_HEADS, w_hi,
                                                   jnp.where(lane < 4 * SSM_HEADS, w - w_hi, 0.0))))
            ex = _dot(packed.astype(BF16), expand_ref[...])
            ctx["ex_e"] = ex[:, 0:SSM_WIDTH]
            ctx["xw"] = (S.xs[r, :] * ex[:, SSM_WIDTH:2 * SSM_WIDTH]).astype(BF16)

        def group_open(gi):
            def piece():
                gsl = slice(gi * GROUP_WIDTH, (gi + 1) * GROUP_WIDTH)
                b_g = S.bc[r, gi * SSM_STATE:(gi + 1) * SSM_STATE]
                c_g = S.bc[r, BC_WIDTH + gi * SSM_STATE:BC_WIDTH + (gi + 1) * SSM_STATE]
                ctx["cb"] = _dot_nt(c_g, b_g)
                st = sstate_ref[gi]
                ctx["y_inter"] = _dot(c_g, st.astype(BF16)) * ctx["ex_e"][:, gsl]
                sstate_ref[gi] = (st * ctx["ex_e"][L - 1:L, gsl]
                                  + _dot(S.bT[gi * SSM_STATE:(gi + 1) * SSM_STATE, r], ctx["xw"][:, gsl]))
            return piece

        def head_pair(gi, p):
            def piece():
                h0 = gi * HEADS_PER_GROUP + 2 * p
                psl = slice(h0 * SSM_HEAD_DIM, (h0 + 2) * SSM_HEAD_DIM)
                gmats = []
                for h in (h0, h0 + 1):
                    diff = ctx["acum"][:, h:h + 1] - ctx["lrow"][h:h + 1, :]
                    gmats.append(ctx["cb"] * jnp.exp(jnp.where(causal, diff, -jnp.inf)))
                lhs = jnp.concatenate(gmats, axis=1).astype(BF16)
                xp = S.xs[r, psl]
                rhs = jnp.concatenate([jnp.where(low_half, xp, 0.0),
                                       jnp.where(low_half, 0.0, xp)], axis=0).astype(BF16)
                y = (_dot(lhs, rhs) + ctx["y_inter"][:, 2 * p * SSM_HEAD_DIM:(2 * p + 2) * SSM_HEAD_DIM]
                     + xp * dskip_ref[:, psl])
                y_ref[:, psl] = y * S.zs[r, psl]
            return piece

        def ssd_norm():
            cat_ref[r, 0:SSM_WIDTH] = _rms(y_ref[...], ssmnorm_ref[...]).astype(BF16)

        def retention(h):
            def piece():
                qsl = slice(h * RET_QK_DIM, (h + 1) * RET_QK_DIM)
                vsl = slice(h * RET_V_DIM, (h + 1) * RET_V_DIM)
                vh = S.v[r, vsl]
                sc = _dot_nt(S.q[r, qsl], S.k[r, qsl]) * dmat_ref[h]
                st = rstate_ref[h]
                lhs = jnp.concatenate([sc.astype(BF16), S.qd[r, qsl]], axis=1)
                rhs = jnp.concatenate([vh, st.astype(BF16)], axis=0)
                rh = _dot(lhs, rhs)
                rstate_ref[h] = st * cdec_ref[:, vsl] + _dot(S.kdT[qsl, r], vh)
                d = rh - jnp.mean(rh, axis=-1, keepdims=True)
                var = jnp.mean(d * d, axis=-1, keepdims=True)
                rn = d * lax.rsqrt(var + RET_NORM_EPS) * retnorm_ref[:, vsl]
                cat_ref[r, SSM_WIDTH + h * RET_V_DIM:SSM_WIDTH + (h + 1) * RET_V_DIM] = (
                    S.gs[r, vsl] * rn).astype(BF16)
            return piece

        pieces = [decays]
        for gi in range(SSM_GROUPS):
            pieces.append(group_open(gi))
            pieces += [head_pair(gi, p) for p in range(HEADS_PER_GROUP // 2)]
        pieces.append(ssd_norm)
        pieces += [retention(h) for h in range(RET_HEADS)]
        return pieces

    def s_out(j):
        def piece():
            sl = slice(j * SLAB, (j + 1) * SLAB)
            mixed_ref[:, sl] = _dot(cat_ref[...], wout_ref[:, sl])
        return piece

    def s_residual():
        o_ref[...] = S.x[...] + _rms(mixed_ref[...], post_ref[...])

    p_light = ([p_gate(P.zs, _Z0, j) for j in range(SSM_WIDTH // SLAB)]
               + [p_gate(P.gs, _G0, j) for j in range(RET_WIDTH // SLAB)]
               + [p_rotary(j) for j in range(RET_QK_WIDTH // SLAB)] + [p_dt])
    s_scan = [piece for c in range(tb // L) for piece in s_chunk_pieces(c)]
    p_norm()
    n_p, n_s = len(p_light), len(s_scan)
    done = 0
    for i, piece in enumerate(p_light):
        piece()
        upto = (i + 1) * n_s // n_p
        for sp in s_scan[done:upto]:
            sp()
        done = upto
    convs = [p_conv(j) for j in range(CONV_DIM // SLAB)]
    outs = [s_out(j) for j in range(D_MODEL // SLAB)]
    values = [p_value(j) for j in range(RET_WIDTH // SLAB)]
    for out_piece, conv_piece in zip(outs, convs):
        out_piece()
        conv_piece()
    s_residual()
    rest = convs[len(outs):]
    for i, piece in enumerate(values):
        piece()
        if i < len(rest):
            rest[i]()


def _retention_tables(tb):
    log_gamma = jnp.log(1.0 - 2.0 ** (-5.0 - jnp.arange(RET_HEADS, dtype=F32)))
    pos = jnp.arange(CHUNK, dtype=F32)
    rel = pos[:, None] - pos[None, :]
    dmat = jnp.exp(jnp.where((rel >= 0)[None], rel[None] * log_gamma[:, None, None], -jnp.inf))
    q_decay = jnp.exp((pos[:, None] + 1.0) * log_gamma[None])
    k_decay = jnp.exp((CHUNK - 1.0 - pos[:, None]) * log_gamma[None])
    chunk_decay = jnp.exp(CHUNK * log_gamma)
    qdec = jnp.tile(jnp.repeat(q_decay, RET_QK_DIM, axis=1), (tb // CHUNK, 1))
    kdec = jnp.tile(jnp.repeat(k_decay, RET_QK_DIM, axis=1), (tb // CHUNK, 1))
    cdec = jnp.repeat(chunk_decay, RET_V_DIM)[None, :]
    return dmat, qdec, kdec, cdec


def _constant_matrices():
    e = np.zeros((LANES, 2 * SSM_WIDTH), np.float32)
    for g in range(DT_COPIES):
        for h in range(SSM_HEADS):
            c0 = (g // 2) * SSM_WIDTH + h * SSM_HEAD_DIM
            e[g * SSM_HEADS + h, c0:c0 + SSM_HEAD_DIM] = 1.0
    tril = np.tril(np.ones((CHUNK, CHUNK), np.float32))
    return jnp.asarray(e, BF16), jnp.asarray(tril, BF16), jnp.asarray(tril.T, BF16)


def _slot_scratch(tb):
    return list(_Slot(
        x=pltpu.VMEM((tb, D_MODEL), F32),
        zs=pltpu.VMEM((tb, SSM_WIDTH), F32),
        xs=pltpu.VMEM((tb, SSM_WIDTH), F32),
        bc=pltpu.VMEM((tb, 2 * BC_WIDTH), BF16),
        bT=pltpu.VMEM((BC_WIDTH, tb), BF16),
        q=pltpu.VMEM((tb, RET_QK_WIDTH), BF16),
        qd=pltpu.VMEM((tb, RET_QK_WIDTH), BF16),
        k=pltpu.VMEM((tb, RET_QK_WIDTH), BF16),
        kdT=pltpu.VMEM((RET_QK_WIDTH, tb), BF16),
        v=pltpu.VMEM((tb, RET_WIDTH), BF16),
        gs=pltpu.VMEM((tb, RET_WIDTH), F32),
        dt=pltpu.VMEM((tb, LANES), F32),
        dtT=pltpu.VMEM((SSM_HEADS, tb), F32),
    ))


def _mixer(x2d, batch, seq, cos, sin, pre, w_main, w_dt_pad, conv_w, conv_b, dt_bias, a_log, d_skip,
           ssm_norm, ret_norm, w_out, post):
    tb = min(MIX_BLOCK, seq)
    nj = seq // tb
    nblocks = batch * nj
    lane_pad = (0, LANES - DT_COPIES * SSM_HEADS)
    w_dt_t = w_dt_pad[:, 0:SSM_HEADS].T
    dtb_row = jnp.pad(jnp.tile(dt_bias, DT_COPIES), lane_pad)[None, :]
    alog_row = jnp.pad(jnp.tile(a_log, DT_COPIES), lane_pad)[None, :]
    dskip = jnp.repeat(d_skip, SSM_HEAD_DIM)[None, :]
    dmat, qdec, kdec, cdec = _retention_tables(tb)
    expand, tril, triu = _constant_matrices()

    g = MIX_GROUP
    assert nj % g == 0, "a grid step's blocks must not straddle two sequences"

    def block(k):
        return lambda i: (jnp.clip(g * i - g + 1 + k, 0, nblocks - 1), 0)

    prev = lambda i: (jnp.maximum(i - 1, 0), 0)
    operands = (
        [(x2d, pl.BlockSpec((tb, D_MODEL), block(k))) for k in range(g)]
        + [(cos, pl.BlockSpec((tb, LANES), block(k))) for k in range(g)]
        + [(sin, pl.BlockSpec((tb, LANES), block(k))) for k in range(g)])
    consts = [
        (pre[None, :], None),
        (w_main, _resident_block(w_main.shape)),
        (w_dt_pad, None),
        (w_dt_t, None),
        (conv_w, None),
        (conv_b[None, :], None),
        (dtb_row, None),
        (dt_bias[:, None], None),
        (alog_row, None),
        (a_log[:, None], None),
        (dskip, None),
        (ssm_norm[None, :], None),
        (ret_norm[None, :], None),
        (w_out, _resident_block(w_out.shape)),
        (post[None, :], None),
        (expand, None),
        (tril, None),
        (triu, None),
        (dmat, None),
        (qdec, None),
        (kdec, None),
        (cdec, None),
    ]
    assert len(consts) == N_MIX_CONSTS
    operands += consts
    args = [a for a, _ in operands]
    in_specs = [sp if sp is not None else _resident(a.shape) for a, sp in operands]
    return pl.pallas_call(
        functools.partial(_mixer_kernel, tb=tb, nj=nj),
        grid=(nblocks // g + 1,),
        in_specs=in_specs,
        out_specs=pl.BlockSpec((g * tb, D_MODEL), prev),
        out_shape=jax.ShapeDtypeStruct(x2d.shape, F32),
        scratch_shapes=[
            pltpu.VMEM((tb, D_MODEL), BF16),
            pltpu.VMEM((SUBLANES + tb, CONV_DIM), F32),
            pltpu.VMEM((CHUNK, SSM_WIDTH), F32),
            pltpu.VMEM((tb, D_INNER), BF16),
            pltpu.VMEM((tb, D_MODEL), F32),
            pltpu.VMEM((SSM_GROUPS, SSM_STATE, GROUP_WIDTH), F32),
            pltpu.VMEM((RET_HEADS, RET_QK_DIM, RET_V_DIM), F32),
        ] + 2 * _slot_scratch(tb),
        compiler_params=pltpu.CompilerParams(
            dimension_semantics=("arbitrary",), vmem_limit_bytes=VMEM_LIMIT),
        name="mixer",
    )(*args)


def kernel(x, positions, ffn1_pre_norm, ffn1_w_in, ffn1_w_out, ffn1_post_norm, mix_pre_norm, mix_w_in, conv_w, conv_b, dt_bias, a_log, d_skip, ssm_norm, ret_norm, mix_w_out, mix_post_norm, ffn2_pre_norm, ffn2_w_in, ffn2_w_out, ffn2_post_norm):
    batch, seq, _ = x.shape
    depth = ffn1_w_in.shape[0]
    cos, sin = _rope_tables(positions)

    def cast_job(w, layer):
        return _PrepJob(w, layer, _cast_kernel, w.shape[2])

    f1_in = _prep_weights(ffn1_w_in, 0, _cast_kernel, 2 * D_FF)
    f1_out = _prep_weights(ffn1_w_out, 0, _cast_kernel, D_MODEL)
    h = x.reshape(batch * seq, D_MODEL)
    for l in range(depth):
        jobs = [_PrepJob(mix_w_in, l, _mix_in_kernel, _G1), _PrepJob(mix_w_in, l, _mix_dt_kernel, LANES),
                cast_job(mix_w_out, l), cast_job(ffn2_w_in, l), cast_job(ffn2_w_out, l)]
        h, m_in, m_dt, m_out, f2_in, f2_out = _ffn(h, ffn1_pre_norm[l], f1_in, f1_out, ffn1_post_norm[l],
                                                   jobs)
        h = _mixer(h, batch, seq, cos, sin, mix_pre_norm[l], m_in, m_dt, conv_w[l], conv_b[l],
                   dt_bias[l], a_log[l], d_skip[l], ssm_norm[l], ret_norm[l], m_out, mix_post_norm[l])
        jobs = [cast_job(ffn1_w_in, l + 1), cast_job(ffn1_w_out, l + 1)] if l + 1 < depth else []
        h, *prepared = _ffn(h, ffn2_pre_norm[l], f2_in, f2_out, ffn2_post_norm[l], jobs)
        if prepared:
            f1_in, f1_out = prepared
    return h.reshape(batch, seq, D_MODEL)
```

```python
import collections
import functools

import jax
import jax.numpy as jnp
import numpy as np
from jax import lax
from jax.experimental import pallas as pl
from jax.experimental.pallas import tpu as pltpu

F32 = jnp.float32
BF16 = jnp.bfloat16

D_MODEL = 1024
D_FF = 2816
NORM_EPS = 1e-6
RET_NORM_EPS = 1e-5

D_INNER = 2 * D_MODEL
SSM_WIDTH = D_INNER // 2
RET_WIDTH = D_INNER - SSM_WIDTH
SSM_HEAD_DIM = 64
SSM_HEADS = SSM_WIDTH // SSM_HEAD_DIM
SSM_GROUPS = 2
HEADS_PER_GROUP = SSM_HEADS // SSM_GROUPS
GROUP_WIDTH = HEADS_PER_GROUP * SSM_HEAD_DIM
SSM_STATE = 128
BC_WIDTH = SSM_GROUPS * SSM_STATE
CONV_K = 4
CONV_DIM = SSM_WIDTH + 2 * BC_WIDTH
RET_HEADS = 4
RET_V_DIM = RET_WIDTH // RET_HEADS
RET_QK_DIM = RET_V_DIM // 2
RET_QK_WIDTH = RET_HEADS * RET_QK_DIM
ROPE_BASE = 10000.0

LANES = 128
SUBLANES = 8
BF16_ROWS = 16

CHUNK = 128
MIX_BLOCK = 256
MIX_GROUP = 2
N_MIX_CONSTS = 22
FFN_BLOCK = 1024
FFN_COL_CHUNK = 256
FFN_ROW_GROUPS = 4
ROPE_BLOCK = 2048
VMEM_LIMIT = 58 * 1024 * 1024
DT_COPIES = 4
SLAB = 256

_Z0, _Z1 = 0, SSM_WIDTH
_X0, _X1 = _Z1, _Z1 + CONV_DIM
_Q0, _Q1 = _X1, _X1 + RET_QK_WIDTH
_K0, _K1 = _Q1, _Q1 + RET_QK_WIDTH
_V0, _V1 = _K1, _K1 + RET_WIDTH
_G0, _G1 = _V1, _V1 + RET_WIDTH


def _resident(shape):
    del shape
    return pl.BlockSpec(memory_space=pltpu.VMEM)


def _resident_block(shape):
    nd = len(shape)
    return pl.BlockSpec(shape, lambda *_: (0,) * nd, pipeline_mode=pl.Buffered(1))


def _sigmoid(x):
    return 1.0 / (1.0 + jnp.exp(-x))


def _softplus(x):
    return jnp.maximum(x, 0.0) + jnp.log1p(jnp.exp(-jnp.abs(x)))


def _rms(x, w):
    ms = jnp.mean(x * x, axis=-1, keepdims=True)
    return x * lax.rsqrt(ms + NORM_EPS) * w


def _dot(a, b):
    return jnp.dot(a, b, preferred_element_type=F32)


def _dot_nt(a, b):
    return lax.dot_general(a, b, (((1,), (1,)), ((), ())), preferred_element_type=F32)


def _split3(v):
    p1 = v.astype(BF16)
    r1 = v - p1.astype(F32)
    p2 = r1.astype(BF16)
    p3 = (r1 - p2.astype(F32)).astype(BF16)
    return p1, p2, p3


def _cast_kernel(w_ref, o_ref):
    o_ref[...] = w_ref[...].astype(BF16)


def _mix_in_kernel(w_ref, o_ref):
    o_ref[:, 0:_X1] = w_ref[:, 0:_X1].astype(BF16)
    o_ref[:, _X1:_G1] = w_ref[:, _X1 + SSM_HEADS:_G1 + SSM_HEADS].astype(BF16)


def _mix_dt_kernel(w_ref, o_ref):
    lane = lax.broadcasted_iota(jnp.int32, (w_ref.shape[0], LANES), 1)
    d = jnp.where(lane < SSM_HEADS, w_ref[:, _X1:_X1 + LANES], 0.0)
    out = d
    for c in range(1, DT_COPIES):
        out = out + pltpu.roll(d, c * SSM_HEADS, 1)
    o_ref[...] = out.astype(BF16)


_PrepJob = collections.namedtuple("_PrepJob", ["w", "layer", "body", "out_cols"])


def _job_specs(job, n_steps):
    _, k, n = job.w.shape
    nb = next(c for c in range(min(n_steps, k // BF16_ROWS), 0, -1)
              if k % c == 0 and (k // c) % BF16_ROWS == 0)
    rows = k // nb
    layer = job.layer
    return (pl.BlockSpec((None, rows, n), lambda i: (layer, i * nb // n_steps, 0)),
            pl.BlockSpec((rows, job.out_cols), lambda i: (i * nb // n_steps, 0)),
            jax.ShapeDtypeStruct((k, job.out_cols), BF16))


def _rope_kernel(*refs, job_bodies):
    n_jobs = len(job_bodies)
    pos_ref, invf_ref = refs[:2]
    job_in = refs[2:2 + n_jobs]
    cos_ref, sin_ref = refs[2 + n_jobs:4 + n_jobs]
    job_out = refs[4 + n_jobs:]
    ang = invf_ref[...] * pos_ref[...].astype(F32)
    c = jnp.cos(ang)
    s = jnp.sin(ang)
    cos_ref[...] = jnp.concatenate([c, c], axis=0).T
    sin_ref[...] = jnp.concatenate([-s, s], axis=0).T
    for body, w_ref, out_ref in zip(job_bodies, job_in, job_out):
        body(w_ref, out_ref)


def _rope_tables(positions, jobs=()):
    t = positions.size
    tb = min(ROPE_BLOCK, t)
    assert t % tb == 0
    n_steps = t // tb
    half = RET_QK_DIM // 2
    inv_freq = ROPE_BASE ** (-jnp.arange(half, dtype=F32) / half)
    job_specs = [_job_specs(job, n_steps) for job in jobs]
    table = pl.BlockSpec((tb, LANES), lambda i: (i, 0))
    return pl.pallas_call(
        functools.partial(_rope_kernel, job_bodies=tuple(job.body for job in jobs)),
        grid=(n_steps,),
        in_specs=[pl.BlockSpec((None, 1, tb), lambda i: (i, 0, 0)),
                  pl.BlockSpec((half, 1), lambda i: (0, 0))] + [js[0] for js in job_specs],
        out_specs=[table, table] + [js[1] for js in job_specs],
        out_shape=[jax.ShapeDtypeStruct((t, LANES), F32)] * 2 + [js[2] for js in job_specs],
        compiler_params=pltpu.CompilerParams(
            dimension_semantics=("arbitrary",), vmem_limit_bytes=VMEM_LIMIT),
        name="rope_tables",
    )(positions.reshape(n_steps, 1, tb), inv_freq[:, None], *[job.w for job in jobs])


def _ffn_kernel(*refs, job_bodies):
    n_jobs = len(job_bodies)
    x_ref, pre_ref, win_ref, wout_ref, post_ref = refs[:5]
    job_in = refs[5:5 + n_jobs]
    o_ref = refs[5 + n_jobs]
    job_out = refs[6 + n_jobs:6 + 2 * n_jobs]
    xn_ref, act_ref = refs[6 + 2 * n_jobs:]

    tm = x_ref.shape[0]
    rows = tm // FFN_ROW_GROUPS
    groups = [slice(i * rows, (i + 1) * rows) for i in range(FFN_ROW_GROUPS)]
    slabs = list(range(0, D_FF, FFN_COL_CHUNK))

    def prenorm(r):
        xn_ref[r, :] = _rms(x_ref[r, :], pre_ref[...]).astype(BF16)

    def swiglu(r, lo):
        gate = _dot(xn_ref[r, :], win_ref[:, lo:lo + FFN_COL_CHUNK])
        up = _dot(xn_ref[r, :], win_ref[:, D_FF + lo:D_FF + lo + FFN_COL_CHUNK])
        act_ref[r, lo:lo + FFN_COL_CHUNK] = (gate * _sigmoid(gate) * up).astype(BF16)

    def project(r):
        h = _dot(act_ref[r, :], wout_ref[...])
        o_ref[r, :] = x_ref[r, :] + 0.5 * _rms(h, post_ref[...])

    prenorm(groups[0])
    for i, r in enumerate(groups):
        swiglu(r, slabs[0])
        if i + 1 < len(groups):
            prenorm(groups[i + 1])
        if i > 0:
            project(groups[i - 1])
        else:
            for body, w_ref, out_ref in zip(job_bodies, job_in, job_out):
                body(w_ref, out_ref)
        for lo in slabs[1:]:
            swiglu(r, lo)
    project(groups[-1])


def _ffn(x2d, pre, w_in, w_out, post, jobs=()):
    t = x2d.shape[0]
    tm = min(FFN_BLOCK, t)
    n_steps = t // tm
    job_specs = [_job_specs(job, n_steps) for job in jobs]
    row = pl.BlockSpec((tm, D_MODEL), lambda i: (i, 0))
    return pl.pallas_call(
        functools.partial(_ffn_kernel, job_bodies=tuple(job.body for job in jobs)),
        grid=(n_steps,),
        in_specs=[row, _resident((1, D_MODEL)), _resident_block(w_in.shape),
                  _resident_block(w_out.shape), _resident((1, D_MODEL))] + [js[0] for js in job_specs],
        out_specs=[row] + [js[1] for js in job_specs],
        out_shape=[jax.ShapeDtypeStruct((t, D_MODEL), F32)] + [js[2] for js in job_specs],
        scratch_shapes=[pltpu.VMEM((tm, D_MODEL), BF16),
                        pltpu.VMEM((tm, D_FF), BF16)],
        compiler_params=pltpu.CompilerParams(
            dimension_semantics=("arbitrary",), vmem_limit_bytes=VMEM_LIMIT),
        name="ffn",
    )(x2d, pre[None, :], w_in, w_out, post[None, :], *[job.w for job in jobs])


_Slot = collections.namedtuple(
    "_Slot", ["x", "zs", "xs", "bc", "bT", "q", "qd", "k", "kdT", "v", "gs", "dt", "dtT"])


def _mixer_kernel(*refs, tb, nj):
    g = MIX_GROUP
    x_refs = refs[0:g]
    cos_refs = refs[g:2 * g]
    sin_refs = refs[2 * g:3 * g]
    consts = refs[3 * g:3 * g + N_MIX_CONSTS]
    o_ref = refs[3 * g + N_MIX_CONSTS]
    scratch = refs[3 * g + N_MIX_CONSTS + 1:]
    u_ref, xbc_ref, y_ref, cat_ref, mixed_ref, sstate_ref, rstate_ref = scratch[:7]
    n = len(_Slot._fields)
    slot_a = _Slot(*scratch[7:7 + n])
    slot_b = _Slot(*scratch[7 + n:7 + 2 * n])
    i = pl.program_id(0)

    @pl.when(i == 0)
    def _():
        for ref in slot_a:
            ref[...] = jnp.zeros(ref.shape, ref.dtype)
        xbc_ref[0:SUBLANES, :] = jnp.zeros((SUBLANES, CONV_DIM), F32)

    @pl.when(jnp.logical_or(i == 0, lax.rem(g * i + nj - g, nj) == 0))
    def _():
        sstate_ref[...] = jnp.zeros(sstate_ref.shape, F32)
        rstate_ref[...] = jnp.zeros(rstate_ref.shape, F32)

    shared = consts + (u_ref, xbc_ref, y_ref, cat_ref, mixed_ref, sstate_ref, rstate_ref)
    for k in range(g):
        p_slot, s_slot = (slot_b, slot_a) if k % 2 == 0 else (slot_a, slot_b)
        new_sequence = (lax.rem(g * i, nj) == 0) if k == g - 1 else None
        _mixer_step(x_refs[k], cos_refs[k], sin_refs[k], o_ref.at[pl.ds(k * tb, tb)], *shared,
                    p_slot, s_slot, new_sequence, tb=tb)


def _mixer_step(xp_ref, cos_ref, sin_ref, o_ref, pre_ref, wmain_ref, wdt_ref, wdtT_ref,
                convw_ref, convb_ref, dtb_row_ref, dtb_col_ref, alog_row_ref, alog_col_ref, dskip_ref,
                ssmnorm_ref, retnorm_ref, wout_ref, post_ref, expand_ref, tril_ref, triu_ref,
                dmat_ref, qdec_ref, kdec_ref, cdec_ref,
                u_ref, xbc_ref, y_ref, cat_ref, mixed_ref, sstate_ref, rstate_ref, P, S, new_sequence,
                *, tb):
    L = CHUNK
    if new_sequence is not None:
        tail = xbc_ref[0:SUBLANES, :]
        xbc_ref[0:SUBLANES, :] = jnp.where(new_sequence, jnp.zeros_like(tail), tail)

    def proj(lo, hi):
        return _dot(u_ref[...], wmain_ref[:, lo:hi])

    def p_norm():
        x = xp_ref[...]
        P.x[...] = x
        u_ref[...] = _rms(x, pre_ref[...]).astype(BF16)

    def p_gate(dst, base, j):
        def piece():
            sl = slice(j * SLAB, (j + 1) * SLAB)
            t = proj(base + j * SLAB, base + (j + 1) * SLAB)
            dst[:, sl] = t * _sigmoid(t)
        return piece

    def p_value(j):
        def piece():
            sl = slice(j * SLAB, (j + 1) * SLAB)
            P.v[:, sl] = proj(_V0 + j * SLAB, _V0 + (j + 1) * SLAB).astype(BF16)
        return piece

    def p_rotary(j):
        def piece():
            q = proj(_Q0 + j * SLAB, _Q0 + (j + 1) * SLAB)
            k = proj(_K0 + j * SLAB, _K0 + (j + 1) * SLAB)
            cos = cos_ref[...]
            sin = sin_ref[...]
            for hh in range(SLAB // RET_QK_DIM):
                loc = slice(hh * RET_QK_DIM, (hh + 1) * RET_QK_DIM)
                sl = slice(j * SLAB + hh * RET_QK_DIM, j * SLAB + (hh + 1) * RET_QK_DIM)
                qh = q[:, loc]
                kh = k[:, loc]
                qr = qh * cos + pltpu.roll(qh, RET_QK_DIM // 2, 1) * sin
                kr = (kh * cos + pltpu.roll(kh, RET_QK_DIM // 2, 1) * sin) * (RET_QK_DIM ** -0.5)
                P.q[:, sl] = qr.astype(BF16)
                P.qd[:, sl] = (qr * qdec_ref[:, sl]).astype(BF16)
                P.k[:, sl] = kr.astype(BF16)
                P.kdT[sl, :] = (kr * kdec_ref[:, sl]).T.astype(BF16)
        return piece

    def p_dt():
        P.dt[...] = _softplus(_dot(u_ref[...], wdt_ref[...]) + dtb_row_ref[...])
        P.dtT[...] = _softplus(_dot_nt(wdtT_ref[...], u_ref[...]) + dtb_col_ref[...])

    def p_conv(j):
        def piece():
            sl = slice(j * SLAB, (j + 1) * SLAB)
            xbc_ref[SUBLANES:SUBLANES + tb, sl] = proj(_X0 + j * SLAB, _X0 + (j + 1) * SLAB)
            conv = convb_ref[:, sl]
            for kk in range(CONV_K):
                off = SUBLANES - (CONV_K - 1) + kk
                conv = conv + convw_ref[kk:kk + 1, sl] * xbc_ref[off:off + tb, sl]
            act = conv * _sigmoid(conv)
            xbc_ref[0:SUBLANES, sl] = xbc_ref[tb:tb + SUBLANES, sl]
            if (j + 1) * SLAB <= SSM_WIDTH:
                P.xs[:, sl] = act
            else:
                lo = j * SLAB - SSM_WIDTH
                P.bc[:, lo:lo + SLAB] = act.astype(BF16)
                if lo < BC_WIDTH:
                    P.bT[lo:lo + SLAB, :] = act.T.astype(BF16)
        return piece

    a_row = -jnp.exp(alog_row_ref[...])
    a_col = -jnp.exp(alog_col_ref[...])
    rows = lax.broadcasted_iota(jnp.int32, (L, L), 0)
    cols = lax.broadcasted_iota(jnp.int32, (L, L), 1)
    causal = rows >= cols
    lane = lax.broadcasted_iota(jnp.int32, (L, LANES), 1)
    low_half = lane < SSM_HEAD_DIM

    def s_chunk_pieces(c):
        r = slice(c * L, (c + 1) * L)
        ctx = {}

        def decays():
            dt_c = S.dt[r, :]
            dtT_c = S.dtT[:, r]
            parts = _dot(tril_ref[...], jnp.concatenate(_split3(dt_c * a_row), axis=1))
            acum = parts[:, 0:LANES] + parts[:, LANES:2 * LANES] + parts[:, 2 * LANES:3 * LANES]
            partsT = _dot(jnp.concatenate(_split3(dtT_c * a_col), axis=0), triu_ref[...])
            acumT = (partsT[0:SSM_HEADS] + partsT[SSM_HEADS:2 * SSM_HEADS]
                     + partsT[2 * SSM_HEADS:3 * SSM_HEADS])
            ctx["acum"] = acum
            ctx["lrow"] = acumT - jnp.log(dtT_c)
            e = jnp.exp(acum)
            w = jnp.exp(acum[L - 1:L, :] - acum) * dt_c
            e_hi = e.astype(BF16).astype(F32)
            w_hi = w.astype(BF16).astype(F32)
            packed = jnp.where(lane < SSM_HEADS, e_hi,
                               jnp.where(lane < 2 * SSM_HEADS, e - e_hi,
                                         jnp.where(lane < 3 * SSM_HEADS, w_hi,
                                                   jnp.where(lane < 4 * SSM_HEADS, w - w_hi, 0.0))))
            ex = _dot(packed.astype(BF16), expand_ref[...])
            ctx["ex_e"] = ex[:, 0:SSM_WIDTH]
            ctx["xw"] = (S.xs[r, :] * ex[:, SSM_WIDTH:2 * SSM_WIDTH]).astype(BF16)

        def group_open(gi):
            def piece():
                gsl = slice(gi * GROUP_WIDTH, (gi + 1) * GROUP_WIDTH)
                b_g = S.bc[r, gi * SSM_STATE:(gi + 1) * SSM_STATE]
                c_g = S.bc[r, BC_WIDTH + gi * SSM_STATE:BC_WIDTH + (gi + 1) * SSM_STATE]
                ctx["cb"] = _dot_nt(c_g, b_g)
                st = sstate_ref[gi]
                ctx["y_inter"] = _dot(c_g, st.astype(BF16)) * ctx["ex_e"][:, gsl]
                sstate_ref[gi] = (st * ctx["ex_e"][L - 1:L, gsl]
                                  + _dot(S.bT[gi * SSM_STATE:(gi + 1) * SSM_STATE, r], ctx["xw"][:, gsl]))
            return piece

        def head_pair(gi, p):
            def piece():
                h0 = gi * HEADS_PER_GROUP + 2 * p
                psl = slice(h0 * SSM_HEAD_DIM, (h0 + 2) * SSM_HEAD_DIM)
                gmats = []
                for h in (h0, h0 + 1):
                    diff = ctx["acum"][:, h:h + 1] - ctx["lrow"][h:h + 1, :]
                    gmats.append(ctx["cb"] * jnp.exp(jnp.where(causal, diff, -jnp.inf)))
                lhs = jnp.concatenate(gmats, axis=1).astype(BF16)
                xp = S.xs[r, psl]
                rhs = jnp.concatenate([jnp.where(low_half, xp, 0.0),
                                       jnp.where(low_half, 0.0, xp)], axis=0).astype(BF16)
                y = (_dot(lhs, rhs) + ctx["y_inter"][:, 2 * p * SSM_HEAD_DIM:(2 * p + 2) * SSM_HEAD_DIM]
                     + xp * dskip_ref[:, psl])
                y_ref[:, psl] = y * S.zs[r, psl]
            return piece

        def ssd_norm():
            cat_ref[r, 0:SSM_WIDTH] = _rms(y_ref[...], ssmnorm_ref[...]).astype(BF16)

        def retention(h):
            def piece():
                qsl = slice(h * RET_QK_DIM, (h + 1) * RET_QK_DIM)
                vsl = slice(h * RET_V_DIM, (h + 1) * RET_V_DIM)
                vh = S.v[r, vsl]
                sc = _dot_nt(S.q[r, qsl], S.k[r, qsl]) * dmat_ref[h]
                st = rstate_ref[h]
                lhs = jnp.concatenate([sc.astype(BF16), S.qd[r, qsl]], axis=1)
                rhs = jnp.concatenate([vh, st.astype(BF16)], axis=0)
                rh = _dot(lhs, rhs)
                rstate_ref[h] = st * cdec_ref[:, vsl] + _dot(S.kdT[qsl, r], vh)
                d = rh - jnp.mean(rh, axis=-1, keepdims=True)
                var = jnp.mean(d * d, axis=-1, keepdims=True)
                rn = d * lax.rsqrt(var + RET_NORM_EPS) * retnorm_ref[:, vsl]
                cat_ref[r, SSM_WIDTH + h * RET_V_DIM:SSM_WIDTH + (h + 1) * RET_V_DIM] = (
                    S.gs[r, vsl] * rn).astype(BF16)
            return piece

        pieces = [decays]
        for gi in range(SSM_GROUPS):
            pieces.append(group_open(gi))
            pieces += [head_pair(gi, p) for p in range(HEADS_PER_GROUP // 2)]
        pieces.append(ssd_norm)
        pieces += [retention(h) for h in range(RET_HEADS)]
        return pieces

    def s_out(j):
        def piece():
            sl = slice(j * SLAB, (j + 1) * SLAB)
            mixed_ref[:, sl] = _dot(cat_ref[...], wout_ref[:, sl])
        return piece

    def s_residual():
        o_ref[...] = S.x[...] + _rms(mixed_ref[...], post_ref[...])

    p_light = ([p_gate(P.zs, _Z0, j) for j in range(SSM_WIDTH // SLAB)]
               + [p_gate(P.gs, _G0, j) for j in range(RET_WIDTH // SLAB)]
               + [p_rotary(j) for j in range(RET_QK_WIDTH // SLAB)] + [p_dt])
    s_scan = [piece for c in range(tb // L) for piece in s_chunk_pieces(c)]
    p_norm()
    n_p, n_s = len(p_light), len(s_scan)
    done = 0
    for i, piece in enumerate(p_light):
        piece()
        upto = (i + 1) * n_s // n_p
        for sp in s_scan[done:upto]:
            sp()
        done = upto
    convs = [p_conv(j) for j in range(CONV_DIM // SLAB)]
    outs = [s_out(j) for j in range(D_MODEL // SLAB)]
    values = [p_value(j) for j in range(RET_WIDTH // SLAB)]
    for out_piece, conv_piece in zip(outs, convs):
        out_piece()
        conv_piece()
    s_residual()
    rest = convs[len(outs):]
    for i, piece in enumerate(values):
        piece()
        if i < len(rest):
            rest[i]()


def _retention_tables(tb):
    log_gamma = jnp.log(1.0 - 2.0 ** (-5.0 - jnp.arange(RET_HEADS, dtype=F32)))
    pos = jnp.arange(CHUNK, dtype=F32)
    rel = pos[:, None] - pos[None, :]
    dmat = jnp.exp(jnp.where((rel >= 0)[None], rel[None] * log_gamma[:, None, None], -jnp.inf))
    q_decay = jnp.exp((pos[:, None] + 1.0) * log_gamma[None])
    k_decay = jnp.exp((CHUNK - 1.0 - pos[:, None]) * log_gamma[None])
    chunk_decay = jnp.exp(CHUNK * log_gamma)
    qdec = jnp.tile(jnp.repeat(q_decay, RET_QK_DIM, axis=1), (tb // CHUNK, 1))
    kdec = jnp.tile(jnp.repeat(k_decay, RET_QK_DIM, axis=1), (tb // CHUNK, 1))
    cdec = jnp.repeat(chunk_decay, RET_V_DIM)[None, :]
    return dmat, qdec, kdec, cdec


def _constant_matrices():
    e = np.zeros((LANES, 2 * SSM_WIDTH), np.float32)
    for g in range(DT_COPIES):
        for h in range(SSM_HEADS):
            c0 = (g // 2) * SSM_WIDTH + h * SSM_HEAD_DIM
            e[g * SSM_HEADS + h, c0:c0 + SSM_HEAD_DIM] = 1.0
    tril = np.tril(np.ones((CHUNK, CHUNK), np.float32))
    return jnp.asarray(e, BF16), jnp.asarray(tril, BF16), jnp.asarray(tril.T, BF16)


def _slot_scratch(tb):
    return list(_Slot(
        x=pltpu.VMEM((tb, D_MODEL), F32),
        zs=pltpu.VMEM((tb, SSM_WIDTH), F32),
        xs=pltpu.VMEM((tb, SSM_WIDTH), F32),
        bc=pltpu.VMEM((tb, 2 * BC_WIDTH), BF16),
        bT=pltpu.VMEM((BC_WIDTH, tb), BF16),
        q=pltpu.VMEM((tb, RET_QK_WIDTH), BF16),
        qd=pltpu.VMEM((tb, RET_QK_WIDTH), BF16),
        k=pltpu.VMEM((tb, RET_QK_WIDTH), BF16),
        kdT=pltpu.VMEM((RET_QK_WIDTH, tb), BF16),
        v=pltpu.VMEM((tb, RET_WIDTH), BF16),
        gs=pltpu.VMEM((tb, RET_WIDTH), F32),
        dt=pltpu.VMEM((tb, LANES), F32),
        dtT=pltpu.VMEM((SSM_HEADS, tb), F32),
    ))


def _mixer(x2d, batch, seq, cos, sin, pre, w_main, w_dt_pad, conv_w, conv_b, dt_bias, a_log, d_skip,
           ssm_norm, ret_norm, w_out, post):
    tb = min(MIX_BLOCK, seq)
    nj = seq // tb
    nblocks = batch * nj
    lane_pad = (0, LANES - DT_COPIES * SSM_HEADS)
    w_dt_t = w_dt_pad[:, 0:SSM_HEADS].T
    dtb_row = jnp.pad(jnp.tile(dt_bias, DT_COPIES), lane_pad)[None, :]
    alog_row = jnp.pad(jnp.tile(a_log, DT_COPIES), lane_pad)[None, :]
    dskip = jnp.repeat(d_skip, SSM_HEAD_DIM)[None, :]
    dmat, qdec, kdec, cdec = _retention_tables(tb)
    expand, tril, triu = _constant_matrices()

    g = MIX_GROUP
    assert nj % g == 0, "a grid step's blocks must not straddle two sequences"

    def block(k):
        return lambda i: (jnp.clip(g * i - g + 1 + k, 0, nblocks - 1), 0)

    prev = lambda i: (jnp.maximum(i - 1, 0), 0)
    operands = (
        [(x2d, pl.BlockSpec((tb, D_MODEL), block(k))) for k in range(g)]
        + [(cos, pl.BlockSpec((tb, LANES), block(k))) for k in range(g)]
        + [(sin, pl.BlockSpec((tb, LANES), block(k))) for k in range(g)])
    consts = [
        (pre[None, :], None),
        (w_main, _resident_block(w_main.shape)),
        (w_dt_pad, None),
        (w_dt_t, None),
        (conv_w, None),
        (conv_b[None, :], None),
        (dtb_row, None),
        (dt_bias[:, None], None),
        (alog_row, None),
        (a_log[:, None], None),
        (dskip, None),
        (ssm_norm[None, :], None),
        (ret_norm[None, :], None),
        (w_out, _resident_block(w_out.shape)),
        (post[None, :], None),
        (expand, None),
        (tril, None),
        (triu, None),
        (dmat, None),
        (qdec, None),
        (kdec, None),
        (cdec, None),
    ]
    assert len(consts) == N_MIX_CONSTS
    operands += consts
    args = [a for a, _ in operands]
    in_specs = [sp if sp is not None else _resident(a.shape) for a, sp in operands]
    return pl.pallas_call(
        functools.partial(_mixer_kernel, tb=tb, nj=nj),
        grid=(nblocks // g + 1,),
        in_specs=in_specs,
        out_specs=pl.BlockSpec((g * tb, D_MODEL), prev),
        out_shape=jax.ShapeDtypeStruct(x2d.shape, F32),
        scratch_shapes=[
            pltpu.VMEM((tb, D_MODEL), BF16),
            pltpu.VMEM((SUBLANES + tb, CONV_DIM), F32),
            pltpu.VMEM((CHUNK, SSM_WIDTH), F32),
            pltpu.VMEM((tb, D_INNER), BF16),
            pltpu.VMEM((tb, D_MODEL), F32),
            pltpu.VMEM((SSM_GROUPS, SSM_STATE, GROUP_WIDTH), F32),
            pltpu.VMEM((RET_HEADS, RET_QK_DIM, RET_V_DIM), F32),
        ] + 2 * _slot_scratch(tb),
        compiler_params=pltpu.CompilerParams(
            dimension_semantics=("arbitrary",), vmem_limit_bytes=VMEM_LIMIT),
        name="mixer",
    )(*args)


def kernel(x, positions, ffn1_pre_norm, ffn1_w_in, ffn1_w_out, ffn1_post_norm, mix_pre_norm, mix_w_in, conv_w, conv_b, dt_bias, a_log, d_skip, ssm_norm, ret_norm, mix_w_out, mix_post_norm, ffn2_pre_norm, ffn2_w_in, ffn2_w_out, ffn2_post_norm):
    batch, seq, _ = x.shape
    depth = ffn1_w_in.shape[0]

    def cast_job(w, layer):
        return _PrepJob(w, layer, _cast_kernel, w.shape[2])

    cos, sin, f1_in, f1_out = _rope_tables(positions, [cast_job(ffn1_w_in, 0), cast_job(ffn1_w_out, 0)])
    h = x.reshape(batch * seq, D_MODEL)
    for l in range(depth):
        jobs = [_PrepJob(mix_w_in, l, _mix_in_kernel, _G1), _PrepJob(mix_w_in, l, _mix_dt_kernel, LANES),
                cast_job(mix_w_out, l), cast_job(ffn2_w_in, l), cast_job(ffn2_w_out, l)]
        h, m_in, m_dt, m_out, f2_in, f2_out = _ffn(h, ffn1_pre_norm[l], f1_in, f1_out, ffn1_post_norm[l],
                                                   jobs)
        h = _mixer(h, batch, seq, cos, sin, mix_pre_norm[l], m_in, m_dt, conv_w[l], conv_b[l],
                   dt_bias[l], a_log[l], d_skip[l], ssm_norm[l], ret_norm[l], m_out, mix_post_norm[l])
        jobs = [cast_job(ffn1_w_in, l + 1), cast_job(ffn1_w_out, l + 1)] if l + 1 < depth else []
        h, *prepared = _ffn(h, ffn2_pre_norm[l], f2_in, f2_out, ffn2_post_norm[l], jobs)
        if prepared:
            f1_in, f1_out = prepared
    return h.reshape(batch, seq, D_MODEL)
```

```python
import collections
import functools

import jax
import jax.numpy as jnp
import numpy as np
from jax import lax
from jax.experimental import pallas as pl
from jax.experimental.pallas import tpu as pltpu

F32 = jnp.float32
BF16 = jnp.bfloat16

D_MODEL = 1024
D_FF = 2816
NORM_EPS = 1e-6
RET_NORM_EPS = 1e-5

D_INNER = 2 * D_MODEL
SSM_WIDTH = D_INNER // 2
RET_WIDTH = D_INNER - SSM_WIDTH
SSM_HEAD_DIM = 64
SSM_HEADS = SSM_WIDTH // SSM_HEAD_DIM
SSM_GROUPS = 2
HEADS_PER_GROUP = SSM_HEADS // SSM_GROUPS
GROUP_WIDTH = HEADS_PER_GROUP * SSM_HEAD_DIM
SSM_STATE = 128
BC_WIDTH = SSM_GROUPS * SSM_STATE
CONV_K = 4
CONV_DIM = SSM_WIDTH + 2 * BC_WIDTH
RET_HEADS = 4
RET_V_DIM = RET_WIDTH // RET_HEADS
RET_QK_DIM = RET_V_DIM // 2
RET_QK_WIDTH = RET_HEADS * RET_QK_DIM
ROPE_BASE = 10000.0

LANES = 128
SUBLANES = 8
BF16_ROWS = 16

CHUNK = 128
MIX_BLOCK = 256
MIX_GROUP = 2
N_MIX_CONSTS = 22
FFN_BLOCK = 1024
FFN_COL_CHUNK = 256
FFN_ROW_GROUPS = 4
ROPE_BLOCK = 2048
VMEM_LIMIT = 58 * 1024 * 1024
DT_COPIES = 4
SLAB = 256

_Z0, _Z1 = 0, SSM_WIDTH
_X0, _X1 = _Z1, _Z1 + CONV_DIM
_Q0, _Q1 = _X1, _X1 + RET_QK_WIDTH
_K0, _K1 = _Q1, _Q1 + RET_QK_WIDTH
_V0, _V1 = _K1, _K1 + RET_WIDTH
_G0, _G1 = _V1, _V1 + RET_WIDTH


def _resident(shape):
    del shape
    return pl.BlockSpec(memory_space=pltpu.VMEM)


def _resident_block(shape):
    nd = len(shape)
    return pl.BlockSpec(shape, lambda *_: (0,) * nd, pipeline_mode=pl.Buffered(1))


def _silu(x):
    h = 0.5 * x
    return h + h * jnp.tanh(h)


def _softplus(x):
    return jnp.maximum(x, 0.0) + jnp.log1p(jnp.exp(-jnp.abs(x)))


def _rms(x, w):
    ms = jnp.mean(x * x, axis=-1, keepdims=True)
    return x * lax.rsqrt(ms + NORM_EPS) * w


def _dot(a, b):
    return jnp.dot(a, b, preferred_element_type=F32)


def _dot_nt(a, b):
    return lax.dot_general(a, b, (((1,), (1,)), ((), ())), preferred_element_type=F32)


def _round_robin(*queues):
    queues = [list(q) for q in queues]
    merged = []
    while any(queues):
        for q in queues:
            if q:
                merged.append(q.pop(0))
    return merged


def _split3(v):
    p1 = v.astype(BF16)
    r1 = v - p1.astype(F32)
    p2 = r1.astype(BF16)
    p3 = (r1 - p2.astype(F32)).astype(BF16)
    return p1, p2, p3


def _cast_kernel(w_ref, o_ref):
    o_ref[...] = w_ref[...].astype(BF16)


def _mix_in_kernel(w_ref, o_ref):
    o_ref[:, 0:_X1] = w_ref[:, 0:_X1].astype(BF16)
    o_ref[:, _X1:_G1] = w_ref[:, _X1 + SSM_HEADS:_G1 + SSM_HEADS].astype(BF16)


def _mix_dt_kernel(w_ref, o_ref):
    lane = lax.broadcasted_iota(jnp.int32, (w_ref.shape[0], LANES), 1)
    d = jnp.where(lane < SSM_HEADS, w_ref[:, _X1:_X1 + LANES], 0.0)
    out = d
    for c in range(1, DT_COPIES):
        out = out + pltpu.roll(d, c * SSM_HEADS, 1)
    o_ref[...] = out.astype(BF16)


_PrepJob = collections.namedtuple("_PrepJob", ["w", "layer", "body", "out_cols"])


def _job_specs(job, n_steps):
    _, k, n = job.w.shape
    nb = next(c for c in range(min(n_steps, k // BF16_ROWS), 0, -1)
              if k % c == 0 and (k // c) % BF16_ROWS == 0)
    rows = k // nb
    layer = job.layer
    return (pl.BlockSpec((None, rows, n), lambda i: (layer, i * nb // n_steps, 0)),
            pl.BlockSpec((rows, job.out_cols), lambda i: (i * nb // n_steps, 0)),
            jax.ShapeDtypeStruct((k, job.out_cols), BF16))


def _rope_kernel(*refs, job_bodies):
    n_jobs = len(job_bodies)
    pos_ref, invf_ref = refs[:2]
    job_in = refs[2:2 + n_jobs]
    cos_ref, sin_ref = refs[2 + n_jobs:4 + n_jobs]
    job_out = refs[4 + n_jobs:]
    ang = invf_ref[...] * pos_ref[...].astype(F32)
    c = jnp.cos(ang)
    s = jnp.sin(ang)
    cos_ref[...] = jnp.concatenate([c, c], axis=0).T
    sin_ref[...] = jnp.concatenate([-s, s], axis=0).T
    for body, w_ref, out_ref in zip(job_bodies, job_in, job_out):
        body(w_ref, out_ref)


def _rope_tables(positions, jobs=()):
    t = positions.size
    tb = min(ROPE_BLOCK, t)
    assert t % tb == 0
    n_steps = t // tb
    half = RET_QK_DIM // 2
    inv_freq = ROPE_BASE ** (-jnp.arange(half, dtype=F32) / half)
    job_specs = [_job_specs(job, n_steps) for job in jobs]
    table = pl.BlockSpec((tb, LANES), lambda i: (i, 0))
    return pl.pallas_call(
        functools.partial(_rope_kernel, job_bodies=tuple(job.body for job in jobs)),
        grid=(n_steps,),
        in_specs=[pl.BlockSpec((None, 1, tb), lambda i: (i, 0, 0)),
                  pl.BlockSpec((half, 1), lambda i: (0, 0))] + [js[0] for js in job_specs],
        out_specs=[table, table] + [js[1] for js in job_specs],
        out_shape=[jax.ShapeDtypeStruct((t, LANES), F32)] * 2 + [js[2] for js in job_specs],
        compiler_params=pltpu.CompilerParams(
            dimension_semantics=("arbitrary",), vmem_limit_bytes=VMEM_LIMIT),
        name="rope_tables",
    )(positions.reshape(n_steps, 1, tb), inv_freq[:, None], *[job.w for job in jobs])


def _ffn_kernel(*refs, job_bodies):
    n_jobs = len(job_bodies)
    x_ref, pre_ref, win_ref, wout_ref, post_ref = refs[:5]
    job_in = refs[5:5 + n_jobs]
    o_ref = refs[5 + n_jobs]
    job_out = refs[6 + n_jobs:6 + 2 * n_jobs]
    xn_ref, act_ref = refs[6 + 2 * n_jobs:]

    tm = x_ref.shape[0]
    rows = tm // FFN_ROW_GROUPS
    groups = [slice(i * rows, (i + 1) * rows) for i in range(FFN_ROW_GROUPS)]
    slabs = list(range(0, D_FF, FFN_COL_CHUNK))

    def prenorm(r):
        xn_ref[r, :] = _rms(x_ref[r, :], pre_ref[...]).astype(BF16)

    def swiglu(r, lo):
        gate = _dot(xn_ref[r, :], win_ref[:, lo:lo + FFN_COL_CHUNK])
        up = _dot(xn_ref[r, :], win_ref[:, D_FF + lo:D_FF + lo + FFN_COL_CHUNK])
        act_ref[r, lo:lo + FFN_COL_CHUNK] = (_silu(gate) * up).astype(BF16)

    def project(r):
        h = _dot(act_ref[r, :], wout_ref[...])
        o_ref[r, :] = x_ref[r, :] + 0.5 * _rms(h, post_ref[...])

    prenorm(groups[0])
    for i, r in enumerate(groups):
        swiglu(r, slabs[0])
        if i + 1 < len(groups):
            prenorm(groups[i + 1])
        if i > 0:
            project(groups[i - 1])
        else:
            for body, w_ref, out_ref in zip(job_bodies, job_in, job_out):
                body(w_ref, out_ref)
        for lo in slabs[1:]:
            swiglu(r, lo)
    project(groups[-1])


def _ffn(x2d, pre, w_in, w_out, post, jobs=()):
    t = x2d.shape[0]
    tm = min(FFN_BLOCK, t)
    n_steps = t // tm
    job_specs = [_job_specs(job, n_steps) for job in jobs]
    row = pl.BlockSpec((tm, D_MODEL), lambda i: (i, 0))
    return pl.pallas_call(
        functools.partial(_ffn_kernel, job_bodies=tuple(job.body for job in jobs)),
        grid=(n_steps,),
        in_specs=[row, _resident((1, D_MODEL)), _resident_block(w_in.shape),
                  _resident_block(w_out.shape), _resident((1, D_MODEL))] + [js[0] for js in job_specs],
        out_specs=[row] + [js[1] for js in job_specs],
        out_shape=[jax.ShapeDtypeStruct((t, D_MODEL), F32)] + [js[2] for js in job_specs],
        scratch_shapes=[pltpu.VMEM((tm, D_MODEL), BF16),
                        pltpu.VMEM((tm, D_FF), BF16)],
        compiler_params=pltpu.CompilerParams(
            dimension_semantics=("arbitrary",), vmem_limit_bytes=VMEM_LIMIT),
        name="ffn",
    )(x2d, pre[None, :], w_in, w_out, post[None, :], *[job.w for job in jobs])


_Slot = collections.namedtuple(
    "_Slot", ["x", "zs", "xs", "bc", "bT", "q", "qd", "k", "kdT", "v", "gs", "dt", "dtT"])


def _mixer_kernel(*refs, tb, nj):
    g = MIX_GROUP
    x_refs = refs[0:g]
    cos_refs = refs[g:2 * g]
    sin_refs = refs[2 * g:3 * g]
    consts = refs[3 * g:3 * g + N_MIX_CONSTS]
    o_ref = refs[3 * g + N_MIX_CONSTS]
    scratch = refs[3 * g + N_MIX_CONSTS + 1:]
    u_ref, xbc_ref, y_ref, cat_ref, mixed_ref, sstate_ref, rstate_ref = scratch[:7]
    n = len(_Slot._fields)
    slot_a = _Slot(*scratch[7:7 + n])
    slot_b = _Slot(*scratch[7 + n:7 + 2 * n])
    i = pl.program_id(0)

    @pl.when(i == 0)
    def _():
        for ref in slot_a:
            ref[...] = jnp.zeros(ref.shape, ref.dtype)
        xbc_ref[0:SUBLANES, :] = jnp.zeros((SUBLANES, CONV_DIM), F32)

    @pl.when(jnp.logical_or(i == 0, lax.rem(g * i + nj - g, nj) == 0))
    def _():
        sstate_ref[...] = jnp.zeros(sstate_ref.shape, F32)
        rstate_ref[...] = jnp.zeros(rstate_ref.shape, F32)

    shared = consts + (u_ref, xbc_ref, y_ref, cat_ref, mixed_ref, sstate_ref, rstate_ref)
    for k in range(g):
        p_slot, s_slot = (slot_b, slot_a) if k % 2 == 0 else (slot_a, slot_b)
        new_sequence = (lax.rem(g * i, nj) == 0) if k == g - 1 else None
        _mixer_step(x_refs[k], cos_refs[k], sin_refs[k], o_ref.at[pl.ds(k * tb, tb)], *shared,
                    p_slot, s_slot, new_sequence, tb=tb)


def _mixer_step(xp_ref, cos_ref, sin_ref, o_ref, pre_ref, wmain_ref, wdt_ref, wdtT_ref,
                convw_ref, convb_ref, dtb_row_ref, dtb_col_ref, alog_row_ref, alog_col_ref, dskip_ref,
                ssmnorm_ref, retnorm_ref, wout_ref, post_ref, expand_ref, tril_ref, triu_ref,
                dmat_ref, qdec_ref, kdec_ref, cdec_ref,
                u_ref, xbc_ref, y_ref, cat_ref, mixed_ref, sstate_ref, rstate_ref, P, S, new_sequence,
                *, tb):
    L = CHUNK
    if new_sequence is not None:
        tail = xbc_ref[0:SUBLANES, :]
        xbc_ref[0:SUBLANES, :] = jnp.where(new_sequence, jnp.zeros_like(tail), tail)

    def proj(lo, hi):
        return _dot(u_ref[...], wmain_ref[:, lo:hi])

    def p_norm():
        x = xp_ref[...]
        P.x[...] = x
        u_ref[...] = _rms(x, pre_ref[...]).astype(BF16)

    def p_gate(dst, base, j):
        def piece():
            sl = slice(j * SLAB, (j + 1) * SLAB)
            t = proj(base + j * SLAB, base + (j + 1) * SLAB)
            dst[:, sl] = _silu(t)
        return piece

    def p_value(j):
        def piece():
            sl = slice(j * SLAB, (j + 1) * SLAB)
            P.v[:, sl] = proj(_V0 + j * SLAB, _V0 + (j + 1) * SLAB).astype(BF16)
        return piece

    def p_rotary(j):
        def piece():
            q = proj(_Q0 + j * SLAB, _Q0 + (j + 1) * SLAB)
            k = proj(_K0 + j * SLAB, _K0 + (j + 1) * SLAB)
            cos = cos_ref[...]
            sin = sin_ref[...]
            for hh in range(SLAB // RET_QK_DIM):
                loc = slice(hh * RET_QK_DIM, (hh + 1) * RET_QK_DIM)
                sl = slice(j * SLAB + hh * RET_QK_DIM, j * SLAB + (hh + 1) * RET_QK_DIM)
                qh = q[:, loc]
                kh = k[:, loc]
                qr = qh * cos + pltpu.roll(qh, RET_QK_DIM // 2, 1) * sin
                kr = (kh * cos + pltpu.roll(kh, RET_QK_DIM // 2, 1) * sin) * (RET_QK_DIM ** -0.5)
                P.q[:, sl] = qr.astype(BF16)
                P.qd[:, sl] = (qr * qdec_ref[:, sl]).astype(BF16)
                P.k[:, sl] = kr.astype(BF16)
                P.kdT[sl, :] = (kr * kdec_ref[:, sl]).T.astype(BF16)
        return piece

    def p_dt():
        P.dt[...] = _softplus(_dot(u_ref[...], wdt_ref[...]) + dtb_row_ref[...])
        P.dtT[...] = _softplus(_dot_nt(wdtT_ref[...], u_ref[...]) + dtb_col_ref[...])

    def p_conv(j):
        def piece():
            sl = slice(j * SLAB, (j + 1) * SLAB)
            xbc_ref[SUBLANES:SUBLANES + tb, sl] = proj(_X0 + j * SLAB, _X0 + (j + 1) * SLAB)
            rows = xbc_ref[0:SUBLANES + tb, sl]
            conv = convb_ref[:, sl] + convw_ref[CONV_K - 1:CONV_K, sl] * rows[SUBLANES:, :]
            for kk in range(CONV_K - 1):
                shifted = pltpu.roll(rows, CONV_K - 1 - kk, 0)
                conv = conv + convw_ref[kk:kk + 1, sl] * shifted[SUBLANES:, :]
            act = _silu(conv)
            xbc_ref[0:SUBLANES, sl] = xbc_ref[tb:tb + SUBLANES, sl]
            if (j + 1) * SLAB <= SSM_WIDTH:
                P.xs[:, sl] = act
            else:
                lo = j * SLAB - SSM_WIDTH
                P.bc[:, lo:lo + SLAB] = act.astype(BF16)
                if lo < BC_WIDTH:
                    nb = min(SLAB, BC_WIDTH - lo)
                    P.bT[lo:lo + nb, :] = act[:, 0:nb].T.astype(BF16)
        return piece

    a_row = -jnp.exp(alog_row_ref[...])
    a_col = -jnp.exp(alog_col_ref[...])
    rows = lax.broadcasted_iota(jnp.int32, (L, L), 0)
    cols = lax.broadcasted_iota(jnp.int32, (L, L), 1)
    causal = rows >= cols
    lane = lax.broadcasted_iota(jnp.int32, (L, LANES), 1)
    low_half = lane < SSM_HEAD_DIM

    def s_chunk_pieces(c):
        r = slice(c * L, (c + 1) * L)
        ctx = {}

        def decays():
            dt_c = S.dt[r, :]
            dtT_c = S.dtT[:, r]
            parts = _dot(tril_ref[...], jnp.concatenate(_split3(dt_c * a_row), axis=1))
            acum = parts[:, 0:LANES] + parts[:, LANES:2 * LANES] + parts[:, 2 * LANES:3 * LANES]
            partsT = _dot(jnp.concatenate(_split3(dtT_c * a_col), axis=0), triu_ref[...])
            acumT = (partsT[0:SSM_HEADS] + partsT[SSM_HEADS:2 * SSM_HEADS]
                     + partsT[2 * SSM_HEADS:3 * SSM_HEADS])
            ctx["acum"] = acum
            ctx["lrow"] = acumT - jnp.log(dtT_c)
            e = jnp.exp(acum)
            w = jnp.exp(acum[L - 1:L, :] - acum) * dt_c
            e_hi = e.astype(BF16).astype(F32)
            w_hi = w.astype(BF16).astype(F32)
            packed = jnp.where(lane < SSM_HEADS, e_hi,
                               jnp.where(lane < 2 * SSM_HEADS, e - e_hi,
                                         jnp.where(lane < 3 * SSM_HEADS, w_hi,
                                                   jnp.where(lane < 4 * SSM_HEADS, w - w_hi, 0.0))))
            ex = _dot(packed.astype(BF16), expand_ref[...])
            ctx["ex_e"] = ex[:, 0:SSM_WIDTH]
            ctx["xw"] = (S.xs[r, :] * ex[:, SSM_WIDTH:2 * SSM_WIDTH]).astype(BF16)

        def group_open(gi):
            def piece():
                gsl = slice(gi * GROUP_WIDTH, (gi + 1) * GROUP_WIDTH)
                b_g = S.bc[r, gi * SSM_STATE:(gi + 1) * SSM_STATE]
                c_g = S.bc[r, BC_WIDTH + gi * SSM_STATE:BC_WIDTH + (gi + 1) * SSM_STATE]
                ctx["cb"] = _dot_nt(c_g, b_g)
                st = sstate_ref[gi]
                ctx["y_inter"] = _dot(c_g, st.astype(BF16)) * ctx["ex_e"][:, gsl]
                sstate_ref[gi] = (st * ctx["ex_e"][L - 1:L, gsl]
                                  + _dot(S.bT[gi * SSM_STATE:(gi + 1) * SSM_STATE, r], ctx["xw"][:, gsl]))
            return piece

        def head_pair(gi, p):
            def piece():
                h0 = gi * HEADS_PER_GROUP + 2 * p
                psl = slice(h0 * SSM_HEAD_DIM, (h0 + 2) * SSM_HEAD_DIM)
                gmats = []
                for h in (h0, h0 + 1):
                    diff = ctx["acum"][:, h:h + 1] - ctx["lrow"][h:h + 1, :]
                    gmats.append(ctx["cb"] * jnp.exp(jnp.where(causal, diff, -jnp.inf)))
                lhs = jnp.concatenate(gmats, axis=1).astype(BF16)
                xp = S.xs[r, psl]
                rhs = jnp.concatenate([jnp.where(low_half, xp, 0.0),
                                       jnp.where(low_half, 0.0, xp)], axis=0).astype(BF16)
                y = (_dot(lhs, rhs) + ctx["y_inter"][:, 2 * p * SSM_HEAD_DIM:(2 * p + 2) * SSM_HEAD_DIM]
                     + xp * dskip_ref[:, psl])
                y_ref[:, psl] = y * S.zs[r, psl]
            return piece

        def ssd_norm():
            cat_ref[r, 0:SSM_WIDTH] = _rms(y_ref[...], ssmnorm_ref[...]).astype(BF16)

        def retention(h):
            def piece():
                qsl = slice(h * RET_QK_DIM, (h + 1) * RET_QK_DIM)
                vsl = slice(h * RET_V_DIM, (h + 1) * RET_V_DIM)
                vh = S.v[r, vsl]
                sc = _dot_nt(S.q[r, qsl], S.k[r, qsl]) * dmat_ref[h]
                st = rstate_ref[h]
                lhs = jnp.concatenate([sc.astype(BF16), S.qd[r, qsl]], axis=1)
                rhs = jnp.concatenate([vh, st.astype(BF16)], axis=0)
                rh = _dot(lhs, rhs)
                rstate_ref[h] = st * cdec_ref[:, vsl] + _dot(S.kdT[qsl, r], vh)
                d = rh - jnp.mean(rh, axis=-1, keepdims=True)
                var = jnp.mean(d * d, axis=-1, keepdims=True)
                rn = d * lax.rsqrt(var + RET_NORM_EPS) * retnorm_ref[:, vsl]
                cat_ref[r, SSM_WIDTH + h * RET_V_DIM:SSM_WIDTH + (h + 1) * RET_V_DIM] = (
                    S.gs[r, vsl] * rn).astype(BF16)
            return piece

        pieces = [decays]
        for gi in range(SSM_GROUPS):
            pieces.append(group_open(gi))
            pieces += [head_pair(gi, p) for p in range(HEADS_PER_GROUP // 2)]
        pieces.append(ssd_norm)
        pieces += [retention(h) for h in range(RET_HEADS)]
        return pieces

    def s_out(j):
        def piece():
            sl = slice(j * SLAB, (j + 1) * SLAB)
            mixed_ref[:, sl] = _dot(cat_ref[...], wout_ref[:, sl])
        return piece

    def s_residual():
        o_ref[...] = S.x[...] + _rms(mixed_ref[...], post_ref[...])

    gates = ([p_gate(P.zs, _Z0, j) for j in range(SSM_WIDTH // SLAB)]
             + [p_gate(P.gs, _G0, j) for j in range(RET_WIDTH // SLAB)]
             + [p_rotary(j) for j in range(RET_QK_WIDTH // SLAB)] + [p_dt])
    convs = [p_conv(j) for j in range(CONV_DIM // SLAB)]
    early = _round_robin(gates, convs)
    s_scan = [piece for c in range(tb // L) for piece in s_chunk_pieces(c)]
    p_norm()
    done = 0
    for i, piece in enumerate(early):
        piece()
        upto = (i + 1) * len(s_scan) // len(early)
        for sp in s_scan[done:upto]:
            sp()
        done = upto
    outs = [s_out(j) for j in range(D_MODEL // SLAB)]
    values = [p_value(j) for j in range(RET_WIDTH // SLAB)]
    for out_piece, value_piece in zip(outs, values):
        out_piece()
        value_piece()
    s_residual()


def _retention_tables(tb):
    log_gamma = jnp.log(1.0 - 2.0 ** (-5.0 - jnp.arange(RET_HEADS, dtype=F32)))
    pos = jnp.arange(CHUNK, dtype=F32)
    rel = pos[:, None] - pos[None, :]
    dmat = jnp.exp(jnp.where((rel >= 0)[None], rel[None] * log_gamma[:, None, None], -jnp.inf))
    q_decay = jnp.exp((pos[:, None] + 1.0) * log_gamma[None])
    k_decay = jnp.exp((CHUNK - 1.0 - pos[:, None]) * log_gamma[None])
    chunk_decay = jnp.exp(CHUNK * log_gamma)
    qdec = jnp.tile(jnp.repeat(q_decay, RET_QK_DIM, axis=1), (tb // CHUNK, 1))
    kdec = jnp.tile(jnp.repeat(k_decay, RET_QK_DIM, axis=1), (tb // CHUNK, 1))
    cdec = jnp.repeat(chunk_decay, RET_V_DIM)[None, :]
    return dmat, qdec, kdec, cdec


def _constant_matrices():
    e = np.zeros((LANES, 2 * SSM_WIDTH), np.float32)
    for g in range(DT_COPIES):
        for h in range(SSM_HEADS):
            c0 = (g // 2) * SSM_WIDTH + h * SSM_HEAD_DIM
            e[g * SSM_HEADS + h, c0:c0 + SSM_HEAD_DIM] = 1.0
    tril = np.tril(np.ones((CHUNK, CHUNK), np.float32))
    return jnp.asarray(e, BF16), jnp.asarray(tril, BF16), jnp.asarray(tril.T, BF16)


def _slot_scratch(tb):
    return list(_Slot(
        x=pltpu.VMEM((tb, D_MODEL), F32),
        zs=pltpu.VMEM((tb, SSM_WIDTH), F32),
        xs=pltpu.VMEM((tb, SSM_WIDTH), F32),
        bc=pltpu.VMEM((tb, 2 * BC_WIDTH), BF16),
        bT=pltpu.VMEM((BC_WIDTH, tb), BF16),
        q=pltpu.VMEM((tb, RET_QK_WIDTH), BF16),
        qd=pltpu.VMEM((tb, RET_QK_WIDTH), BF16),
        k=pltpu.VMEM((tb, RET_QK_WIDTH), BF16),
        kdT=pltpu.VMEM((RET_QK_WIDTH, tb), BF16),
        v=pltpu.VMEM((tb, RET_WIDTH), BF16),
        gs=pltpu.VMEM((tb, RET_WIDTH), F32),
        dt=pltpu.VMEM((tb, LANES), F32),
        dtT=pltpu.VMEM((SSM_HEADS, tb), F32),
    ))


def _mixer(x2d, batch, seq, cos, sin, pre, w_main, w_dt_pad, conv_w, conv_b, dt_bias, a_log, d_skip,
           ssm_norm, ret_norm, w_out, post):
    tb = min(MIX_BLOCK, seq)
    nj = seq // tb
    nblocks = batch * nj
    lane_pad = (0, LANES - DT_COPIES * SSM_HEADS)
    w_dt_t = w_dt_pad[:, 0:SSM_HEADS].T
    dtb_row = jnp.pad(jnp.tile(dt_bias, DT_COPIES), lane_pad)[None, :]
    alog_row = jnp.pad(jnp.tile(a_log, DT_COPIES), lane_pad)[None, :]
    dskip = jnp.repeat(d_skip, SSM_HEAD_DIM)[None, :]
    dmat, qdec, kdec, cdec = _retention_tables(tb)
    expand, tril, triu = _constant_matrices()

    g = MIX_GROUP
    assert nj % g == 0, "a grid step's blocks must not straddle two sequences"

    def block(k):
        return lambda i: (jnp.clip(g * i - g + 1 + k, 0, nblocks - 1), 0)

    prev = lambda i: (jnp.maximum(i - 1, 0), 0)
    operands = (
        [(x2d, pl.BlockSpec((tb, D_MODEL), block(k))) for k in range(g)]
        + [(cos, pl.BlockSpec((tb, LANES), block(k))) for k in range(g)]
        + [(sin, pl.BlockSpec((tb, LANES), block(k))) for k in range(g)])
    consts = [
        (pre[None, :], None),
        (w_main, _resident_block(w_main.shape)),
        (w_dt_pad, None),
        (w_dt_t, None),
        (conv_w, None),
        (conv_b[None, :], None),
        (dtb_row, None),
        (dt_bias[:, None], None),
        (alog_row, None),
        (a_log[:, None], None),
        (dskip, None),
        (ssm_norm[None, :], None),
        (ret_norm[None, :], None),
        (w_out, _resident_block(w_out.shape)),
        (post[None, :], None),
        (expand, None),
        (tril, None),
        (triu, None),
        (dmat, None),
        (qdec, None),
        (kdec, None),
        (cdec, None),
    ]
    assert len(consts) == N_MIX_CONSTS
    operands += consts
    args = [a for a, _ in operands]
    in_specs = [sp if sp is not None else _resident(a.shape) for a, sp in operands]
    return pl.pallas_call(
        functools.partial(_mixer_kernel, tb=tb, nj=nj),
        grid=(nblocks // g + 1,),
        in_specs=in_specs,
        out_specs=pl.BlockSpec((g * tb, D_MODEL), prev),
        out_shape=jax.ShapeDtypeStruct(x2d.shape, F32),
        scratch_shapes=[
            pltpu.VMEM((tb, D_MODEL), BF16),
            pltpu.VMEM((SUBLANES + tb, CONV_DIM), F32),
            pltpu.VMEM((CHUNK, SSM_WIDTH), F32),
            pltpu.VMEM((tb, D_INNER), BF16),
            pltpu.VMEM((tb, D_MODEL), F32),
            pltpu.VMEM((SSM_GROUPS, SSM_STATE, GROUP_WIDTH), F32),
            pltpu.VMEM((RET_HEADS, RET_QK_DIM, RET_V_DIM), F32),
        ] + 2 * _slot_scratch(tb),
        compiler_params=pltpu.CompilerParams(
            dimension_semantics=("arbitrary",), vmem_limit_bytes=VMEM_LIMIT),
        name="mixer",
    )(*args)


def kernel(x, positions, ffn1_pre_norm, ffn1_w_in, ffn1_w_out, ffn1_post_norm, mix_pre_norm, mix_w_in, conv_w, conv_b, dt_bias, a_log, d_skip, ssm_norm, ret_norm, mix_w_out, mix_post_norm, ffn2_pre_norm, ffn2_w_in, ffn2_w_out, ffn2_post_norm):
    batch, seq, _ = x.shape
    depth = ffn1_w_in.shape[0]

    def cast_job(w, layer):
        return _PrepJob(w, layer, _cast_kernel, w.shape[2])

    cos, sin, f1_in, f1_out = _rope_tables(positions, [cast_job(ffn1_w_in, 0), cast_job(ffn1_w_out, 0)])
    h = x.reshape(batch * seq, D_MODEL)
    for l in range(depth):
        jobs = [_PrepJob(mix_w_in, l, _mix_in_kernel, _G1), _PrepJob(mix_w_in, l, _mix_dt_kernel, LANES),
                cast_job(mix_w_out, l), cast_job(ffn2_w_in, l), cast_job(ffn2_w_out, l)]
        h, m_in, m_dt, m_out, f2_in, f2_out = _ffn(h, ffn1_pre_norm[l], f1_in, f1_out, ffn1_post_norm[l],
                                                   jobs)
        h = _mixer(h, batch, seq, cos, sin, mix_pre_norm[l], m_in, m_dt, conv_w[l], conv_b[l],
                   dt_bias[l], a_log[l], d_skip[l], ssm_norm[l], ret_norm[l], m_out, mix_post_norm[l])
        jobs = [cast_job(ffn1_w_in, l + 1), cast_job(ffn1_w_out, l + 1)] if l + 1 < depth else []
        h, *prepared = _ffn(h, ffn2_pre_norm[l], f2_in, f2_out, ffn2_post_norm[l], jobs)
        if prepared:
            f1_in, f1_out = prepared
    return h.reshape(batch, seq, D_MODEL)
```

```python
import collections
import functools

import jax
import jax.numpy as jnp
import numpy as np
from jax import lax
from jax.experimental import pallas as pl
from jax.experimental.pallas import tpu as pltpu

F32 = jnp.float32
BF16 = jnp.bfloat16

D_MODEL = 1024
D_FF = 2816
NORM_EPS = 1e-6
RET_NORM_EPS = 1e-5

D_INNER = 2 * D_MODEL
SSM_WIDTH = D_INNER // 2
RET_WIDTH = D_INNER - SSM_WIDTH
SSM_HEAD_DIM = 64
SSM_HEADS = SSM_WIDTH // SSM_HEAD_DIM
SSM_GROUPS = 2
HEADS_PER_GROUP = SSM_HEADS // SSM_GROUPS
GROUP_WIDTH = HEADS_PER_GROUP * SSM_HEAD_DIM
SSM_STATE = 128
BC_WIDTH = SSM_GROUPS * SSM_STATE
CONV_K = 4
CONV_DIM = SSM_WIDTH + 2 * BC_WIDTH
RET_HEADS = 4
RET_V_DIM = RET_WIDTH // RET_HEADS
RET_QK_DIM = RET_V_DIM // 2
RET_QK_WIDTH = RET_HEADS * RET_QK_DIM
ROPE_BASE = 10000.0

LANES = 128
SUBLANES = 8
BF16_ROWS = 16

CHUNK = 128
MIX_BLOCK = 256
MIX_GROUP = 2
N_MIX_CONSTS = 22
FFN_BLOCK = 1024
FFN_COL_CHUNK = 256
FFN_ROW_GROUPS = 4
ROPE_BLOCK = 2048
VMEM_LIMIT = 58 * 1024 * 1024
DT_COPIES = 4
SLAB = 256
OUT_SLAB = 512

_Z0, _Z1 = 0, SSM_WIDTH
_X0, _X1 = _Z1, _Z1 + CONV_DIM
_Q0, _Q1 = _X1, _X1 + RET_QK_WIDTH
_K0, _K1 = _Q1, _Q1 + RET_QK_WIDTH
_V0, _V1 = _K1, _K1 + RET_WIDTH
_G0, _G1 = _V1, _V1 + RET_WIDTH


def _resident(shape):
    del shape
    return pl.BlockSpec(memory_space=pltpu.VMEM)


def _resident_block(shape):
    nd = len(shape)
    return pl.BlockSpec(shape, lambda *_: (0,) * nd, pipeline_mode=pl.Buffered(1))


def _silu(x):
    h = 0.5 * x
    return h + h * jnp.tanh(h)


def _softplus(x):
    return jnp.maximum(x, 0.0) + jnp.log1p(jnp.exp(-jnp.abs(x)))


def _rms(x, w):
    ms = jnp.mean(x * x, axis=-1, keepdims=True)
    return x * lax.rsqrt(ms + NORM_EPS) * w


def _dot(a, b):
    return jnp.dot(a, b, preferred_element_type=F32)


def _dot_nt(a, b):
    return lax.dot_general(a, b, (((1,), (1,)), ((), ())), preferred_element_type=F32)


def _round_robin(*queues):
    queues = [list(q) for q in queues]
    merged = []
    while any(queues):
        for q in queues:
            if q:
                merged.append(q.pop(0))
    return merged


def _split3(v):
    p1 = v.astype(BF16)
    r1 = v - p1.astype(F32)
    p2 = r1.astype(BF16)
    p3 = (r1 - p2.astype(F32)).astype(BF16)
    return p1, p2, p3


def _cast_kernel(w_ref, o_ref):
    o_ref[...] = w_ref[...].astype(BF16)


def _mix_in_kernel(w_ref, o_ref):
    o_ref[:, 0:_X1] = w_ref[:, 0:_X1].astype(BF16)
    o_ref[:, _X1:_G1] = w_ref[:, _X1 + SSM_HEADS:_G1 + SSM_HEADS].astype(BF16)


def _mix_dt_kernel(w_ref, o_ref):
    lane = lax.broadcasted_iota(jnp.int32, (w_ref.shape[0], LANES), 1)
    d = jnp.where(lane < SSM_HEADS, w_ref[:, _X1:_X1 + LANES], 0.0)
    out = d
    for c in range(1, DT_COPIES):
        out = out + pltpu.roll(d, c * SSM_HEADS, 1)
    o_ref[...] = out.astype(BF16)


_PrepJob = collections.namedtuple("_PrepJob", ["w", "layer", "body", "out_cols"])


def _job_specs(job, n_steps):
    _, k, n = job.w.shape
    nb = next(c for c in range(min(n_steps, k // BF16_ROWS), 0, -1)
              if k % c == 0 and (k // c) % BF16_ROWS == 0)
    rows = k // nb
    layer = job.layer
    return (pl.BlockSpec((None, rows, n), lambda i: (layer, i * nb // n_steps, 0)),
            pl.BlockSpec((rows, job.out_cols), lambda i: (i * nb // n_steps, 0)),
            jax.ShapeDtypeStruct((k, job.out_cols), BF16))


def _rope_kernel(*refs, job_bodies):
    n_jobs = len(job_bodies)
    pos_ref, invf_ref = refs[:2]
    job_in = refs[2:2 + n_jobs]
    cos_ref, sin_ref = refs[2 + n_jobs:4 + n_jobs]
    job_out = refs[4 + n_jobs:]
    ang = invf_ref[...] * pos_ref[...].astype(F32)
    c = jnp.cos(ang)
    s = jnp.sin(ang)
    cos_ref[...] = jnp.concatenate([c, c], axis=0).T
    sin_ref[...] = jnp.concatenate([-s, s], axis=0).T
    for body, w_ref, out_ref in zip(job_bodies, job_in, job_out):
        body(w_ref, out_ref)


def _rope_tables(positions, jobs=()):
    t = positions.size
    tb = min(ROPE_BLOCK, t)
    assert t % tb == 0
    n_steps = t // tb
    half = RET_QK_DIM // 2
    inv_freq = ROPE_BASE ** (-jnp.arange(half, dtype=F32) / half)
    job_specs = [_job_specs(job, n_steps) for job in jobs]
    table = pl.BlockSpec((tb, LANES), lambda i: (i, 0))
    return pl.pallas_call(
        functools.partial(_rope_kernel, job_bodies=tuple(job.body for job in jobs)),
        grid=(n_steps,),
        in_specs=[pl.BlockSpec((None, 1, tb), lambda i: (i, 0, 0)),
                  pl.BlockSpec((half, 1), lambda i: (0, 0))] + [js[0] for js in job_specs],
        out_specs=[table, table] + [js[1] for js in job_specs],
        out_shape=[jax.ShapeDtypeStruct((t, LANES), F32)] * 2 + [js[2] for js in job_specs],
        compiler_params=pltpu.CompilerParams(
            dimension_semantics=("arbitrary",), vmem_limit_bytes=VMEM_LIMIT),
        name="rope_tables",
    )(positions.reshape(n_steps, 1, tb), inv_freq[:, None], *[job.w for job in jobs])


def _ffn_kernel(*refs, job_bodies):
    n_jobs = len(job_bodies)
    x_ref, pre_ref, win_ref, wout_ref, post_ref = refs[:5]
    job_in = refs[5:5 + n_jobs]
    o_ref = refs[5 + n_jobs]
    job_out = refs[6 + n_jobs:6 + 2 * n_jobs]
    xn_ref, act_ref = refs[6 + 2 * n_jobs:]

    tm = x_ref.shape[0]
    rows = tm // FFN_ROW_GROUPS
    groups = [slice(i * rows, (i + 1) * rows) for i in range(FFN_ROW_GROUPS)]
    slabs = list(range(0, D_FF, FFN_COL_CHUNK))

    def prenorm(r):
        xn_ref[r, :] = _rms(x_ref[r, :], pre_ref[...]).astype(BF16)

    def swiglu(r, lo):
        gate = _dot(xn_ref[r, :], win_ref[:, lo:lo + FFN_COL_CHUNK])
        up = _dot(xn_ref[r, :], win_ref[:, D_FF + lo:D_FF + lo + FFN_COL_CHUNK])
        act_ref[r, lo:lo + FFN_COL_CHUNK] = (_silu(gate) * up).astype(BF16)

    def project(r):
        h = _dot(act_ref[r, :], wout_ref[...])
        o_ref[r, :] = x_ref[r, :] + 0.5 * _rms(h, post_ref[...])

    prenorm(groups[0])
    for i, r in enumerate(groups):
        swiglu(r, slabs[0])
        if i + 1 < len(groups):
            prenorm(groups[i + 1])
        if i > 0:
            project(groups[i - 1])
        else:
            for body, w_ref, out_ref in zip(job_bodies, job_in, job_out):
                body(w_ref, out_ref)
        for lo in slabs[1:]:
            swiglu(r, lo)
    project(groups[-1])


def _ffn(x2d, pre, w_in, w_out, post, jobs=()):
    t = x2d.shape[0]
    tm = min(FFN_BLOCK, t)
    n_steps = t // tm
    job_specs = [_job_specs(job, n_steps) for job in jobs]
    row = pl.BlockSpec((tm, D_MODEL), lambda i: (i, 0))
    return pl.pallas_call(
        functools.partial(_ffn_kernel, job_bodies=tuple(job.body for job in jobs)),
        grid=(n_steps,),
        in_specs=[row, _resident((1, D_MODEL)), _resident_block(w_in.shape),
                  _resident_block(w_out.shape), _resident((1, D_MODEL))] + [js[0] for js in job_specs],
        out_specs=[row] + [js[1] for js in job_specs],
        out_shape=[jax.ShapeDtypeStruct((t, D_MODEL), F32)] + [js[2] for js in job_specs],
        scratch_shapes=[pltpu.VMEM((tm, D_MODEL), BF16),
                        pltpu.VMEM((tm, D_FF), BF16)],
        compiler_params=pltpu.CompilerParams(
            dimension_semantics=("arbitrary",), vmem_limit_bytes=VMEM_LIMIT),
        name="ffn",
    )(x2d, pre[None, :], w_in, w_out, post[None, :], *[job.w for job in jobs])


_Slot = collections.namedtuple(
    "_Slot", ["x", "zs", "xs", "bc", "bT", "q", "qd", "k", "kdT", "v", "gs", "dt", "dtT"])


def _mixer_kernel(*refs, tb, nj):
    g = MIX_GROUP
    x_refs = refs[0:g]
    cos_refs = refs[g:2 * g]
    sin_refs = refs[2 * g:3 * g]
    consts = refs[3 * g:3 * g + N_MIX_CONSTS]
    o_ref = refs[3 * g + N_MIX_CONSTS]
    scratch = refs[3 * g + N_MIX_CONSTS + 1:]
    u_ref, xbc_ref, y_ref, cat_ref, mixed_ref, sstate_ref, rstate_ref = scratch[:7]
    n = len(_Slot._fields)
    slot_a = _Slot(*scratch[7:7 + n])
    slot_b = _Slot(*scratch[7 + n:7 + 2 * n])
    i = pl.program_id(0)

    @pl.when(i == 0)
    def _():
        for ref in slot_a:
            ref[...] = jnp.zeros(ref.shape, ref.dtype)
        xbc_ref[0:SUBLANES, :] = jnp.zeros((SUBLANES, CONV_DIM), F32)

    @pl.when(jnp.logical_or(i == 0, lax.rem(g * i + nj - g, nj) == 0))
    def _():
        sstate_ref[...] = jnp.zeros(sstate_ref.shape, F32)
        rstate_ref[...] = jnp.zeros(rstate_ref.shape, F32)

    shared = consts + (u_ref, xbc_ref, y_ref, cat_ref, mixed_ref, sstate_ref, rstate_ref)
    for k in range(g):
        p_slot, s_slot = (slot_b, slot_a) if k % 2 == 0 else (slot_a, slot_b)
        new_sequence = (lax.rem(g * i, nj) == 0) if k == g - 1 else None
        _mixer_step(x_refs[k], cos_refs[k], sin_refs[k], o_ref.at[pl.ds(k * tb, tb)], *shared,
                    p_slot, s_slot, new_sequence, tb=tb)


def _mixer_step(xp_ref, cos_ref, sin_ref, o_ref, pre_ref, wmain_ref, wdt_ref, wdtT_ref,
                convw_ref, convb_ref, dtb_row_ref, dtb_col_ref, alog_row_ref, alog_col_ref, dskip_ref,
                ssmnorm_ref, retnorm_ref, wout_ref, post_ref, expand_ref, tril_ref, triu_ref,
                dmat_ref, qdec_ref, kdec_ref, cdec_ref,
                u_ref, xbc_ref, y_ref, cat_ref, mixed_ref, sstate_ref, rstate_ref, P, S, new_sequence,
                *, tb):
    L = CHUNK
    if new_sequence is not None:
        tail = xbc_ref[0:SUBLANES, :]
        xbc_ref[0:SUBLANES, :] = jnp.where(new_sequence, jnp.zeros_like(tail), tail)

    def proj(lo, hi):
        return _dot(u_ref[...], wmain_ref[:, lo:hi])

    def p_norm():
        x = xp_ref[...]
        P.x[...] = x
        u_ref[...] = _rms(x, pre_ref[...]).astype(BF16)

    def p_gate(dst, base, j):
        def piece():
            sl = slice(j * SLAB, (j + 1) * SLAB)
            t = proj(base + j * SLAB, base + (j + 1) * SLAB)
            dst[:, sl] = _silu(t)
        return piece

    def p_value(j):
        def piece():
            sl = slice(j * SLAB, (j + 1) * SLAB)
            P.v[:, sl] = proj(_V0 + j * SLAB, _V0 + (j + 1) * SLAB).astype(BF16)
        return piece

    def p_rotary(j):
        def piece():
            q = proj(_Q0 + j * SLAB, _Q0 + (j + 1) * SLAB)
            k = proj(_K0 + j * SLAB, _K0 + (j + 1) * SLAB)
            cos = cos_ref[...]
            sin = sin_ref[...]
            for hh in range(SLAB // RET_QK_DIM):
                loc = slice(hh * RET_QK_DIM, (hh + 1) * RET_QK_DIM)
                sl = slice(j * SLAB + hh * RET_QK_DIM, j * SLAB + (hh + 1) * RET_QK_DIM)
                qh = q[:, loc]
                kh = k[:, loc]
                qr = qh * cos + pltpu.roll(qh, RET_QK_DIM // 2, 1) * sin
                kr = (kh * cos + pltpu.roll(kh, RET_QK_DIM // 2, 1) * sin) * (RET_QK_DIM ** -0.5)
                P.q[:, sl] = qr.astype(BF16)
                P.qd[:, sl] = (qr * qdec_ref[:, sl]).astype(BF16)
                P.k[:, sl] = kr.astype(BF16)
                P.kdT[sl, :] = (kr * kdec_ref[:, sl]).T.astype(BF16)
        return piece

    def p_dt():
        P.dt[...] = _softplus(_dot(u_ref[...], wdt_ref[...]) + dtb_row_ref[...])
        P.dtT[...] = _softplus(_dot_nt(wdtT_ref[...], u_ref[...]) + dtb_col_ref[...])

    def p_conv(j):
        def piece():
            sl = slice(j * SLAB, (j + 1) * SLAB)
            xbc_ref[SUBLANES:SUBLANES + tb, sl] = proj(_X0 + j * SLAB, _X0 + (j + 1) * SLAB)
            rows = xbc_ref[0:SUBLANES + tb, sl]
            conv = convb_ref[:, sl] + convw_ref[CONV_K - 1:CONV_K, sl] * rows[SUBLANES:, :]
            for kk in range(CONV_K - 1):
                shifted = pltpu.roll(rows, CONV_K - 1 - kk, 0)
                conv = conv + convw_ref[kk:kk + 1, sl] * shifted[SUBLANES:, :]
            act = _silu(conv)
            xbc_ref[0:SUBLANES, sl] = xbc_ref[tb:tb + SUBLANES, sl]
            if (j + 1) * SLAB <= SSM_WIDTH:
                P.xs[:, sl] = act
            else:
                lo = j * SLAB - SSM_WIDTH
                P.bc[:, lo:lo + SLAB] = act.astype(BF16)
                if lo < BC_WIDTH:
                    nb = min(SLAB, BC_WIDTH - lo)
                    P.bT[lo:lo + nb, :] = act[:, 0:nb].T.astype(BF16)
        return piece

    a_row = -jnp.exp(alog_row_ref[...])
    a_col = -jnp.exp(alog_col_ref[...])
    rows = lax.broadcasted_iota(jnp.int32, (L, L), 0)
    cols = lax.broadcasted_iota(jnp.int32, (L, L), 1)
    causal = rows >= cols
    lane = lax.broadcasted_iota(jnp.int32, (L, LANES), 1)
    low_half = lane < SSM_HEAD_DIM

    def s_chunk_pieces(c):
        r = slice(c * L, (c + 1) * L)
        ctx = {}

        def decays():
            dt_c = S.dt[r, :]
            dtT_c = S.dtT[:, r]
            parts = _dot(tril_ref[...], jnp.concatenate(_split3(dt_c * a_row), axis=1))
            acum = parts[:, 0:LANES] + parts[:, LANES:2 * LANES] + parts[:, 2 * LANES:3 * LANES]
            partsT = _dot(jnp.concatenate(_split3(dtT_c * a_col), axis=0), triu_ref[...])
            acumT = (partsT[0:SSM_HEADS] + partsT[SSM_HEADS:2 * SSM_HEADS]
                     + partsT[2 * SSM_HEADS:3 * SSM_HEADS])
            ctx["acum"] = acum
            ctx["lrow"] = acumT - jnp.log(dtT_c)
            e = jnp.exp(acum)
            w = jnp.exp(acum[L - 1:L, :] - acum) * dt_c
            e_hi = e.astype(BF16).astype(F32)
            w_hi = w.astype(BF16).astype(F32)
            packed = jnp.where(lane < SSM_HEADS, e_hi,
                               jnp.where(lane < 2 * SSM_HEADS, e - e_hi,
                                         jnp.where(lane < 3 * SSM_HEADS, w_hi,
                                                   jnp.where(lane < 4 * SSM_HEADS, w - w_hi, 0.0))))
            ex = _dot(packed.astype(BF16), expand_ref[...])
            ctx["ex_e"] = ex[:, 0:SSM_WIDTH]
            ctx["xw"] = (S.xs[r, :] * ex[:, SSM_WIDTH:2 * SSM_WIDTH]).astype(BF16)

        def group_open(gi):
            def piece():
                gsl = slice(gi * GROUP_WIDTH, (gi + 1) * GROUP_WIDTH)
                b_g = S.bc[r, gi * SSM_STATE:(gi + 1) * SSM_STATE]
                c_g = S.bc[r, BC_WIDTH + gi * SSM_STATE:BC_WIDTH + (gi + 1) * SSM_STATE]
                ctx["cb"] = _dot_nt(c_g, b_g)
                st = sstate_ref[gi]
                ctx["y_inter"] = _dot(c_g, st.astype(BF16)) * ctx["ex_e"][:, gsl]
                sstate_ref[gi] = (st * ctx["ex_e"][L - 1:L, gsl]
                                  + _dot(S.bT[gi * SSM_STATE:(gi + 1) * SSM_STATE, r], ctx["xw"][:, gsl]))
            return piece

        def head_pair(gi, p):
            def piece():
                h0 = gi * HEADS_PER_GROUP + 2 * p
                psl = slice(h0 * SSM_HEAD_DIM, (h0 + 2) * SSM_HEAD_DIM)
                gmats = []
                for h in (h0, h0 + 1):
                    diff = ctx["acum"][:, h:h + 1] - ctx["lrow"][h:h + 1, :]
                    gmats.append(ctx["cb"] * jnp.exp(jnp.where(causal, diff, -jnp.inf)))
                lhs = jnp.concatenate(gmats, axis=1).astype(BF16)
                xp = S.xs[r, psl]
                rhs = jnp.concatenate([jnp.where(low_half, xp, 0.0),
                                       jnp.where(low_half, 0.0, xp)], axis=0).astype(BF16)
                y = (_dot(lhs, rhs) + ctx["y_inter"][:, 2 * p * SSM_HEAD_DIM:(2 * p + 2) * SSM_HEAD_DIM]
                     + xp * dskip_ref[:, psl])
                y_ref[:, psl] = y * S.zs[r, psl]
            return piece

        def ssd_norm():
            cat_ref[r, 0:SSM_WIDTH] = _rms(y_ref[...], ssmnorm_ref[...]).astype(BF16)

        def retention(h):
            def piece():
                qsl = slice(h * RET_QK_DIM, (h + 1) * RET_QK_DIM)
                vsl = slice(h * RET_V_DIM, (h + 1) * RET_V_DIM)
                vh = S.v[r, vsl]
                sc = _dot_nt(S.q[r, qsl], S.k[r, qsl]) * dmat_ref[h]
                st = rstate_ref[h]
                lhs = jnp.concatenate([sc.astype(BF16), S.qd[r, qsl]], axis=1)
                rhs = jnp.concatenate([vh, st.astype(BF16)], axis=0)
                rh = _dot(lhs, rhs)
                rstate_ref[h] = st * cdec_ref[:, vsl] + _dot(S.kdT[qsl, r], vh)
                d = rh - jnp.mean(rh, axis=-1, keepdims=True)
                var = jnp.mean(d * d, axis=-1, keepdims=True)
                rn = d * lax.rsqrt(var + RET_NORM_EPS) * retnorm_ref[:, vsl]
                cat_ref[r, SSM_WIDTH + h * RET_V_DIM:SSM_WIDTH + (h + 1) * RET_V_DIM] = (
                    S.gs[r, vsl] * rn).astype(BF16)
            return piece

        pieces = [decays]
        for gi in range(SSM_GROUPS):
            pieces.append(group_open(gi))
            pieces += [head_pair(gi, p) for p in range(HEADS_PER_GROUP // 2)]
        pieces.append(ssd_norm)
        pieces += [retention(h) for h in range(RET_HEADS)]
        return pieces

    def s_out(j):
        def piece():
            sl = slice(j * OUT_SLAB, (j + 1) * OUT_SLAB)
            mixed_ref[:, sl] = _dot(cat_ref[...], wout_ref[:, sl])
        return piece

    def s_residual():
        o_ref[...] = S.x[...] + _rms(mixed_ref[...], post_ref[...])

    gates = ([p_gate(P.zs, _Z0, j) for j in range(SSM_WIDTH // SLAB)]
             + [p_gate(P.gs, _G0, j) for j in range(RET_WIDTH // SLAB)]
             + [p_rotary(j) for j in range(RET_QK_WIDTH // SLAB)] + [p_dt])
    convs = [p_conv(j) for j in range(CONV_DIM // SLAB)]
    early = _round_robin(gates, convs)
    s_scan = [piece for c in range(tb // L) for piece in s_chunk_pieces(c)]
    p_norm()
    done = 0
    for i, piece in enumerate(early):
        piece()
        upto = (i + 1) * len(s_scan) // len(early)
        for sp in s_scan[done:upto]:
            sp()
        done = upto
    outs = [s_out(j) for j in range(D_MODEL // OUT_SLAB)]
    values = [p_value(j) for j in range(RET_WIDTH // SLAB)]
    for piece in _round_robin(outs, values):
        piece()
    s_residual()


def _retention_tables(tb):
    log_gamma = jnp.log(1.0 - 2.0 ** (-5.0 - jnp.arange(RET_HEADS, dtype=F32)))
    pos = jnp.arange(CHUNK, dtype=F32)
    rel = pos[:, None] - pos[None, :]
    dmat = jnp.exp(jnp.where((rel >= 0)[None], rel[None] * log_gamma[:, None, None], -jnp.inf))
    q_decay = jnp.exp((pos[:, None] + 1.0) * log_gamma[None])
    k_decay = jnp.exp((CHUNK - 1.0 - pos[:, None]) * log_gamma[None])
    chunk_decay = jnp.exp(CHUNK * log_gamma)
    qdec = jnp.tile(jnp.repeat(q_decay, RET_QK_DIM, axis=1), (tb // CHUNK, 1))
    kdec = jnp.tile(jnp.repeat(k_decay, RET_QK_DIM, axis=1), (tb // CHUNK, 1))
    cdec = jnp.repeat(chunk_decay, RET_V_DIM)[None, :]
    return dmat, qdec, kdec, cdec


def _constant_matrices():
    e = np.zeros((LANES, 2 * SSM_WIDTH), np.float32)
    for g in range(DT_COPIES):
        for h in range(SSM_HEADS):
            c0 = (g // 2) * SSM_WIDTH + h * SSM_HEAD_DIM
            e[g * SSM_HEADS + h, c0:c0 + SSM_HEAD_DIM] = 1.0
    tril = np.tril(np.ones((CHUNK, CHUNK), np.float32))
    return jnp.asarray(e, BF16), jnp.asarray(tril, BF16), jnp.asarray(tril.T, BF16)


def _slot_scratch(tb):
    return list(_Slot(
        x=pltpu.VMEM((tb, D_MODEL), F32),
        zs=pltpu.VMEM((tb, SSM_WIDTH), F32),
        xs=pltpu.VMEM((tb, SSM_WIDTH), F32),
        bc=pltpu.VMEM((tb, 2 * BC_WIDTH), BF16),
        bT=pltpu.VMEM((BC_WIDTH, tb), BF16),
        q=pltpu.VMEM((tb, RET_QK_WIDTH), BF16),
        qd=pltpu.VMEM((tb, RET_QK_WIDTH), BF16),
        k=pltpu.VMEM((tb, RET_QK_WIDTH), BF16),
        kdT=pltpu.VMEM((RET_QK_WIDTH, tb), BF16),
        v=pltpu.VMEM((tb, RET_WIDTH), BF16),
        gs=pltpu.VMEM((tb, RET_WIDTH), F32),
        dt=pltpu.VMEM((tb, LANES), F32),
        dtT=pltpu.VMEM((SSM_HEADS, tb), F32),
    ))


def _mixer(x2d, batch, seq, cos, sin, pre, w_main, w_dt_pad, conv_w, conv_b, dt_bias, a_log, d_skip,
           ssm_norm, ret_norm, w_out, post):
    tb = min(MIX_BLOCK, seq)
    nj = seq // tb
    nblocks = batch * nj
    lane_pad = (0, LANES - DT_COPIES * SSM_HEADS)
    w_dt_t = w_dt_pad[:, 0:SSM_HEADS].T
    dtb_row = jnp.pad(jnp.tile(dt_bias, DT_COPIES), lane_pad)[None, :]
    alog_row = jnp.pad(jnp.tile(a_log, DT_COPIES), lane_pad)[None, :]
    dskip = jnp.repeat(d_skip, SSM_HEAD_DIM)[None, :]
    dmat, qdec, kdec, cdec = _retention_tables(tb)
    expand, tril, triu = _constant_matrices()

    g = MIX_GROUP
    assert nj % g == 0, "a grid step's blocks must not straddle two sequences"

    def block(k):
        return lambda i: (jnp.clip(g * i - g + 1 + k, 0, nblocks - 1), 0)

    prev = lambda i: (jnp.maximum(i - 1, 0), 0)
    operands = (
        [(x2d, pl.BlockSpec((tb, D_MODEL), block(k))) for k in range(g)]
        + [(cos, pl.BlockSpec((tb, LANES), block(k))) for k in range(g)]
        + [(sin, pl.BlockSpec((tb, LANES), block(k))) for k in range(g)])
    consts = [
        (pre[None, :], None),
        (w_main, _resident_block(w_main.shape)),
        (w_dt_pad, None),
        (w_dt_t, None),
        (conv_w, None),
        (conv_b[None, :], None),
        (dtb_row, None),
        (dt_bias[:, None], None),
        (alog_row, None),
        (a_log[:, None], None),
        (dskip, None),
        (ssm_norm[None, :], None),
        (ret_norm[None, :], None),
        (w_out, _resident_block(w_out.shape)),
        (post[None, :], None),
        (expand, None),
        (tril, None),
        (triu, None),
        (dmat, None),
        (qdec, None),
        (kdec, None),
        (cdec, None),
    ]
    assert len(consts) == N_MIX_CONSTS
    operands += consts
    args = [a for a, _ in operands]
    in_specs = [sp if sp is not None else _resident(a.shape) for a, sp in operands]
    return pl.pallas_call(
        functools.partial(_mixer_kernel, tb=tb, nj=nj),
        grid=(nblocks // g + 1,),
        in_specs=in_specs,
        out_specs=pl.BlockSpec((g * tb, D_MODEL), prev),
        out_shape=jax.ShapeDtypeStruct(x2d.shape, F32),
        scratch_shapes=[
            pltpu.VMEM((tb, D_MODEL), BF16),
            pltpu.VMEM((SUBLANES + tb, CONV_DIM), F32),
            pltpu.VMEM((CHUNK, SSM_WIDTH), F32),
            pltpu.VMEM((tb, D_INNER), BF16),
            pltpu.VMEM((tb, D_MODEL), F32),
            pltpu.VMEM((SSM_GROUPS, SSM_STATE, GROUP_WIDTH), F32),
            pltpu.VMEM((RET_HEADS, RET_QK_DIM, RET_V_DIM), F32),
        ] + 2 * _slot_scratch(tb),
        compiler_params=pltpu.CompilerParams(
            dimension_semantics=("arbitrary",), vmem_limit_bytes=VMEM_LIMIT),
        name="mixer",
    )(*args)


def kernel(x, positions, ffn1_pre_norm, ffn1_w_in, ffn1_w_out, ffn1_post_norm, mix_pre_norm, mix_w_in, conv_w, conv_b, dt_bias, a_log, d_skip, ssm_norm, ret_norm, mix_w_out, mix_post_norm, ffn2_pre_norm, ffn2_w_in, ffn2_w_out, ffn2_post_norm):
    batch, seq, _ = x.shape
    depth = ffn1_w_in.shape[0]

    def cast_job(w, layer):
        return _PrepJob(w, layer, _cast_kernel, w.shape[2])

    cos, sin, f1_in, f1_out = _rope_tables(positions, [cast_job(ffn1_w_in, 0), cast_job(ffn1_w_out, 0)])
    h = x.reshape(batch * seq, D_MODEL)
    for l in range(depth):
        jobs = [_PrepJob(mix_w_in, l, _mix_in_kernel, _G1), _PrepJob(mix_w_in, l, _mix_dt_kernel, LANES),
                cast_job(mix_w_out, l), cast_job(ffn2_w_in, l), cast_job(ffn2_w_out, l)]
        h, m_in, m_dt, m_out, f2_in, f2_out = _ffn(h, ffn1_pre_norm[l], f1_in, f1_out, ffn1_post_norm[l],
                                                   jobs)
        h = _mixer(h, batch, seq, cos, sin, mix_pre_norm[l], m_in, m_dt, conv_w[l], conv_b[l],
                   dt_bias[l], a_log[l], d_skip[l], ssm_norm[l], ret_norm[l], m_out, mix_post_norm[l])
        jobs = [cast_job(ffn1_w_in, l + 1), cast_job(ffn1_w_out, l + 1)] if l + 1 < depth else []
        h, *prepared = _ffn(h, ffn2_pre_norm[l], f2_in, f2_out, ffn2_post_norm[l], jobs)
        if prepared:
            f1_in, f1_out = prepared
    return h.reshape(batch, seq, D_MODEL)
```

```python
import collections
import functools

import jax
import jax.numpy as jnp
import numpy as np
from jax import lax
from jax.experimental import pallas as pl
from jax.experimental.pallas import tpu as pltpu

F32 = jnp.float32
BF16 = jnp.bfloat16

D_MODEL = 1024
D_FF = 2816
NORM_EPS = 1e-6
RET_NORM_EPS = 1e-5

D_INNER = 2 * D_MODEL
SSM_WIDTH = D_INNER // 2
RET_WIDTH = D_INNER - SSM_WIDTH
SSM_HEAD_DIM = 64
SSM_HEADS = SSM_WIDTH // SSM_HEAD_DIM
SSM_GROUPS = 2
HEADS_PER_GROUP = SSM_HEADS // SSM_GROUPS
GROUP_WIDTH = HEADS_PER_GROUP * SSM_HEAD_DIM
SSM_STATE = 128
BC_WIDTH = SSM_GROUPS * SSM_STATE
CONV_K = 4
CONV_DIM = SSM_WIDTH + 2 * BC_WIDTH
RET_HEADS = 4
RET_V_DIM = RET_WIDTH // RET_HEADS
RET_QK_DIM = RET_V_DIM // 2
RET_QK_WIDTH = RET_HEADS * RET_QK_DIM
ROPE_BASE = 10000.0

LANES = 128
SUBLANES = 8
BF16_ROWS = 16

CHUNK = 128
MIX_BLOCK = 256
MIX_GROUP = 2
N_MIX_CONSTS = 22
FFN_BLOCK = 1024
FFN_COL_CHUNK = 256
FFN_ROW_GROUPS = 4
ROPE_BLOCK = 2048
VMEM_LIMIT = 58 * 1024 * 1024
DT_COPIES = 4
SLAB = 256
OUT_SLAB = 512

_Z0, _Z1 = 0, SSM_WIDTH
_X0, _X1 = _Z1, _Z1 + CONV_DIM
_Q0, _Q1 = _X1, _X1 + RET_QK_WIDTH
_K0, _K1 = _Q1, _Q1 + RET_QK_WIDTH
_V0, _V1 = _K1, _K1 + RET_WIDTH
_G0, _G1 = _V1, _V1 + RET_WIDTH


def _resident(shape):
    del shape
    return pl.BlockSpec(memory_space=pltpu.VMEM)


def _resident_block(shape):
    nd = len(shape)
    return pl.BlockSpec(shape, lambda *_: (0,) * nd, pipeline_mode=pl.Buffered(1))


def _silu(x):
    h = 0.5 * x
    return h + h * jnp.tanh(h)


def _softplus(x):
    return jnp.maximum(x, 0.0) + jnp.log1p(jnp.exp(-jnp.abs(x)))


def _rms(x, w):
    ms = jnp.mean(x * x, axis=-1, keepdims=True)
    return x * lax.rsqrt(ms + NORM_EPS) * w


def _dot(a, b):
    return jnp.dot(a, b, preferred_element_type=F32)


def _dot_nt(a, b):
    return lax.dot_general(a, b, (((1,), (1,)), ((), ())), preferred_element_type=F32)


def _round_robin(*queues):
    queues = [list(q) for q in queues]
    merged = []
    while any(queues):
        for q in queues:
            if q:
                merged.append(q.pop(0))
    return merged


def _split3(v):
    p1 = v.astype(BF16)
    r1 = v - p1.astype(F32)
    p2 = r1.astype(BF16)
    p3 = (r1 - p2.astype(F32)).astype(BF16)
    return p1, p2, p3


def _cast_kernel(w_ref, o_ref):
    o_ref[...] = w_ref[...].astype(BF16)


def _mix_in_kernel(w_ref, o_ref):
    o_ref[:, 0:_X1] = w_ref[:, 0:_X1].astype(BF16)
    o_ref[:, _X1:_G1] = w_ref[:, _X1 + SSM_HEADS:_G1 + SSM_HEADS].astype(BF16)


def _mix_dt_kernel(w_ref, o_ref):
    lane = lax.broadcasted_iota(jnp.int32, (w_ref.shape[0], LANES), 1)
    d = jnp.where(lane < SSM_HEADS, w_ref[:, _X1:_X1 + LANES], 0.0)
    out = d
    for c in range(1, DT_COPIES):
        out = out + pltpu.roll(d, c * SSM_HEADS, 1)
    o_ref[...] = out.astype(BF16)


_PrepJob = collections.namedtuple("_PrepJob", ["w", "layer", "body", "out_cols"])


def _job_specs(job, n_steps):
    _, k, n = job.w.shape
    nb = next(c for c in range(min(n_steps, k // BF16_ROWS), 0, -1)
              if k % c == 0 and (k // c) % BF16_ROWS == 0)
    rows = k // nb
    layer = job.layer
    return (pl.BlockSpec((None, rows, n), lambda i: (layer, i * nb // n_steps, 0)),
            pl.BlockSpec((rows, job.out_cols), lambda i: (i * nb // n_steps, 0)),
            jax.ShapeDtypeStruct((k, job.out_cols), BF16))


def _rope_kernel(*refs, job_bodies):
    n_jobs = len(job_bodies)
    pos_ref, invf_ref = refs[:2]
    job_in = refs[2:2 + n_jobs]
    cos_ref, sin_ref = refs[2 + n_jobs:4 + n_jobs]
    job_out = refs[4 + n_jobs:]
    ang = invf_ref[...] * pos_ref[...].astype(F32)
    c = jnp.cos(ang)
    s = jnp.sin(ang)
    cos_ref[...] = jnp.concatenate([c, c], axis=0).T
    sin_ref[...] = jnp.concatenate([-s, s], axis=0).T
    for body, w_ref, out_ref in zip(job_bodies, job_in, job_out):
        body(w_ref, out_ref)


def _rope_tables(positions, jobs=()):
    t = positions.size
    tb = min(ROPE_BLOCK, t)
    assert t % tb == 0
    n_steps = t // tb
    half = RET_QK_DIM // 2
    inv_freq = ROPE_BASE ** (-jnp.arange(half, dtype=F32) / half)
    job_specs = [_job_specs(job, n_steps) for job in jobs]
    table = pl.BlockSpec((tb, LANES), lambda i: (i, 0))
    return pl.pallas_call(
        functools.partial(_rope_kernel, job_bodies=tuple(job.body for job in jobs)),
        grid=(n_steps,),
        in_specs=[pl.BlockSpec((None, 1, tb), lambda i: (i, 0, 0)),
                  pl.BlockSpec((half, 1), lambda i: (0, 0))] + [js[0] for js in job_specs],
        out_specs=[table, table] + [js[1] for js in job_specs],
        out_shape=[jax.ShapeDtypeStruct((t, LANES), F32)] * 2 + [js[2] for js in job_specs],
        compiler_params=pltpu.CompilerParams(
            dimension_semantics=("arbitrary",), vmem_limit_bytes=VMEM_LIMIT),
        name="rope_tables",
    )(positions.reshape(n_steps, 1, tb), inv_freq[:, None], *[job.w for job in jobs])


def _ffn_kernel(*refs, job_bodies):
    n_jobs = len(job_bodies)
    x_ref, pre_ref, win_ref, wout_ref, post_ref = refs[:5]
    job_in = refs[5:5 + n_jobs]
    o_ref = refs[5 + n_jobs]
    job_out = refs[6 + n_jobs:6 + 2 * n_jobs]
    xn_ref, act_ref = refs[6 + 2 * n_jobs:]

    tm = x_ref.shape[0]
    rows = tm // FFN_ROW_GROUPS
    groups = [slice(i * rows, (i + 1) * rows) for i in range(FFN_ROW_GROUPS)]
    slabs = list(range(0, D_FF, FFN_COL_CHUNK))

    def prenorm(r):
        xn_ref[r, :] = _rms(x_ref[r, :], pre_ref[...]).astype(BF16)

    def swiglu(r, lo):
        gate = _dot(xn_ref[r, :], win_ref[:, lo:lo + FFN_COL_CHUNK])
        up = _dot(xn_ref[r, :], win_ref[:, D_FF + lo:D_FF + lo + FFN_COL_CHUNK])
        act_ref[r, lo:lo + FFN_COL_CHUNK] = (_silu(gate) * up).astype(BF16)

    def project(r):
        h = _dot(act_ref[r, :], wout_ref[...])
        o_ref[r, :] = x_ref[r, :] + 0.5 * _rms(h, post_ref[...])

    prenorm(groups[0])
    for i, r in enumerate(groups):
        swiglu(r, slabs[0])
        if i + 1 < len(groups):
            prenorm(groups[i + 1])
        if i > 0:
            project(groups[i - 1])
        else:
            for body, w_ref, out_ref in zip(job_bodies, job_in, job_out):
                body(w_ref, out_ref)
        for lo in slabs[1:]:
            swiglu(r, lo)
    project(groups[-1])


def _ffn(x2d, pre, w_in, w_out, post, jobs=()):
    t = x2d.shape[0]
    tm = min(FFN_BLOCK, t)
    n_steps = t // tm
    job_specs = [_job_specs(job, n_steps) for job in jobs]
    row = pl.BlockSpec((tm, D_MODEL), lambda i: (i, 0))
    return pl.pallas_call(
        functools.partial(_ffn_kernel, job_bodies=tuple(job.body for job in jobs)),
        grid=(n_steps,),
        in_specs=[row, _resident((1, D_MODEL)), _resident_block(w_in.shape),
                  _resident_block(w_out.shape), _resident((1, D_MODEL))] + [js[0] for js in job_specs],
        out_specs=[row] + [js[1] for js in job_specs],
        out_shape=[jax.ShapeDtypeStruct((t, D_MODEL), F32)] + [js[2] for js in job_specs],
        scratch_shapes=[pltpu.VMEM((tm, D_MODEL), BF16),
                        pltpu.VMEM((tm, D_FF), BF16)],
        compiler_params=pltpu.CompilerParams(
            dimension_semantics=("arbitrary",), vmem_limit_bytes=VMEM_LIMIT),
        name="ffn",
    )(x2d, pre[None, :], w_in, w_out, post[None, :], *[job.w for job in jobs])


_Slot = collections.namedtuple(
    "_Slot", ["x", "zs", "xs", "bc", "bT", "q", "qd", "k", "kdT", "v", "gs", "dt", "dtT",
              "acum", "lrow", "ex"])


def _mixer_kernel(*refs, tb, nj):
    g = MIX_GROUP
    x_refs = refs[0:g]
    cos_refs = refs[g:2 * g]
    sin_refs = refs[2 * g:3 * g]
    consts = refs[3 * g:3 * g + N_MIX_CONSTS]
    o_ref = refs[3 * g + N_MIX_CONSTS]
    scratch = refs[3 * g + N_MIX_CONSTS + 1:]
    u_ref, xbc_ref, y_ref, cat_ref, mixed_ref, sstate_ref, rstate_ref = scratch[:7]
    n = len(_Slot._fields)
    slot_a = _Slot(*scratch[7:7 + n])
    slot_b = _Slot(*scratch[7 + n:7 + 2 * n])
    i = pl.program_id(0)

    @pl.when(i == 0)
    def _():
        for ref in slot_a:
            ref[...] = jnp.zeros(ref.shape, ref.dtype)
        xbc_ref[0:SUBLANES, :] = jnp.zeros((SUBLANES, CONV_DIM), F32)

    @pl.when(jnp.logical_or(i == 0, lax.rem(g * i + nj - g, nj) == 0))
    def _():
        sstate_ref[...] = jnp.zeros(sstate_ref.shape, F32)
        rstate_ref[...] = jnp.zeros(rstate_ref.shape, F32)

    shared = consts + (u_ref, xbc_ref, y_ref, cat_ref, mixed_ref, sstate_ref, rstate_ref)
    for k in range(g):
        p_slot, s_slot = (slot_b, slot_a) if k % 2 == 0 else (slot_a, slot_b)
        new_sequence = (lax.rem(g * i, nj) == 0) if k == g - 1 else None
        _mixer_step(x_refs[k], cos_refs[k], sin_refs[k], o_ref.at[pl.ds(k * tb, tb)], *shared,
                    p_slot, s_slot, new_sequence, tb=tb)


def _mixer_step(xp_ref, cos_ref, sin_ref, o_ref, pre_ref, wmain_ref, wdt_ref, wdtT_ref,
                convw_ref, convb_ref, dtb_row_ref, dtb_col_ref, alog_row_ref, alog_col_ref, dskip_ref,
                ssmnorm_ref, retnorm_ref, wout_ref, post_ref, expand_ref, tril_ref, triu_ref,
                dmat_ref, qdec_ref, kdec_ref, cdec_ref,
                u_ref, xbc_ref, y_ref, cat_ref, mixed_ref, sstate_ref, rstate_ref, P, S, new_sequence,
                *, tb):
    L = CHUNK
    if new_sequence is not None:
        tail = xbc_ref[0:SUBLANES, :]
        xbc_ref[0:SUBLANES, :] = jnp.where(new_sequence, jnp.zeros_like(tail), tail)

    def proj(lo, hi):
        return _dot(u_ref[...], wmain_ref[:, lo:hi])

    def p_norm():
        x = xp_ref[...]
        P.x[...] = x
        u_ref[...] = _rms(x, pre_ref[...]).astype(BF16)

    def p_gate(dst, base, j):
        def piece():
            sl = slice(j * SLAB, (j + 1) * SLAB)
            t = proj(base + j * SLAB, base + (j + 1) * SLAB)
            dst[:, sl] = _silu(t)
        return piece

    def p_value(j):
        def piece():
            sl = slice(j * SLAB, (j + 1) * SLAB)
            P.v[:, sl] = proj(_V0 + j * SLAB, _V0 + (j + 1) * SLAB).astype(BF16)
        return piece

    def p_rotary(j):
        def piece():
            q = proj(_Q0 + j * SLAB, _Q0 + (j + 1) * SLAB)
            k = proj(_K0 + j * SLAB, _K0 + (j + 1) * SLAB)
            cos = cos_ref[...]
            sin = sin_ref[...]
            for hh in range(SLAB // RET_QK_DIM):
                loc = slice(hh * RET_QK_DIM, (hh + 1) * RET_QK_DIM)
                sl = slice(j * SLAB + hh * RET_QK_DIM, j * SLAB + (hh + 1) * RET_QK_DIM)
                qh = q[:, loc]
                kh = k[:, loc]
                qr = qh * cos + pltpu.roll(qh, RET_QK_DIM // 2, 1) * sin
                kr = (kh * cos + pltpu.roll(kh, RET_QK_DIM // 2, 1) * sin) * (RET_QK_DIM ** -0.5)
                P.q[:, sl] = qr.astype(BF16)
                P.qd[:, sl] = (qr * qdec_ref[:, sl]).astype(BF16)
                P.k[:, sl] = kr.astype(BF16)
                P.kdT[sl, :] = (kr * kdec_ref[:, sl]).T.astype(BF16)
        return piece

    def p_dt():
        P.dt[...] = _softplus(_dot(u_ref[...], wdt_ref[...]) + dtb_row_ref[...])
        P.dtT[...] = _softplus(_dot_nt(wdtT_ref[...], u_ref[...]) + dtb_col_ref[...])

    def p_conv(j):
        def piece():
            sl = slice(j * SLAB, (j + 1) * SLAB)
            xbc_ref[SUBLANES:SUBLANES + tb, sl] = proj(_X0 + j * SLAB, _X0 + (j + 1) * SLAB)
            rows = xbc_ref[0:SUBLANES + tb, sl]
            conv = convb_ref[:, sl] + convw_ref[CONV_K - 1:CONV_K, sl] * rows[SUBLANES:, :]
            for kk in range(CONV_K - 1):
                shifted = pltpu.roll(rows, CONV_K - 1 - kk, 0)
                conv = conv + convw_ref[kk:kk + 1, sl] * shifted[SUBLANES:, :]
            act = _silu(conv)
            xbc_ref[0:SUBLANES, sl] = xbc_ref[tb:tb + SUBLANES, sl]
            if (j + 1) * SLAB <= SSM_WIDTH:
                P.xs[:, sl] = act
            else:
                lo = j * SLAB - SSM_WIDTH
                P.bc[:, lo:lo + SLAB] = act.astype(BF16)
                if lo < BC_WIDTH:
                    nb = min(SLAB, BC_WIDTH - lo)
                    P.bT[lo:lo + nb, :] = act[:, 0:nb].T.astype(BF16)
        return piece

    a_row = -jnp.exp(alog_row_ref[...])
    a_col = -jnp.exp(alog_col_ref[...])
    rows = lax.broadcasted_iota(jnp.int32, (L, L), 0)
    cols = lax.broadcasted_iota(jnp.int32, (L, L), 1)
    causal = rows >= cols
    lane = lax.broadcasted_iota(jnp.int32, (L, LANES), 1)
    low_half = lane < SSM_HEAD_DIM

    def p_decays(c):
        def piece():
            r = slice(c * L, (c + 1) * L)
            dt_c = P.dt[r, :]
            dtT_c = P.dtT[:, r]
            parts = _dot(tril_ref[...], jnp.concatenate(_split3(dt_c * a_row), axis=1))
            acum = parts[:, 0:LANES] + parts[:, LANES:2 * LANES] + parts[:, 2 * LANES:3 * LANES]
            partsT = _dot(jnp.concatenate(_split3(dtT_c * a_col), axis=0), triu_ref[...])
            acumT = (partsT[0:SSM_HEADS] + partsT[SSM_HEADS:2 * SSM_HEADS]
                     + partsT[2 * SSM_HEADS:3 * SSM_HEADS])
            P.acum[r, :] = acum
            P.lrow[:, r] = acumT - jnp.log(dtT_c)
            e = jnp.exp(acum)
            w = jnp.exp(acum[L - 1:L, :] - acum) * dt_c
            e_hi = e.astype(BF16).astype(F32)
            w_hi = w.astype(BF16).astype(F32)
            packed = jnp.where(lane < SSM_HEADS, e_hi,
                               jnp.where(lane < 2 * SSM_HEADS, e - e_hi,
                                         jnp.where(lane < 3 * SSM_HEADS, w_hi,
                                                   jnp.where(lane < 4 * SSM_HEADS, w - w_hi, 0.0))))
            P.ex[r, :] = _dot(packed.astype(BF16), expand_ref[...])
        return piece

    def s_chunk_pieces(c):
        r = slice(c * L, (c + 1) * L)
        ctx = {}

        def group_open(gi):
            def piece():
                gsl = slice(gi * GROUP_WIDTH, (gi + 1) * GROUP_WIDTH)
                b_g = S.bc[r, gi * SSM_STATE:(gi + 1) * SSM_STATE]
                c_g = S.bc[r, BC_WIDTH + gi * SSM_STATE:BC_WIDTH + (gi + 1) * SSM_STATE]
                ctx["cb"] = _dot_nt(c_g, b_g)
                st = sstate_ref[gi]
                ex_e = S.ex[r, gsl]
                xw = (S.xs[r, gsl] * S.ex[r, SSM_WIDTH + gi * GROUP_WIDTH:SSM_WIDTH + (gi + 1) * GROUP_WIDTH]
                      ).astype(BF16)
                ctx["y_inter"] = _dot(c_g, st.astype(BF16)) * ex_e
                sstate_ref[gi] = (st * ex_e[L - 1:L, :]
                                  + _dot(S.bT[gi * SSM_STATE:(gi + 1) * SSM_STATE, r], xw))
            return piece

        def head_pair(gi, p):
            def piece():
                h0 = gi * HEADS_PER_GROUP + 2 * p
                psl = slice(h0 * SSM_HEAD_DIM, (h0 + 2) * SSM_HEAD_DIM)
                gmats = []
                acum = S.acum[r, :]
                lrow = S.lrow[:, r]
                for h in (h0, h0 + 1):
                    diff = acum[:, h:h + 1] - lrow[h:h + 1, :]
                    gmats.append(ctx["cb"] * jnp.exp(jnp.where(causal, diff, -jnp.inf)))
                lhs = jnp.concatenate(gmats, axis=1).astype(BF16)
                xp = S.xs[r, psl]
                rhs = jnp.concatenate([jnp.where(low_half, xp, 0.0),
                                       jnp.where(low_half, 0.0, xp)], axis=0).astype(BF16)
                y = (_dot(lhs, rhs) + ctx["y_inter"][:, 2 * p * SSM_HEAD_DIM:(2 * p + 2) * SSM_HEAD_DIM]
                     + xp * dskip_ref[:, psl])
                y_ref[:, psl] = y * S.zs[r, psl]
            return piece

        def ssd_norm():
            cat_ref[r, 0:SSM_WIDTH] = _rms(y_ref[...], ssmnorm_ref[...]).astype(BF16)

        def retention(h):
            def piece():
                qsl = slice(h * RET_QK_DIM, (h + 1) * RET_QK_DIM)
                vsl = slice(h * RET_V_DIM, (h + 1) * RET_V_DIM)
                vh = S.v[r, vsl]
                sc = _dot_nt(S.q[r, qsl], S.k[r, qsl]) * dmat_ref[h]
                st = rstate_ref[h]
                lhs = jnp.concatenate([sc.astype(BF16), S.qd[r, qsl]], axis=1)
                rhs = jnp.concatenate([vh, st.astype(BF16)], axis=0)
                rh = _dot(lhs, rhs)
                rstate_ref[h] = st * cdec_ref[:, vsl] + _dot(S.kdT[qsl, r], vh)
                d = rh - jnp.mean(rh, axis=-1, keepdims=True)
                var = jnp.mean(d * d, axis=-1, keepdims=True)
                rn = d * lax.rsqrt(var + RET_NORM_EPS) * retnorm_ref[:, vsl]
                cat_ref[r, SSM_WIDTH + h * RET_V_DIM:SSM_WIDTH + (h + 1) * RET_V_DIM] = (
                    S.gs[r, vsl] * rn).astype(BF16)
            return piece

        pieces = []
        for gi in range(SSM_GROUPS):
            pieces.append(group_open(gi))
            pieces += [head_pair(gi, p) for p in range(HEADS_PER_GROUP // 2)]
        pieces.append(ssd_norm)
        pieces += [retention(h) for h in range(RET_HEADS)]
        return pieces

    def s_out(j):
        def piece():
            sl = slice(j * OUT_SLAB, (j + 1) * OUT_SLAB)
            mixed_ref[:, sl] = _dot(cat_ref[...], wout_ref[:, sl])
        return piece

    def s_residual():
        o_ref[...] = S.x[...] + _rms(mixed_ref[...], post_ref[...])

    gates = ([p_dt] + [p_decays(c) for c in range(tb // L)]
             + [p_gate(P.zs, _Z0, j) for j in range(SSM_WIDTH // SLAB)]
             + [p_gate(P.gs, _G0, j) for j in range(RET_WIDTH // SLAB)]
             + [p_rotary(j) for j in range(RET_QK_WIDTH // SLAB)])
    convs = [p_conv(j) for j in range(CONV_DIM // SLAB)]
    early = _round_robin(gates, convs)
    s_scan = [piece for c in range(tb // L) for piece in s_chunk_pieces(c)]
    p_norm()
    done = 0
    for i, piece in enumerate(early):
        piece()
        upto = (i + 1) * len(s_scan) // len(early)
        for sp in s_scan[done:upto]:
            sp()
        done = upto
    outs = [s_out(j) for j in range(D_MODEL // OUT_SLAB)]
    values = [p_value(j) for j in range(RET_WIDTH // SLAB)]
    for piece in _round_robin(outs, values):
        piece()
    s_residual()


def _retention_tables(tb):
    log_gamma = jnp.log(1.0 - 2.0 ** (-5.0 - jnp.arange(RET_HEADS, dtype=F32)))
    pos = jnp.arange(CHUNK, dtype=F32)
    rel = pos[:, None] - pos[None, :]
    dmat = jnp.exp(jnp.where((rel >= 0)[None], rel[None] * log_gamma[:, None, None], -jnp.inf))
    q_decay = jnp.exp((pos[:, None] + 1.0) * log_gamma[None])
    k_decay = jnp.exp((CHUNK - 1.0 - pos[:, None]) * log_gamma[None])
    chunk_decay = jnp.exp(CHUNK * log_gamma)
    qdec = jnp.tile(jnp.repeat(q_decay, RET_QK_DIM, axis=1), (tb // CHUNK, 1))
    kdec = jnp.tile(jnp.repeat(k_decay, RET_QK_DIM, axis=1), (tb // CHUNK, 1))
    cdec = jnp.repeat(chunk_decay, RET_V_DIM)[None, :]
    return dmat, qdec, kdec, cdec


def _constant_matrices():
    e = np.zeros((LANES, 2 * SSM_WIDTH), np.float32)
    for g in range(DT_COPIES):
        for h in range(SSM_HEADS):
            c0 = (g // 2) * SSM_WIDTH + h * SSM_HEAD_DIM
            e[g * SSM_HEADS + h, c0:c0 + SSM_HEAD_DIM] = 1.0
    tril = np.tril(np.ones((CHUNK, CHUNK), np.float32))
    return jnp.asarray(e, BF16), jnp.asarray(tril, BF16), jnp.asarray(tril.T, BF16)


def _slot_scratch(tb):
    return list(_Slot(
        x=pltpu.VMEM((tb, D_MODEL), F32),
        zs=pltpu.VMEM((tb, SSM_WIDTH), F32),
        xs=pltpu.VMEM((tb, SSM_WIDTH), F32),
        bc=pltpu.VMEM((tb, 2 * BC_WIDTH), BF16),
        bT=pltpu.VMEM((BC_WIDTH, tb), BF16),
        q=pltpu.VMEM((tb, RET_QK_WIDTH), BF16),
        qd=pltpu.VMEM((tb, RET_QK_WIDTH), BF16),
        k=pltpu.VMEM((tb, RET_QK_WIDTH), BF16),
        kdT=pltpu.VMEM((RET_QK_WIDTH, tb), BF16),
        v=pltpu.VMEM((tb, RET_WIDTH), BF16),
        gs=pltpu.VMEM((tb, RET_WIDTH), F32),
        dt=pltpu.VMEM((tb, LANES), F32),
        dtT=pltpu.VMEM((SSM_HEADS, tb), F32),
        acum=pltpu.VMEM((tb, LANES), F32),
        lrow=pltpu.VMEM((SSM_HEADS, tb), F32),
        ex=pltpu.VMEM((tb, 2 * SSM_WIDTH), F32),
    ))


def _mixer(x2d, batch, seq, cos, sin, pre, w_main, w_dt_pad, conv_w, conv_b, dt_bias, a_log, d_skip,
           ssm_norm, ret_norm, w_out, post):
    tb = min(MIX_BLOCK, seq)
    nj = seq // tb
    nblocks = batch * nj
    lane_pad = (0, LANES - DT_COPIES * SSM_HEADS)
    w_dt_t = w_dt_pad[:, 0:SSM_HEADS].T
    dtb_row = jnp.pad(jnp.tile(dt_bias, DT_COPIES), lane_pad)[None, :]
    alog_row = jnp.pad(jnp.tile(a_log, DT_COPIES), lane_pad)[None, :]
    dskip = jnp.repeat(d_skip, SSM_HEAD_DIM)[None, :]
    dmat, qdec, kdec, cdec = _retention_tables(tb)
    expand, tril, triu = _constant_matrices()

    g = MIX_GROUP
    assert nj % g == 0, "a grid step's blocks must not straddle two sequences"

    def block(k):
        return lambda i: (jnp.clip(g * i - g + 1 + k, 0, nblocks - 1), 0)

    prev = lambda i: (jnp.maximum(i - 1, 0), 0)
    operands = (
        [(x2d, pl.BlockSpec((tb, D_MODEL), block(k))) for k in range(g)]
        + [(cos, pl.BlockSpec((tb, LANES), block(k))) for k in range(g)]
        + [(sin, pl.BlockSpec((tb, LANES), block(k))) for k in range(g)])
    consts = [
        (pre[None, :], None),
        (w_main, _resident_block(w_main.shape)),
        (w_dt_pad, None),
        (w_dt_t, None),
        (conv_w, None),
        (conv_b[None, :], None),
        (dtb_row, None),
        (dt_bias[:, None], None),
        (alog_row, None),
        (a_log[:, None], None),
        (dskip, None),
        (ssm_norm[None, :], None),
        (ret_norm[None, :], None),
        (w_out, _resident_block(w_out.shape)),
        (post[None, :], None),
        (expand, None),
        (tril, None),
        (triu, None),
        (dmat, None),
        (qdec, None),
        (kdec, None),
        (cdec, None),
    ]
    assert len(consts) == N_MIX_CONSTS
    operands += consts
    args = [a for a, _ in operands]
    in_specs = [sp if sp is not None else _resident(a.shape) for a, sp in operands]
    return pl.pallas_call(
        functools.partial(_mixer_kernel, tb=tb, nj=nj),
        grid=(nblocks // g + 1,),
        in_specs=in_specs,
        out_specs=pl.BlockSpec((g * tb, D_MODEL), prev),
        out_shape=jax.ShapeDtypeStruct(x2d.shape, F32),
        scratch_shapes=[
            pltpu.VMEM((tb, D_MODEL), BF16),
            pltpu.VMEM((SUBLANES + tb, CONV_DIM), F32),
            pltpu.VMEM((CHUNK, SSM_WIDTH), F32),
            pltpu.VMEM((tb, D_INNER), BF16),
            pltpu.VMEM((tb, D_MODEL), F32),
            pltpu.VMEM((SSM_GROUPS, SSM_STATE, GROUP_WIDTH), F32),
            pltpu.VMEM((RET_HEADS, RET_QK_DIM, RET_V_DIM), F32),
        ] + 2 * _slot_scratch(tb),
        compiler_params=pltpu.CompilerParams(
            dimension_semantics=("arbitrary",), vmem_limit_bytes=VMEM_LIMIT),
        name="mixer",
    )(*args)


def kernel(x, positions, ffn1_pre_norm, ffn1_w_in, ffn1_w_out, ffn1_post_norm, mix_pre_norm, mix_w_in, conv_w, conv_b, dt_bias, a_log, d_skip, ssm_norm, ret_norm, mix_w_out, mix_post_norm, ffn2_pre_norm, ffn2_w_in, ffn2_w_out, ffn2_post_norm):
    batch, seq, _ = x.shape
    depth = ffn1_w_in.shape[0]

    def cast_job(w, layer):
        return _PrepJob(w, layer, _cast_kernel, w.shape[2])

    cos, sin, f1_in, f1_out = _rope_tables(positions, [cast_job(ffn1_w_in, 0), cast_job(ffn1_w_out, 0)])
    h = x.reshape(batch * seq, D_MODEL)
    for l in range(depth):
        jobs = [_PrepJob(mix_w_in, l, _mix_in_kernel, _G1), _PrepJob(mix_w_in, l, _mix_dt_kernel, LANES),
                cast_job(mix_w_out, l), cast_job(ffn2_w_in, l), cast_job(ffn2_w_out, l)]
        h, m_in, m_dt, m_out, f2_in, f2_out = _ffn(h, ffn1_pre_norm[l], f1_in, f1_out, ffn1_post_norm[l],
                                                   jobs)
        h = _mixer(h, batch, seq, cos, sin, mix_pre_norm[l], m_in, m_dt, conv_w[l], conv_b[l],
                   dt_bias[l], a_log[l], d_skip[l], ssm_norm[l], ret_norm[l], m_out, mix_post_norm[l])
        jobs = [cast_job(ffn1_w_in, l + 1), cast_job(ffn1_w_out, l + 1)] if l + 1 < depth else []
        h, *prepared = _ffn(h, ffn2_pre_norm[l], f2_in, f2_out, ffn2_post_norm[l], jobs)
        if prepared:
            f1_in, f1_out = prepared
    return h.reshape(batch, seq, D_MODEL)
```
